```python
import jax, jax.numpy as jnp
from jax import lax
import numpy as np

D_MODEL = 2048
BATCH = 8
SEQ = 2048
DEPTH = 1
DEC_BATCH = 128
DEC_SEQ = 1
PAST_LEN = 2048
PAGE_SIZE = 128

HEAD_DIM = 128
HEADS_PER_GROUP = 4
ATTN_GROUPS = ((128, 1), (512, 4), (2048, 16))
N_HEADS = HEADS_PER_GROUP * len(ATTN_GROUPS)
D_ATTN = N_HEADS * HEAD_DIM
D_ATTN_OUT = HEADS_PER_GROUP * HEAD_DIM
BAND = 128
QBLOCK = 128
POOL_WINDOWS = (2, 4, 8, 16)
POOL_GROUP = 128
D_POOL = POOL_GROUP * len(POOL_WINDOWS)
POOL_HIST = max(POOL_WINDOWS) - 1
D_FF = 4 * D_MODEL
D_IN = 3 * D_ATTN + D_POOL + 2 * D_MODEL
SPLITS = (D_ATTN, 2 * D_ATTN, 3 * D_ATTN, 3 * D_ATTN + D_POOL, 3 * D_ATTN + D_POOL + D_MODEL)
ALIBI_MAX_BIAS = 8.0
EPS = 1e-6

kernel_name = 'hybrid_dilated_attn_pool_adaln_step'


def rmsnorm(x, g):
    xf = x.astype(jnp.float32)
    y = xf * lax.rsqrt(jnp.mean(xf * xf, axis=-1, keepdims=True) + EPS)
    return (y * g.astype(jnp.float32)).astype(x.dtype)


def alibi_slopes():
    h = jnp.arange(1, N_HEADS + 1, dtype=jnp.float32)
    return jnp.exp2(-ALIBI_MAX_BIAS * h / N_HEADS)


def masked_softmax_attend(s, valid, v, eq):
    s = jnp.where(valid, s, -jnp.inf)
    m = jnp.max(s, axis=-1, keepdims=True)
    p = jnp.exp(s - m)
    den = jnp.sum(p, axis=-1, keepdims=True)
    out = jnp.einsum(eq, p / den, v.astype(jnp.float32))
    return out, (m + jnp.log(den))[..., 0]


def dilated_attn_prompt(q, k, v, dilation, slopes):
    B, S, H, hd = q.shape
    n = S // dilation
    nb = -(-n // QBLOCK)
    n_pad = nb * QBLOCK
    Z = B * dilation

    def to_sub(t):
        return t.reshape(B, n, dilation, H, hd).transpose(0, 2, 1, 3, 4).reshape(Z, n, H, hd)

    def band_keys(t):
        tp = jnp.pad(t, ((0, 0), (BAND, n_pad - n), (0, 0), (0, 0)))
        prev = tp[:, :n_pad].reshape(Z, nb, QBLOCK, H, hd)
        cur = tp[:, BAND:].reshape(Z, nb, QBLOCK, H, hd)
        return jnp.concatenate([prev, cur], axis=2)

    qb = jnp.pad(to_sub(q), ((0, 0), (0, n_pad - n), (0, 0), (0, 0))).reshape(Z, nb, QBLOCK, H, hd)
    kb = band_keys(to_sub(k))
    vb = band_keys(to_sub(v))
    s = jnp.einsum('znqhd,znkhd->znhqk', qb, kb, preferred_element_type=jnp.float32) * (hd ** -0.5)
    a = jnp.arange(QBLOCK)[:, None]
    b = jnp.arange(2 * QBLOCK)[None, :]
    dist = a - b + BAND
    u = (jnp.arange(nb) * QBLOCK)[:, None, None] + a[None]
    valid = (dist >= 0) & (dist <= BAND) & (u - dist >= 0)
    bias = -slopes[:, None, None] * (dist * dilation).astype(jnp.float32)
    out, lse = masked_softmax_attend(s + bias[None, None], valid[None, :, None], vb, 'znhqk,znkhd->znqhd')
    out = out.reshape(Z, n_pad, H, hd)[:, :n]
    lse = lse.transpose(0, 1, 3, 2).reshape(Z, n_pad, H)[:, :n]
    out = out.reshape(B, dilation, n, H, hd).transpose(0, 2, 1, 3, 4).reshape(B, S, H, hd)
    lse = lse.reshape(B, dilation, n, H).transpose(0, 2, 1, 3).reshape(B, S, H)
    return out, lse


def dilated_attn_sample(q, k_new, v_new, hist_kv, dilation, slopes):
    T, hd = q.shape[1], q.shape[-1]
    w_hist = hist_kv.shape[1]
    kvc = jnp.concatenate([hist_kv, jnp.stack([k_new, v_new], axis=2)], axis=1)
    i = jnp.arange(T)[:, None]
    j = jnp.arange(BAND + 1)[None, :]
    idx = w_hist + i - j * dilation
    valid = idx >= 0
    kvg = kvc[:, jnp.maximum(idx, 0)]
    s = jnp.einsum('bthd,btlhd->bthl', q, kvg[:, :, :, 0], preferred_element_type=jnp.float32) * (hd ** -0.5)
    bias = -slopes[:, None] * (j * dilation).astype(jnp.float32)
    out, lse = masked_softmax_attend(s + bias, valid[None, :, None, :], kvg[:, :, :, 1], 'bthl,btlhd->bthd')
    return out, lse, kvc


def pool_mixer(u_ext, pos0, w_pool, pool_scale):
    B, n_ext, _ = u_ext.shape
    n_new = n_ext - POOL_HIST
    cs = jnp.pad(jnp.cumsum(u_ext.astype(jnp.float32), axis=1), ((0, 0), (1, 0), (0, 0)))
    u_new = u_ext[:, POOL_HIST:].astype(jnp.float32)
    pos = pos0 + jnp.arange(n_new)
    outs = []
    for g, w in enumerate(POOL_WINDOWS):
        sl = slice(g * POOL_GROUP, (g + 1) * POOL_GROUP)
        win_sum = cs[:, POOL_HIST + 1:, sl] - cs[:, POOL_HIST + 1 - w:POOL_HIST + 1 - w + n_new, sl]
        count = jnp.minimum(pos + 1, w).astype(jnp.float32)[None, :, None]
        outs.append(win_sum / count - u_new[..., sl])
    z = jnp.stack(outs, axis=2)
    z = jnp.einsum('btgc,gcd->btgd', z, w_pool.astype(jnp.float32))
    return (z.reshape(B, n_new, D_POOL) * pool_scale.astype(jnp.float32)).astype(u_ext.dtype)


def trunk_layer(x, c, kv_hist, u_hist, pos0, norm_mix_g, w_ada, b_ada, w_in, w_up_attn, w_pool, pool_scale,
                w_up_pool, w_out, norm_mlp_g, w_mlp_up, w_mlp_down):
    B, T, _ = x.shape
    mod = jax.nn.silu(c) @ w_ada + b_ada
    sh1, sc1, gt1, sh2, sc2, gt2 = jnp.split(mod[:, None, :], 6, axis=-1)
    h = rmsnorm(x, norm_mix_g) * (1 + sc1) + sh1
    q, k, v, u, ga, gb = jnp.split(h @ w_in, SPLITS, axis=-1)
    q = q.reshape(B, T, N_HEADS, HEAD_DIM)
    k = k.reshape(B, T, N_HEADS, HEAD_DIM)
    v = v.reshape(B, T, N_HEADS, HEAD_DIM)
    slopes = alibi_slopes()
    outs, lses, new_kv = [], [], []
    for g, (window, dil) in enumerate(ATTN_GROUPS):
        hs = slice(g * HEADS_PER_GROUP, (g + 1) * HEADS_PER_GROUP)
        if kv_hist is None:
            o, l = dilated_attn_prompt(q[:, :, hs], k[:, :, hs], v[:, :, hs], dil, slopes[hs])
            keep = min(window, T)
            new_kv.append(jnp.stack([k[:, T - keep:, hs], v[:, T - keep:, hs]], axis=2))
        else:
            o, l, kvc = dilated_attn_sample(q[:, :, hs], k[:, :, hs], v[:, :, hs], kv_hist[g], dil, slopes[hs])
            keep = min(window, kvc.shape[1])
            new_kv.append(kvc[:, kvc.shape[1] - keep:])
        outs.append(o)
        lses.append(l)
    alpha = jax.nn.softmax(jnp.stack(lses, axis=0), axis=0)
    o_attn = jnp.sum(alpha[..., None] * jnp.stack(outs, axis=0), axis=0).reshape(B, T, D_ATTN_OUT).astype(x.dtype)
    if u_hist is None:
        u_hist = jnp.zeros((B, POOL_HIST, D_POOL), u.dtype)
    u_ext = jnp.concatenate([u_hist, u], axis=1)
    p = pool_mixer(u_ext, pos0, w_pool, pool_scale)
    new_u = u_ext[:, u_ext.shape[1] - POOL_HIST:]
    mix = jax.nn.sigmoid(ga) * (o_attn @ w_up_attn) + jax.nn.sigmoid(gb) * (p @ w_up_pool)
    x = x + gt1 * (mix @ w_out)
    h2 = rmsnorm(x, norm_mlp_g) * (1 + sc2) + sh2
    ff = jnp.square(jax.nn.relu(h2 @ w_mlp_up)) @ w_mlp_down
    x = x + gt2 * ff
    return x, new_kv, new_u


def setup_inputs(seed: int = 0) -> dict:
    key = jax.random.key(seed)
    ks = jax.random.split(key, 24)
    f32 = jnp.float32
    nrm = lambda k, shape, s: jax.random.normal(k, shape, f32) * s
    hist = [min(w, PAST_LEN) for w, _ in ATTN_GROUPS]
    return {
        'x_prompt': nrm(ks[0], (BATCH, SEQ, D_MODEL), 1.0),
        'x_sample': nrm(ks[1], (DEC_BATCH, DEC_SEQ, D_MODEL), 1.0),
        'c_prompt': nrm(ks[2], (BATCH, D_MODEL), 1.0),
        'c_sample': nrm(ks[3], (DEC_BATCH, D_MODEL), 1.0),
        'cache_kv_w128': nrm(ks[4], (DEPTH, DEC_BATCH, hist[0], 2, HEADS_PER_GROUP, HEAD_DIM), 1.0),
        'cache_kv_w512': nrm(ks[5], (DEPTH, DEC_BATCH, hist[1], 2, HEADS_PER_GROUP, HEAD_DIM), 1.0),
        'cache_kv_w2048': nrm(ks[6], (DEPTH, DEC_BATCH, hist[2], 2, HEADS_PER_GROUP, HEAD_DIM), 1.0),
        'state_pool': nrm(ks[7], (DEPTH, DEC_BATCH, POOL_HIST, D_POOL), 1.0),
        'norm_mix_g': 1.0 + nrm(ks[8], (DEPTH, D_MODEL), 0.05),
        'w_ada': nrm(ks[9], (DEPTH, D_MODEL, 6 * D_MODEL), D_MODEL ** -0.5),
        'b_ada': nrm(ks[10], (DEPTH, 6 * D_MODEL), 0.01),
        'w_in': nrm(ks[11], (DEPTH, D_MODEL, D_IN), D_MODEL ** -0.5),
        'w_up_attn': nrm(ks[12], (DEPTH, D_ATTN_OUT, D_MODEL), D_ATTN_OUT ** -0.5),
        'w_pool': nrm(ks[13], (DEPTH, len(POOL_WINDOWS), POOL_GROUP, POOL_GROUP), POOL_GROUP ** -0.5),
        'pool_scale': 1.0 + nrm(ks[14], (DEPTH, D_POOL), 0.05),
        'w_up_pool': nrm(ks[15], (DEPTH, D_POOL, D_MODEL), D_POOL ** -0.5),
        'w_out': nrm(ks[16], (DEPTH, D_MODEL, D_MODEL), D_MODEL ** -0.5),
        'norm_mlp_g': 1.0 + nrm(ks[17], (DEPTH, D_MODEL), 0.05),
        'w_mlp_up': nrm(ks[18], (DEPTH, D_MODEL, D_FF), D_MODEL ** -0.5),
        'w_mlp_down': nrm(ks[19], (DEPTH, D_FF, D_MODEL), D_FF ** -0.5),
        'norm_final_g': 1.0 + nrm(ks[20], (D_MODEL,), 0.05),
    }


def reference(x_prompt, x_sample, c_prompt, c_sample, cache_kv_w128, cache_kv_w512, cache_kv_w2048, state_pool,
              norm_mix_g, w_ada, b_ada, w_in, w_up_attn, w_pool, pool_scale, w_up_pool, w_out, norm_mlp_g,
              w_mlp_up, w_mlp_down, norm_final_g):
    xp, xs = x_prompt, x_sample
    kvp = ([], [], [])
    kvs = ([], [], [])
    pool_p, pool_s = [], []
    for l in range(DEPTH):
        params = (norm_mix_g[l], w_ada[l], b_ada[l], w_in[l], w_up_attn[l], w_pool[l], pool_scale[l],
                  w_up_pool[l], w_out[l], norm_mlp_g[l], w_mlp_up[l], w_mlp_down[l])
        xp, nkv_p, nu_p = trunk_layer(xp, c_prompt, None, None, 0, *params)
        hist = (cache_kv_w128[l], cache_kv_w512[l], cache_kv_w2048[l])
        xs, nkv_s, nu_s = trunk_layer(xs, c_sample, hist, state_pool[l], PAST_LEN, *params)
        for g in range(len(ATTN_GROUPS)):
            kvp[g].append(nkv_p[g])
            kvs[g].append(nkv_s[g])
        pool_p.append(nu_p)
        pool_s.append(nu_s)
    y_prompt = rmsnorm(xp, norm_final_g)
    y_sample = rmsnorm(xs, norm_final_g)
    return (y_prompt, y_sample,
            jnp.stack(kvp[0]), jnp.stack(kvp[1]), jnp.stack(kvp[2]), jnp.stack(pool_p),
            jnp.stack(kvs[0]), jnp.stack(kvs[1]), jnp.stack(kvs[2]), jnp.stack(pool_s))
```

```python
import functools

import jax
import jax.numpy as jnp
from jax import lax
from jax.experimental import pallas as pl
from jax.experimental.pallas import tpu as pltpu

F32 = jnp.float32
BF16 = jnp.bfloat16

D_MODEL = 2048
HEAD_DIM = 128
HEADS_PER_GROUP = 4
ATTN_GROUPS = ((128, 1), (512, 4), (2048, 16))
N_GROUPS = len(ATTN_GROUPS)
N_HEADS = HEADS_PER_GROUP * N_GROUPS
D_ATTN = N_HEADS * HEAD_DIM
D_GROUP = HEADS_PER_GROUP * HEAD_DIM
BAND = 128
POOL_WINDOWS = (2, 4, 8, 16)
POOL_GROUP = 128
D_POOL = POOL_GROUP * len(POOL_WINDOWS)
POOL_HIST = max(POOL_WINDOWS) - 1
POOL_HALO = 16
D_FF = 4 * D_MODEL
D_IN = 3 * D_ATTN + D_POOL + 2 * D_MODEL
ALIBI_MAX_BIAS = 8.0
EPS = 1e-6
SCALE = HEAD_DIM ** -0.5
NEG = -1e30

VMEM_LIMIT_BYTES = 56 * 1024 * 1024

IN_TILE = D_GROUP
N_IN_TILES = D_IN // IN_TILE
QKV_TILES = 3 * N_GROUPS
POOL_TILE = QKV_TILES
GATE_TILE0 = POOL_TILE + 1

TM_IN = 1024
TM_MIX = 256
TM_MLP = 512
TF_MLP = 1024
BS_ATTN = 8


def _cparams(*sem):
    return pltpu.CompilerParams(dimension_semantics=sem, vmem_limit_bytes=VMEM_LIMIT_BYTES)


def _rms(x):
    return x * lax.rsqrt(jnp.mean(x * x, axis=-1, keepdims=True) + EPS)


def _ada_kernel(c_ref, w_ref, b_ref, o_ref):
    c = c_ref[...]
    a = (c * jax.nn.sigmoid(c)).astype(BF16)
    o_ref[...] = jnp.dot(a, w_ref[...].astype(BF16), preferred_element_type=F32) + b_ref[...]


def _ada(c_all, w_ada, b_ada):
    m = c_all.shape[0]
    tn = 1024
    return pl.pallas_call(
        _ada_kernel,
        grid=(6 * D_MODEL // tn,),
        in_specs=[pl.BlockSpec((m, D_MODEL), lambda j: (0, 0)),
                  pl.BlockSpec((D_MODEL, tn), lambda j: (0, j)),
                  pl.BlockSpec((1, tn), lambda j: (0, j))],
        out_specs=pl.BlockSpec((m, tn), lambda j: (0, j)),
        out_shape=jax.ShapeDtypeStruct((m, 6 * D_MODEL), F32),
        compiler_params=_cparams("arbitrary"),
        name="ada",
    )(c_all, w_ada, b_ada.reshape(1, 6 * D_MODEL))


def _store_heads(ref, val):
    for h in range(HEADS_PER_GROUP):
        ref[:, h, :] = val[:, h * HEAD_DIM:(h + 1) * HEAD_DIM]


def _inproj_p_kernel(x_ref, sh_ref, sc_ref, g_ref, w_ref,
                     qkv0_ref, qkv1_ref, qkv2_ref, u_ref, gate_ref, kv2_ref, kv1_ref, kv0_ref,
                     h_scr, acc_scr, *, tiles_per_batch):
    j = pl.program_id(1)
    tm = x_ref.shape[0]

    @pl.when(j == 0)
    def _():
        y = _rms(x_ref[...]) * g_ref[...]
        h_scr[...] = (y * (1.0 + sc_ref[...]) + sh_ref[...]).astype(BF16)

    acc = jnp.dot(h_scr[...], w_ref[...], preferred_element_type=F32)

    for g, (qkv_ref, (_, dil)) in enumerate(zip((qkv0_ref, qkv1_ref, qkv2_ref), ATTN_GROUPS)):
        @pl.when((j == g) | (j == N_GROUPS + g) | (j == 2 * N_GROUPS + g))
        def _(qkv_ref=qkv_ref, dil=dil):
            if dil == 1:
                qkv_ref[0] = acc.astype(BF16)
            else:
                for h in range(HEADS_PER_GROUP):
                    sl = slice(h * HEAD_DIM, (h + 1) * HEAD_DIM)
                    acc_scr[h] = acc[:, sl]
                    for r in range(dil):
                        qkv_ref[r, :, sl] = acc_scr[h, pl.ds(r, tm // dil, stride=dil), :].astype(BF16)

    @pl.when(j == POOL_TILE)
    def _():
        u_ref[...] = acc

    @pl.when(j >= GATE_TILE0)
    def _():
        gate_ref[...] = jax.nn.sigmoid(acc).astype(BF16)

    @pl.when((j == N_GROUPS + 2) | (j == 2 * N_GROUPS + 2))
    def _():
        _store_heads(kv2_ref, acc)

    last = pl.program_id(0) % tiles_per_batch == tiles_per_batch - 1

    @pl.when(last & ((j == N_GROUPS + 1) | (j == 2 * N_GROUPS + 1)))
    def _():
        _store_heads(kv1_ref, acc[tm - kv1_ref.shape[0]:, :])

    @pl.when(last & ((j == N_GROUPS) | (j == 2 * N_GROUPS)))
    def _():
        _store_heads(kv0_ref, acc[tm - kv0_ref.shape[0]:, :])


def _inproj_prompt(x2d, mod_p, norm_g, w_in, batch, seq):
    m = x2d.shape[0]
    tm = TM_IN
    tpb = seq // tm
    keep = [min(w, seq) for w, _ in ATTN_GROUPS]
    assert keep[2] == seq and tm >= keep[1] and seq % tm == 0

    def sub_spec(g):
        dil = ATTN_GROUPS[g][1]
        return pl.BlockSpec((None, None, dil, tm // dil, D_GROUP),
                            lambda mi, j: (jnp.clip((j - g) // N_GROUPS, 0, 2), mi // tpb, 0, mi % tpb, 0))

    def sub_shape(g):
        dil = ATTN_GROUPS[g][1]
        return jax.ShapeDtypeStruct((3, batch, dil, seq // dil, D_GROUP), BF16)

    def kv_spec(rows, first_v_tile, tail_only):
        def index(mi, j):
            is_v = j >= first_v_tile
            if tail_only:
                return (0, mi // tpb, 0, jnp.where(is_v & (mi % tpb == tpb - 1), 1, 0), 0, 0)
            return (0, mi // tpb, mi % tpb, jnp.where(is_v, 1, 0), 0, 0)
        return pl.BlockSpec((None, None, rows, None, HEADS_PER_GROUP, HEAD_DIM), index)

    def kv_shape(rows):
        return jax.ShapeDtypeStruct((1, batch, rows, 2, HEADS_PER_GROUP, HEAD_DIM), F32)

    in_specs = [
        pl.BlockSpec((tm, D_MODEL), lambda mi, j: (mi, 0)),
        pl.BlockSpec((None, 1, D_MODEL), lambda mi, j: (mi // tpb, 0, 0)),
        pl.BlockSpec((None, 1, D_MODEL), lambda mi, j: (mi // tpb, 0, 1)),
        pl.BlockSpec((1, D_MODEL), lambda mi, j: (0, 0)),
        pl.BlockSpec((D_MODEL, IN_TILE), lambda mi, j: (0, j)),
    ]
    out_specs = [
        sub_spec(0), sub_spec(1), sub_spec(2),
        pl.BlockSpec((tm, IN_TILE), lambda mi, j: (mi, 0)),
        pl.BlockSpec((tm, IN_TILE), lambda mi, j: (mi, jnp.clip(j - GATE_TILE0, 0, N_IN_TILES - GATE_TILE0 - 1))),
        kv_spec(tm, 2 * N_GROUPS + 2, False),
        kv_spec(keep[1], 2 * N_GROUPS + 1, True),
        kv_spec(keep[0], 2 * N_GROUPS, True),
    ]
    out_shape = [
        sub_shape(0), sub_shape(1), sub_shape(2),
        jax.ShapeDtypeStruct((m, D_POOL), F32),
        jax.ShapeDtypeStruct((m, 2 * D_MODEL), BF16),
        kv_shape(keep[2]), kv_shape(keep[1]), kv_shape(keep[0]),
    ]
    return pl.pallas_call(
        functools.partial(_inproj_p_kernel, tiles_per_batch=tpb),
        grid=(m // tm, N_IN_TILES),
        in_specs=in_specs, out_specs=out_specs, out_shape=out_shape,
        scratch_shapes=[pltpu.VMEM((tm, D_MODEL), BF16), pltpu.VMEM((HEADS_PER_GROUP, tm, HEAD_DIM), F32)],
        compiler_params=_cparams("arbitrary", "arbitrary"),
        name="inproj_prompt",
    )(x2d, mod_p, mod_p, norm_g, w_in)


def _attn_block(q, k, v, bias_fn):
    lane = lax.broadcasted_iota(jnp.int32, (BAND, HEAD_DIM), 1)
    outs = []
    lse = jnp.zeros((BAND, HEAD_DIM), F32)
    for h in range(HEADS_PER_GROUP):
        sl = slice(h * HEAD_DIM, (h + 1) * HEAD_DIM)
        s = lax.dot_general(q[:, sl], k[:, sl], (((1,), (1,)), ((), ())), preferred_element_type=F32)
        s = s * SCALE + bias_fn(h)
        mx = jnp.max(s, axis=-1, keepdims=True)
        p = jnp.exp(s - mx)
        den = jnp.sum(p, axis=-1, keepdims=True)
        outs.append(jnp.dot(p.astype(BF16), v[:, sl], preferred_element_type=F32) / den)
        lse = jnp.where(lane == h, mx + jnp.log(den), lse)
    return outs, lse


def _attn_p_kernel(bias_ref, q_ref, k_ref, v_ref, o_ref, l_ref, o_scr):
    dil, n, _ = q_ref.shape
    nb = n // BAND

    def rows(start):
        return pl.ds(start, BAND) if dil == 1 else pl.ds(start, BAND, stride=dil)

    def put(start, outs, lse):
        for h in range(HEADS_PER_GROUP):
            o_scr[h, rows(start), :] = outs[h]
        l_ref[rows(start), :] = lse

    for r in range(dil):
        put(r, *_attn_block(q_ref[r, 0:BAND, :], k_ref[r, 0:BAND, :], v_ref[r, 0:BAND, :],
                            lambda h: bias_ref[h, :, BAND:2 * BAND]))

        if nb > 1:
            def body(i, carry, r=r):
                r0 = pl.multiple_of(i * BAND, BAND)
                rk = pl.multiple_of((i - 1) * BAND, BAND)
                put(r0 * dil + r, *_attn_block(q_ref[r, pl.ds(r0, BAND), :], k_ref[r, pl.ds(rk, 2 * BAND), :],
                                               v_ref[r, pl.ds(rk, 2 * BAND), :], lambda h: bias_ref[h]))
                return carry

            lax.fori_loop(1, nb, body, 0)

    for h in range(HEADS_PER_GROUP):
        o_ref[:, h * HEAD_DIM:(h + 1) * HEAD_DIM] = o_scr[h].astype(BF16)


def _attn_prompt_group(qkv_sub, bias, g, batch, seq):
    dil = ATTN_GROUPS[g][1]
    n = seq // dil

    def sub(which):
        return pl.BlockSpec((None, None, dil, n, D_GROUP), lambda b: (which, b, 0, 0, 0))

    return pl.pallas_call(
        _attn_p_kernel,
        grid=(batch,),
        in_specs=[pl.BlockSpec((HEADS_PER_GROUP, BAND, 2 * BAND), lambda b: (0, 0, 0)),
                  sub(0), sub(1), sub(2)],
        out_specs=[pl.BlockSpec((seq, D_GROUP), lambda b: (b, 0)),
                   pl.BlockSpec((seq, HEAD_DIM), lambda b: (b, 0))],
        out_shape=[jax.ShapeDtypeStruct((batch * seq, D_GROUP), BF16),
                   jax.ShapeDtypeStruct((batch * seq, HEAD_DIM), F32)],
        scratch_shapes=[pltpu.VMEM((HEADS_PER_GROUP, seq, HEAD_DIM), F32)],
        compiler_params=_cparams("arbitrary"),
        name=f"attn_prompt_g{g}",
    )(bias, qkv_sub, qkv_sub, qkv_sub)


def _merge_heads(outs, lses):
    cols = []
    for h in range(HEADS_PER_GROUP):
        sl = slice(h * HEAD_DIM, (h + 1) * HEAD_DIM)
        l = [lg[:, h:h + 1] for lg in lses]
        mx = jnp.maximum(jnp.maximum(l[0], l[1]), l[2])
        e = [jnp.exp(x - mx) for x in l]
        tot = e[0] + e[1] + e[2]
        cols.append(sum((e[g] / tot) * outs[g][:, sl].astype(F32) for g in range(N_GROUPS)))
    return jnp.concatenate(cols, axis=-1)


def _mix_tail(o_attn, p, gates_a, gates_b, x, gt1, sh2, sc2, wua_ref, wup_ref, wo_ref, g2_ref, x1_ref, h2_ref):
    a = jnp.dot(o_attn.astype(BF16), wua_ref[...], preferred_element_type=F32)
    b = jnp.dot(p.astype(BF16), wup_ref[...], preferred_element_type=F32)
    mix = (gates_a * a + gates_b * b).astype(BF16)
    x1 = x + gt1 * jnp.dot(mix, wo_ref[...], preferred_element_type=F32)
    x1_ref[...] = x1
    h2_ref[...] = (_rms(x1) * g2_ref[...] * (1.0 + sc2) + sh2).astype(BF16)


def _pool_project(z_groups, wp_ref, ps_ref):
    cols = [jnp.dot(z.astype(BF16), wp_ref[g], preferred_element_type=F32) for g, z in enumerate(z_groups)]
    return jnp.concatenate(cols, axis=-1) * ps_ref[...]


def _mix_p_kernel(o0_ref, o1_ref, o2_ref, l0_ref, l1_ref, l2_ref, u_ref, uh_ref, gate_ref, x_ref,
                  gt1_ref, sh2_ref, sc2_ref, wua_ref, wp_ref, ps_ref, wup_ref, wo_ref, g2_ref,
                  x1_ref, h2_ref, ext_scr, *, tiles_per_batch):
    tm = u_ref.shape[0]
    t = pl.program_id(0) % tiles_per_batch
    o_attn = _merge_heads([o0_ref[...], o1_ref[...], o2_ref[...]], [l0_ref[...], l1_ref[...], l2_ref[...]])

    u = u_ref[...]
    ext_scr[0:POOL_HALO, :] = jnp.where(t == 0, 0.0, uh_ref[...])
    ext_scr[POOL_HALO:, :] = u
    pos = t * tm + lax.broadcasted_iota(jnp.int32, (tm, 1), 0)
    zs = []
    for g, w in enumerate(POOL_WINDOWS):
        sl = slice(g * POOL_GROUP, (g + 1) * POOL_GROUP)
        win = u[:, sl]
        for jj in range(1, w):
            win = win + ext_scr[POOL_HALO - jj:POOL_HALO - jj + tm, sl]
        cnt = jnp.minimum(pos + 1, w).astype(F32)
        zs.append(win / cnt - u[:, sl])
    p = _pool_project(zs, wp_ref, ps_ref)

    gates = gate_ref[...]
    _mix_tail(o_attn, p, gates[:, :D_MODEL].astype(F32), gates[:, D_MODEL:].astype(F32), x_ref[...],
              gt1_ref[...], sh2_ref[...], sc2_ref[...], wua_ref, wup_ref, wo_ref, g2_ref, x1_ref, h2_ref)


def _const_spec(shape):
    nd = len(shape)
    return pl.BlockSpec(shape, lambda *_: (0,) * nd)


def _mix_prompt(outs, lses, u, gates, x2d, mod_p, w_up_attn, w_pool, pool_scale, w_up_pool, w_out, norm_g2, seq):
    m = x2d.shape[0]
    tm = TM_MIX
    tpb = seq // tm
    hb = tm // POOL_HALO

    def row(width):
        return pl.BlockSpec((tm, width), lambda mi: (mi, 0))

    def mod(c):
        return pl.BlockSpec((None, 1, D_MODEL), lambda mi: (mi // tpb, 0, c))

    in_specs = ([row(D_GROUP)] * 3 + [row(HEAD_DIM)] * 3 + [
        row(D_POOL),
        pl.BlockSpec((POOL_HALO, D_POOL), lambda mi: (jnp.maximum(mi * hb - 1, 0), 0)),
        row(2 * D_MODEL), row(D_MODEL), mod(2), mod(3), mod(4),
        _const_spec((D_GROUP, D_MODEL)), _const_spec((len(POOL_WINDOWS), POOL_GROUP, POOL_GROUP)),
        _const_spec((1, D_POOL)), _const_spec((D_POOL, D_MODEL)), _const_spec((D_MODEL, D_MODEL)),
        _const_spec((1, D_MODEL))])
    return pl.pallas_call(
        functools.partial(_mix_p_kernel, tiles_per_batch=tpb),
        grid=(m // tm,),
        in_specs=in_specs,
        out_specs=[row(D_MODEL), row(D_MODEL)],
        out_shape=[jax.ShapeDtypeStruct((m, D_MODEL), F32), jax.ShapeDtypeStruct((m, D_MODEL), BF16)],
        scratch_shapes=[pltpu.VMEM((POOL_HALO + tm, D_POOL), F32)],
        compiler_params=_cparams("arbitrary"),
        name="mix_prompt",
    )(*outs, *lses, u, u, gates, x2d, mod_p, mod_p, mod_p, w_up_attn, w_pool, pool_scale, w_up_pool, w_out, norm_g2)


def _mlp_kernel(h_ref, wu_ref, wd_ref, x1_ref, gt_ref, gf_ref, y_ref, acc_ref):
    f = pl.program_id(1)
    a = jnp.dot(h_ref[...], wu_ref[...], preferred_element_type=F32)
    a = jnp.square(jnp.maximum(a, 0.0)).astype(BF16)
    part = jnp.dot(a, wd_ref[...], preferred_element_type=F32)

    @pl.when(f == 0)
    def _():
        acc_ref[...] = part

    @pl.when(f > 0)
    def _():
        acc_ref[...] += part

    @pl.when(f == pl.num_programs(1) - 1)
    def _():
        x2 = x1_ref[...] + gt_ref[...] * acc_ref[...]
        y_ref[...] = _rms(x2) * gf_ref[...]


def _mlp(h2, x1, gt_spec, gt_arr, w_up, w_down, norm_gf, tm, name):
    m = h2.shape[0]
    return pl.pallas_call(
        _mlp_kernel,
        grid=(m // tm, D_FF // TF_MLP),
        in_specs=[pl.BlockSpec((tm, D_MODEL), lambda mi, f: (mi, 0)),
                  pl.BlockSpec((D_MODEL, TF_MLP), lambda mi, f: (0, f)),
                  pl.BlockSpec((TF_MLP, D_MODEL), lambda mi, f: (f, 0)),
                  pl.BlockSpec((tm, D_MODEL), lambda mi, f: (mi, 0)),
                  gt_spec,
                  pl.BlockSpec((1, D_MODEL), lambda mi, f: (0, 0))],
        out_specs=pl.BlockSpec((tm, D_MODEL), lambda mi, f: (mi, 0)),
        out_shape=jax.ShapeDtypeStruct((m, D_MODEL), F32),
        scratch_shapes=[pltpu.VMEM((tm, D_MODEL), F32)],
        compiler_params=_cparams("arbitrary", "arbitrary"),
        name=name,
    )(h2, w_up, w_down, x1, gt_arr, norm_gf)


def _inproj_s_kernel(x_ref, sh_ref, sc_ref, g_ref, w_ref, o_ref, h_scr):
    @pl.when(pl.program_id(0) == 0)
    def _():
        y = _rms(x_ref[...]) * g_ref[...]
        h_scr[...] = (y * (1.0 + sc_ref[...]) + sh_ref[...]).astype(BF16)

    o_ref[...] = jnp.dot(h_scr[...], w_ref[...], preferred_element_type=F32)


def _inproj_sample(x_s, mod_s, norm_g, w_in):
    m = x_s.shape[0]
    tn = D_ATTN
    return pl.pallas_call(
        _inproj_s_kernel,
        grid=(D_IN // tn,),
        in_specs=[pl.BlockSpec((m, D_MODEL), lambda j: (0, 0)),
                  pl.BlockSpec((m, D_MODEL), lambda j: (0, 0)),
                  pl.BlockSpec((m, D_MODEL), lambda j: (0, 1)),
                  pl.BlockSpec((1, D_MODEL), lambda j: (0, 0)),
                  pl.BlockSpec((D_MODEL, tn), lambda j: (0, j))],
        out_specs=pl.BlockSpec((m, tn), lambda j: (0, j)),
        out_shape=jax.ShapeDtypeStruct((m, D_IN), F32),
        scratch_shapes=[pltpu.VMEM((m, D_MODEL), BF16)],
        compiler_params=_cparams("arbitrary"),
        name="inproj_sample",
    )(x_s, mod_s, mod_s, norm_g, w_in)


def _attn_s_kernel(bias_ref, q_ref, kn_ref, vn_ref, k0_ref, v0_ref, k1_ref, v1_ref, k2_ref, v2_ref, o_ref):
    k_refs = (k0_ref, k1_ref, k2_ref)
    v_refs = (v0_ref, v1_ref, v2_ref)

    def body(s, carry):
        outs, lses = [], []
        for g in range(N_GROUPS):
            hs = slice(g * HEADS_PER_GROUP, (g + 1) * HEADS_PER_GROUP)
            q = q_ref[s, hs, :]
            kn = kn_ref[s, hs, :]
            vn = vn_ref[s, hs, :]
            kt = k_refs[g][s]
            vt = v_refs[g][s]
            sh = jnp.sum(kt * q[None], axis=-1, keepdims=True) * SCALE + bias_ref[g]
            sn = jnp.sum(kn * q, axis=-1, keepdims=True) * SCALE
            mx = jnp.maximum(jnp.max(sh, axis=0), sn)
            p = jnp.exp(sh - mx[None])
            pn = jnp.exp(sn - mx)
            den = jnp.sum(p, axis=0) + pn
            num = jnp.sum(p * vt, axis=0) + pn * vn
            outs.append(num / den)
            lses.append(mx + jnp.log(den))
        mx = jnp.maximum(jnp.maximum(lses[0], lses[1]), lses[2])
        e = [jnp.exp(l - mx) for l in lses]
        tot = e[0] + e[1] + e[2]
        o_ref[s] = (e[0] * outs[0] + e[1] * outs[1] + e[2] * outs[2]) / tot
        return carry

    lax.fori_loop(0, q_ref.shape[0], body, 0)


def _attn_sample(bias_s, q3, kn3, vn3, caches):
    nb = q3.shape[0]
    bs = BS_ATTN
    tok = pl.BlockSpec((bs, N_HEADS, HEAD_DIM), lambda i: (i, 0, 0))
    in_specs = [_const_spec((N_GROUPS, BAND, HEADS_PER_GROUP, HEAD_DIM)), tok, tok, tok]
    args = [bias_s, q3, kn3, vn3]
    for c, (_, dil) in zip(caches, ATTN_GROUPS):
        cv = c.reshape(nb, BAND, dil * 2, HEADS_PER_GROUP, HEAD_DIM)
        for kv in range(2):
            in_specs.append(pl.BlockSpec((bs, BAND, None, HEADS_PER_GROUP, HEAD_DIM),
                                         lambda i, kv=kv: (i, 0, kv, 0, 0)))
            args.append(cv)
    return pl.pallas_call(
        _attn_s_kernel,
        grid=(nb // bs,),
        in_specs=in_specs,
        out_specs=pl.BlockSpec((bs, HEADS_PER_GROUP, HEAD_DIM), lambda i: (i, 0, 0)),
        out_shape=jax.ShapeDtypeStruct((nb, HEADS_PER_GROUP, HEAD_DIM), F32),
        compiler_params=_cparams("arbitrary"),
        name="attn_sample",
    )(*args)


def _mix_s_kernel(o_ref, u_ref, hist_ref, ga_ref, gb_ref, x_ref, gt1_ref, sh2_ref, sc2_ref,
                  wua_ref, wp_ref, ps_ref, wup_ref, wo_ref, g2_ref, x1_ref, h2_ref, np_ref):
    u = u_ref[...]
    zs = []
    for g, w in enumerate(POOL_WINDOWS):
        sl = slice(g * POOL_GROUP, (g + 1) * POOL_GROUP)
        win = u[:, sl]
        for jj in range(1, w):
            win = win + hist_ref[:, POOL_HIST - jj, sl]
        zs.append(win / float(w) - u[:, sl])
    p = _pool_project(zs, wp_ref, ps_ref)
    np_ref[:, 0:POOL_HIST - 1, :] = hist_ref[:, 1:POOL_HIST, :]
    np_ref[:, POOL_HIST - 1, :] = u
    _mix_tail(o_ref[...], p, jax.nn.sigmoid(ga_ref[...]), jax.nn.sigmoid(gb_ref[...]), x_ref[...],
              gt1_ref[...], sh2_ref[...], sc2_ref[...], wua_ref, wup_ref, wo_ref, g2_ref, x1_ref, h2_ref)


def _mix_sample(o_attn, proj_s, hist, x_s, mod_s, w_up_attn, w_pool, pool_scale, w_up_pool, w_out, norm_g2):
    m = x_s.shape[0]

    def cols(width, c):
        return pl.BlockSpec((m, width), lambda i: (0, c))

    u0 = 3 * D_ATTN
    u = lax.slice_in_dim(proj_s, u0, u0 + D_POOL, axis=1)
    ga = lax.slice_in_dim(proj_s, u0 + D_POOL, u0 + D_POOL + D_MODEL, axis=1)
    gb = lax.slice_in_dim(proj_s, u0 + D_POOL + D_MODEL, D_IN, axis=1)
    in_specs = [cols(D_GROUP, 0), cols(D_POOL, 0), _const_spec((m, POOL_HIST, D_POOL)),
                cols(D_MODEL, 0), cols(D_MODEL, 0),
                cols(D_MODEL, 0), cols(D_MODEL, 2), cols(D_MODEL, 3), cols(D_MODEL, 4),
                _const_spec((D_GROUP, D_MODEL)), _const_spec((len(POOL_WINDOWS), POOL_GROUP, POOL_GROUP)),
                _const_spec((1, D_POOL)), _const_spec((D_POOL, D_MODEL)), _const_spec((D_MODEL, D_MODEL)),
                _const_spec((1, D_MODEL))]
    return pl.pallas_call(
        _mix_s_kernel,
        grid=(1,),
        in_specs=in_specs,
        out_specs=[_const_spec((m, D_MODEL)), _const_spec((m, D_MODEL)), _const_spec((m, POOL_HIST, D_POOL))],
        out_shape=[jax.ShapeDtypeStruct((m, D_MODEL), F32), jax.ShapeDtypeStruct((m, D_MODEL), BF16),
                   jax.ShapeDtypeStruct((m, POOL_HIST, D_POOL), F32)],
        compiler_params=_cparams("arbitrary"),
        name="mix_sample",
    )(o_attn, u, hist, ga, gb, x_s, mod_s, mod_s, mod_s, w_up_attn, w_pool, pool_scale, w_up_pool, w_out, norm_g2)


def _shift_copies(c_refs, n_refs, o_refs, sems):
    copies = []
    for i, (c, nw, o) in enumerate(zip(c_refs, n_refs, o_refs)):
        w = c.shape[2]
        copies.append(pltpu.make_async_copy(c.at[0, :, pl.ds(1, w - 1)], o.at[0, :, pl.ds(0, w - 1)], sems.at[2 * i]))
        copies.append(pltpu.make_async_copy(nw, o.at[0, :, w - 1], sems.at[2 * i + 1]))
    return copies


def _shift_kernel(c0, c1, c2, n0, n1, n2, o0, o1, o2, sems):
    copies = _shift_copies((c0, c1, c2), (n0, n1, n2), (o0, o1, o2), sems)
    for cp in copies:
        cp.start()
    for cp in copies:
        cp.wait()


def _shift_caches(caches, new_rows):
    any_spec = pl.BlockSpec(memory_space=pl.ANY)
    return pl.pallas_call(
        _shift_kernel,
        in_specs=[any_spec] * 6,
        out_specs=[any_spec] * 3,
        out_shape=[jax.ShapeDtypeStruct(c.shape, c.dtype) for c in caches],
        scratch_shapes=[pltpu.SemaphoreType.DMA((6,))],
        name="shift_caches",
    )(*caches, *new_rows)


def _alibi_slopes():
    h = jnp.arange(1, N_HEADS + 1, dtype=F32)
    return jnp.exp2(-ALIBI_MAX_BIAS * h / N_HEADS)


def _prompt_bias(g):
    dil = ATTN_GROUPS[g][1]
    slopes = _alibi_slopes()[g * HEADS_PER_GROUP:(g + 1) * HEADS_PER_GROUP]
    a = jnp.arange(BAND)[:, None]
    b = jnp.arange(2 * BAND)[None, :]
    dist = a - b + BAND
    valid = (dist >= 0) & (dist <= BAND)
    bias = -slopes[:, None, None] * (dist * dil).astype(F32)
    return jnp.where(valid[None], bias, NEG)


def _sample_bias():
    slopes = _alibi_slopes().reshape(N_GROUPS, 1, HEADS_PER_GROUP, 1)
    dil = jnp.array([d for _, d in ATTN_GROUPS], F32).reshape(N_GROUPS, 1, 1, 1)
    back = (BAND - jnp.arange(BAND, dtype=F32)).reshape(1, BAND, 1, 1)
    return jnp.broadcast_to(-slopes * (back * dil), (N_GROUPS, BAND, HEADS_PER_GROUP, HEAD_DIM))


def kernel(x_prompt, x_sample, c_prompt, c_sample, cache_kv_w128, cache_kv_w512, cache_kv_w2048, state_pool,
           norm_mix_g, w_ada, b_ada, w_in, w_up_attn, w_pool, pool_scale, w_up_pool, w_out, norm_mlp_g,
           w_mlp_up, w_mlp_down, norm_final_g):
    batch, seq, _ = x_prompt.shape
    nb = x_sample.shape[0]
    depth = w_in.shape[0]
    assert depth == 1 and x_sample.shape[1] == 1
    caches = (cache_kv_w128, cache_kv_w512, cache_kv_w2048)
    for c, (w, dil) in zip(caches, ATTN_GROUPS):
        assert c.shape[2] == w == BAND * dil

    w_in_b = w_in[0].astype(BF16)
    w_ua_b = w_up_attn[0].astype(BF16)
    w_pool_b = w_pool[0].astype(BF16)
    w_up_b = w_up_pool[0].astype(BF16)
    w_out_b = w_out[0].astype(BF16)
    w_mu_b = w_mlp_up[0].astype(BF16)
    w_md_b = w_mlp_down[0].astype(BF16)
    g1 = norm_mix_g[0].reshape(1, D_MODEL)
    g2 = norm_mlp_g[0].reshape(1, D_MODEL)
    gf = norm_final_g.reshape(1, D_MODEL)
    ps = pool_scale[0].reshape(1, D_POOL)

    mod = _ada(jnp.concatenate([c_sample, c_prompt], axis=0), w_ada[0], b_ada[0])
    mod_s = mod[:nb]
    mod_p = mod[nb:].reshape(batch, 1, 6 * D_MODEL)

    xs = x_sample.reshape(nb, D_MODEL)
    proj_s = _inproj_sample(xs, mod_s, g1, w_in_b)
    q3 = proj_s[:, 0:D_ATTN].reshape(nb, N_HEADS, HEAD_DIM)
    kn3 = proj_s[:, D_ATTN:2 * D_ATTN].reshape(nb, N_HEADS, HEAD_DIM)
    vn3 = proj_s[:, 2 * D_ATTN:3 * D_ATTN].reshape(nb, N_HEADS, HEAD_DIM)
    o_s = _attn_sample(_sample_bias(), q3, kn3, vn3, [c[0] for c in caches])
    new_rows = [jnp.stack([kn3[:, g * HEADS_PER_GROUP:(g + 1) * HEADS_PER_GROUP],
                           vn3[:, g * HEADS_PER_GROUP:(g + 1) * HEADS_PER_GROUP]], axis=1)
                for g in range(N_GROUPS)]
    kv_s = _shift_caches(caches, new_rows)
    x1_s, h2_s, pool_s = _mix_sample(o_s.reshape(nb, D_GROUP), proj_s, state_pool[0], xs, mod_s,
                                     w_ua_b, w_pool_b, ps, w_up_b, w_out_b, g2)
    y_s = _mlp(h2_s, x1_s, pl.BlockSpec((nb, D_MODEL), lambda mi, f: (0, 5)), mod_s, w_mu_b, w_md_b, gf,
               nb, "mlp_sample")

    x2d = x_prompt.reshape(batch * seq, D_MODEL)
    *qkv_sub, u_p, gates, kv2, kv1, kv0 = _inproj_prompt(x2d, mod_p, g1, w_in_b, batch, seq)
    outs, lses = [], []
    for g in range(N_GROUPS):
        o, l = _attn_prompt_group(qkv_sub[g], _prompt_bias(g), g, batch, seq)
        outs.append(o)
        lses.append(l)
    x1_p, h2_p = _mix_prompt(outs, lses, u_p, gates, x2d, mod_p, w_ua_b, w_pool_b, ps, w_up_b, w_out_b, g2, seq)
    tpb = seq // TM_MLP
    y_p = _mlp(h2_p, x1_p, pl.BlockSpec((None, 1, D_MODEL), lambda mi, f: (mi // tpb, 0, 5)), mod_p,
               w_mu_b, w_md_b, gf, TM_MLP, "mlp_prompt")

    pool_p = u_p.reshape(batch, seq, D_POOL)[:, seq - POOL_HIST:][None]
    return (y_p.reshape(batch, seq, D_MODEL), y_s.reshape(nb, 1, D_MODEL), kv0, kv1, kv2, pool_p,
            kv_s[0], kv_s[1], kv_s[2], pool_s[None])
```

```python
import functools

import jax
import jax.numpy as jnp
from jax import lax
from jax.experimental import pallas as pl
from jax.experimental.pallas import tpu as pltpu

F32 = jnp.float32
BF16 = jnp.bfloat16

D_MODEL = 2048
HEAD_DIM = 128
HEADS_PER_GROUP = 4
ATTN_GROUPS = ((128, 1), (512, 4), (2048, 16))
N_GROUPS = len(ATTN_GROUPS)
N_HEADS = HEADS_PER_GROUP * N_GROUPS
D_ATTN = N_HEADS * HEAD_DIM
D_GROUP = HEADS_PER_GROUP * HEAD_DIM
BAND = 128
POOL_WINDOWS = (2, 4, 8, 16)
POOL_GROUP = 128
D_POOL = POOL_GROUP * len(POOL_WINDOWS)
POOL_HIST = max(POOL_WINDOWS) - 1
POOL_HALO = 16
D_FF = 4 * D_MODEL
D_IN = 3 * D_ATTN + D_POOL + 2 * D_MODEL
ALIBI_MAX_BIAS = 8.0
EPS = 1e-6
SCALE = HEAD_DIM ** -0.5
NEG = -1e30

VMEM_LIMIT_BYTES = 56 * 1024 * 1024

IN_TILE = D_GROUP
N_IN_TILES = D_IN // IN_TILE
QKV_TILES = 3 * N_GROUPS
POOL_TILE = QKV_TILES
GATE_TILE0 = POOL_TILE + 1

TM_IN = 1024
TM_MIX = 256
TM_MLP = 512
TF_MLP = 1024
BS_ATTN = 8


def _cparams(*sem):
    return pltpu.CompilerParams(dimension_semantics=sem, vmem_limit_bytes=VMEM_LIMIT_BYTES)


def _rms(x):
    return x * lax.rsqrt(jnp.mean(x * x, axis=-1, keepdims=True) + EPS)


def _ada_kernel(c_ref, w_ref, b_ref, o_ref):
    c = c_ref[...]
    a = (c * jax.nn.sigmoid(c)).astype(BF16)
    o_ref[...] = jnp.dot(a, w_ref[...].astype(BF16), preferred_element_type=F32) + b_ref[...]


def _ada(c_all, w_ada, b_ada):
    m = c_all.shape[0]
    tn = 1024
    return pl.pallas_call(
        _ada_kernel,
        grid=(6 * D_MODEL // tn,),
        in_specs=[pl.BlockSpec((m, D_MODEL), lambda j: (0, 0)),
                  pl.BlockSpec((D_MODEL, tn), lambda j: (0, j)),
                  pl.BlockSpec((1, tn), lambda j: (0, j))],
        out_specs=pl.BlockSpec((m, tn), lambda j: (0, j)),
        out_shape=jax.ShapeDtypeStruct((m, 6 * D_MODEL), F32),
        compiler_params=_cparams("arbitrary"),
        name="ada",
    )(c_all, w_ada, b_ada.reshape(1, 6 * D_MODEL))


def _store_heads(ref, val):
    for h in range(HEADS_PER_GROUP):
        ref[:, h, :] = val[:, h * HEAD_DIM:(h + 1) * HEAD_DIM]


def _inproj_p_kernel(x_ref, sh_ref, sc_ref, g_ref, w_ref,
                     qkv0_ref, qkv1_ref, qkv2_ref, u_ref, gate_ref, kv2_ref, kv1_ref, kv0_ref,
                     h_scr, acc_scr, *, tiles_per_batch):
    j = pl.program_id(1)
    tm = x_ref.shape[0]

    @pl.when(j == 0)
    def _():
        y = _rms(x_ref[...]) * g_ref[...]
        h_scr[...] = (y * (1.0 + sc_ref[...]) + sh_ref[...]).astype(BF16)

    acc = jnp.dot(h_scr[...], w_ref[...], preferred_element_type=F32)

    for g, (qkv_ref, (_, dil)) in enumerate(zip((qkv0_ref, qkv1_ref, qkv2_ref), ATTN_GROUPS)):
        @pl.when((j == g) | (j == N_GROUPS + g) | (j == 2 * N_GROUPS + g))
        def _(qkv_ref=qkv_ref, dil=dil):
            if dil == 1:
                qkv_ref[0] = acc.astype(BF16)
            else:
                for h in range(HEADS_PER_GROUP):
                    sl = slice(h * HEAD_DIM, (h + 1) * HEAD_DIM)
                    acc_scr[h] = acc[:, sl]
                    for r in range(dil):
                        qkv_ref[r, :, sl] = acc_scr[h, pl.ds(r, tm // dil, stride=dil), :].astype(BF16)

    @pl.when(j == POOL_TILE)
    def _():
        u_ref[...] = acc

    @pl.when(j >= GATE_TILE0)
    def _():
        gate_ref[...] = jax.nn.sigmoid(acc).astype(BF16)

    @pl.when((j == N_GROUPS + 2) | (j == 2 * N_GROUPS + 2))
    def _():
        _store_heads(kv2_ref, acc)

    last = pl.program_id(0) % tiles_per_batch == tiles_per_batch - 1

    @pl.when(last & ((j == N_GROUPS + 1) | (j == 2 * N_GROUPS + 1)))
    def _():
        _store_heads(kv1_ref, acc[tm - kv1_ref.shape[0]:, :])

    @pl.when(last & ((j == N_GROUPS) | (j == 2 * N_GROUPS)))
    def _():
        _store_heads(kv0_ref, acc[tm - kv0_ref.shape[0]:, :])


def _inproj_prompt(x2d, mod_p, norm_g, w_in, batch, seq):
    m = x2d.shape[0]
    tm = TM_IN
    tpb = seq // tm
    keep = [min(w, seq) for w, _ in ATTN_GROUPS]
    assert keep[2] == seq and tm >= keep[1] and seq % tm == 0

    def sub_spec(g):
        dil = ATTN_GROUPS[g][1]
        return pl.BlockSpec((None, None, dil, tm // dil, D_GROUP),
                            lambda mi, j: (jnp.clip((j - g) // N_GROUPS, 0, 2), mi // tpb, 0, mi % tpb, 0))

    def sub_shape(g):
        dil = ATTN_GROUPS[g][1]
        return jax.ShapeDtypeStruct((3, batch, dil, seq // dil, D_GROUP), BF16)

    def kv_spec(rows, first_v_tile, tail_only):
        def index(mi, j):
            is_v = j >= first_v_tile
            if tail_only:
                return (0, mi // tpb, 0, jnp.where(is_v & (mi % tpb == tpb - 1), 1, 0), 0, 0)
            return (0, mi // tpb, mi % tpb, jnp.where(is_v, 1, 0), 0, 0)
        return pl.BlockSpec((None, None, rows, None, HEADS_PER_GROUP, HEAD_DIM), index)

    def kv_shape(rows):
        return jax.ShapeDtypeStruct((1, batch, rows, 2, HEADS_PER_GROUP, HEAD_DIM), F32)

    in_specs = [
        pl.BlockSpec((tm, D_MODEL), lambda mi, j: (mi, 0)),
        pl.BlockSpec((None, 1, D_MODEL), lambda mi, j: (mi // tpb, 0, 0)),
        pl.BlockSpec((None, 1, D_MODEL), lambda mi, j: (mi // tpb, 0, 1)),
        pl.BlockSpec((1, D_MODEL), lambda mi, j: (0, 0)),
        pl.BlockSpec((D_MODEL, IN_TILE), lambda mi, j: (0, j)),
    ]
    out_specs = [
        sub_spec(0), sub_spec(1), sub_spec(2),
        pl.BlockSpec((tm, IN_TILE), lambda mi, j: (mi, 0)),
        pl.BlockSpec((tm, IN_TILE), lambda mi, j: (mi, jnp.clip(j - GATE_TILE0, 0, N_IN_TILES - GATE_TILE0 - 1))),
        kv_spec(tm, 2 * N_GROUPS + 2, False),
        kv_spec(keep[1], 2 * N_GROUPS + 1, True),
        kv_spec(keep[0], 2 * N_GROUPS, True),
    ]
    out_shape = [
        sub_shape(0), sub_shape(1), sub_shape(2),
        jax.ShapeDtypeStruct((m, D_POOL), F32),
        jax.ShapeDtypeStruct((m, 2 * D_MODEL), BF16),
        kv_shape(keep[2]), kv_shape(keep[1]), kv_shape(keep[0]),
    ]
    return pl.pallas_call(
        functools.partial(_inproj_p_kernel, tiles_per_batch=tpb),
        grid=(m // tm, N_IN_TILES),
        in_specs=in_specs, out_specs=out_specs, out_shape=out_shape,
        scratch_shapes=[pltpu.VMEM((tm, D_MODEL), BF16), pltpu.VMEM((HEADS_PER_GROUP, tm, HEAD_DIM), F32)],
        compiler_params=_cparams("arbitrary", "arbitrary"),
        name="inproj_prompt",
    )(x2d, mod_p, mod_p, norm_g, w_in)


def _attn_block(q, k, v, bias_fn):
    lane = lax.broadcasted_iota(jnp.int32, (BAND, HEAD_DIM), 1)
    outs = []
    lse = jnp.zeros((BAND, HEAD_DIM), F32)
    for h in range(HEADS_PER_GROUP):
        sl = slice(h * HEAD_DIM, (h + 1) * HEAD_DIM)
        s = lax.dot_general(q[:, sl], k[:, sl], (((1,), (1,)), ((), ())), preferred_element_type=F32)
        s = s * SCALE + bias_fn(h)
        mx = jnp.max(s, axis=-1, keepdims=True)
        p = jnp.exp(s - mx)
        den = jnp.sum(p, axis=-1, keepdims=True)
        outs.append(jnp.dot(p.astype(BF16), v[:, sl], preferred_element_type=F32) / den)
        lse = jnp.where(lane == h, mx + jnp.log(den), lse)
    return outs, lse


def _attn_p_kernel(bias_ref, q_ref, k_ref, v_ref, o_ref, l_ref, o_scr):
    dil, n, _ = q_ref.shape
    nb = n // BAND

    def rows(start):
        return pl.ds(start, BAND) if dil == 1 else pl.ds(start, BAND, stride=dil)

    def put(start, outs, lse):
        for h in range(HEADS_PER_GROUP):
            o_scr[h, rows(start), :] = outs[h]
        l_ref[rows(start), :] = lse

    for r in range(dil):
        put(r, *_attn_block(q_ref[r, 0:BAND, :], k_ref[r, 0:BAND, :], v_ref[r, 0:BAND, :],
                            lambda h: bias_ref[h, :, BAND:2 * BAND]))

        if nb > 1:
            def body(i, carry, r=r):
                r0 = pl.multiple_of(i * BAND, BAND)
                rk = pl.multiple_of((i - 1) * BAND, BAND)
                put(r0 * dil + r, *_attn_block(q_ref[r, pl.ds(r0, BAND), :], k_ref[r, pl.ds(rk, 2 * BAND), :],
                                               v_ref[r, pl.ds(rk, 2 * BAND), :], lambda h: bias_ref[h]))
                return carry

            lax.fori_loop(1, nb, body, 0)

    for h in range(HEADS_PER_GROUP):
        o_ref[:, h * HEAD_DIM:(h + 1) * HEAD_DIM] = o_scr[h].astype(BF16)


def _attn_prompt_group(qkv_sub, bias, g, batch, seq):
    dil = ATTN_GROUPS[g][1]
    n = seq // dil

    def sub(which):
        return pl.BlockSpec((None, None, dil, n, D_GROUP), lambda b: (which, b, 0, 0, 0))

    return pl.pallas_call(
        _attn_p_kernel,
        grid=(batch,),
        in_specs=[pl.BlockSpec((HEADS_PER_GROUP, BAND, 2 * BAND), lambda b: (0, 0, 0)),
                  sub(0), sub(1), sub(2)],
        out_specs=[pl.BlockSpec((seq, D_GROUP), lambda b: (b, 0)),
                   pl.BlockSpec((seq, HEAD_DIM), lambda b: (b, 0))],
        out_shape=[jax.ShapeDtypeStruct((batch * seq, D_GROUP), BF16),
                   jax.ShapeDtypeStruct((batch * seq, HEAD_DIM), F32)],
        scratch_shapes=[pltpu.VMEM((HEADS_PER_GROUP, seq, HEAD_DIM), F32)],
        compiler_params=_cparams("arbitrary"),
        name=f"attn_prompt_g{g}",
    )(bias, qkv_sub, qkv_sub, qkv_sub)


def _merge_heads(outs, lses):
    cols = []
    for h in range(HEADS_PER_GROUP):
        sl = slice(h * HEAD_DIM, (h + 1) * HEAD_DIM)
        l = [lg[:, h:h + 1] for lg in lses]
        mx = jnp.maximum(jnp.maximum(l[0], l[1]), l[2])
        e = [jnp.exp(x - mx) for x in l]
        tot = e[0] + e[1] + e[2]
        cols.append(sum((e[g] / tot) * outs[g][:, sl].astype(F32) for g in range(N_GROUPS)))
    return jnp.concatenate(cols, axis=-1)


def _mix_tail(o_attn, p, gates_a, gates_b, x, gt1, sh2, sc2, wua_ref, wup_ref, wo_ref, g2_ref, x1_ref, h2_ref):
    a = jnp.dot(o_attn.astype(BF16), wua_ref[...], preferred_element_type=F32)
    b = jnp.dot(p.astype(BF16), wup_ref[...], preferred_element_type=F32)
    mix = (gates_a * a + gates_b * b).astype(BF16)
    x1 = x + gt1 * jnp.dot(mix, wo_ref[...], preferred_element_type=F32)
    x1_ref[...] = x1
    h2_ref[...] = (_rms(x1) * g2_ref[...] * (1.0 + sc2) + sh2).astype(BF16)


def _pool_project(z_groups, wp_ref, ps_ref):
    cols = [jnp.dot(z.astype(BF16), wp_ref[g], preferred_element_type=F32) for g, z in enumerate(z_groups)]
    return jnp.concatenate(cols, axis=-1) * ps_ref[...]


def _mix_p_kernel(o0_ref, o1_ref, o2_ref, l0_ref, l1_ref, l2_ref, u_ref, uh_ref, gate_ref, x_ref,
                  gt1_ref, sh2_ref, sc2_ref, wua_ref, wp_ref, ps_ref, wup_ref, wo_ref, g2_ref,
                  x1_ref, h2_ref, ext_scr, *, tiles_per_batch):
    tm = u_ref.shape[0]
    t = pl.program_id(0) % tiles_per_batch
    o_attn = _merge_heads([o0_ref[...], o1_ref[...], o2_ref[...]], [l0_ref[...], l1_ref[...], l2_ref[...]])

    u = u_ref[...]
    ext_scr[0:POOL_HALO, :] = jnp.where(t == 0, 0.0, uh_ref[...])
    ext_scr[POOL_HALO:, :] = u
    pos = t * tm + lax.broadcasted_iota(jnp.int32, (tm, 1), 0)
    zs = []
    for g, w in enumerate(POOL_WINDOWS):
        sl = slice(g * POOL_GROUP, (g + 1) * POOL_GROUP)
        win = u[:, sl]
        for jj in range(1, w):
            win = win + ext_scr[POOL_HALO - jj:POOL_HALO - jj + tm, sl]
        cnt = jnp.minimum(pos + 1, w).astype(F32)
        zs.append(win / cnt - u[:, sl])
    p = _pool_project(zs, wp_ref, ps_ref)

    gates = gate_ref[...]
    _mix_tail(o_attn, p, gates[:, :D_MODEL].astype(F32), gates[:, D_MODEL:].astype(F32), x_ref[...],
              gt1_ref[...], sh2_ref[...], sc2_ref[...], wua_ref, wup_ref, wo_ref, g2_ref, x1_ref, h2_ref)


def _const_spec(shape):
    nd = len(shape)
    return pl.BlockSpec(shape, lambda *_: (0,) * nd)


def _mix_prompt(outs, lses, u, gates, x2d, mod_p, w_up_attn, w_pool, pool_scale, w_up_pool, w_out, norm_g2, seq):
    m = x2d.shape[0]
    tm = TM_MIX
    tpb = seq // tm
    hb = tm // POOL_HALO

    def row(width):
        return pl.BlockSpec((tm, width), lambda mi: (mi, 0))

    def mod(c):
        return pl.BlockSpec((None, 1, D_MODEL), lambda mi: (mi // tpb, 0, c))

    in_specs = ([row(D_GROUP)] * 3 + [row(HEAD_DIM)] * 3 + [
        row(D_POOL),
        pl.BlockSpec((POOL_HALO, D_POOL), lambda mi: (jnp.maximum(mi * hb - 1, 0), 0)),
        row(2 * D_MODEL), row(D_MODEL), mod(2), mod(3), mod(4),
        _const_spec((D_GROUP, D_MODEL)), _const_spec((len(POOL_WINDOWS), POOL_GROUP, POOL_GROUP)),
        _const_spec((1, D_POOL)), _const_spec((D_POOL, D_MODEL)), _const_spec((D_MODEL, D_MODEL)),
        _const_spec((1, D_MODEL))])
    return pl.pallas_call(
        functools.partial(_mix_p_kernel, tiles_per_batch=tpb),
        grid=(m // tm,),
        in_specs=in_specs,
        out_specs=[row(D_MODEL), row(D_MODEL)],
        out_shape=[jax.ShapeDtypeStruct((m, D_MODEL), F32), jax.ShapeDtypeStruct((m, D_MODEL), BF16)],
        scratch_shapes=[pltpu.VMEM((POOL_HALO + tm, D_POOL), F32)],
        compiler_params=_cparams("arbitrary"),
        name="mix_prompt",
    )(*outs, *lses, u, u, gates, x2d, mod_p, mod_p, mod_p, w_up_attn, w_pool, pool_scale, w_up_pool, w_out, norm_g2)


def _mlp_kernel(h_ref, wu_ref, wd_ref, x1_ref, gt_ref, gf_ref, y_ref, acc_ref):
    f = pl.program_id(1)
    a = jnp.dot(h_ref[...], wu_ref[...], preferred_element_type=F32)
    a = jnp.square(jnp.maximum(a, 0.0)).astype(BF16)
    part = jnp.dot(a, wd_ref[...], preferred_element_type=F32)

    @pl.when(f == 0)
    def _():
        acc_ref[...] = part

    @pl.when(f > 0)
    def _():
        acc_ref[...] += part

    @pl.when(f == pl.num_programs(1) - 1)
    def _():
        x2 = x1_ref[...] + gt_ref[...] * acc_ref[...]
        y_ref[...] = _rms(x2) * gf_ref[...]


def _mlp(h2, x1, gt_spec, gt_arr, w_up, w_down, norm_gf, tm, name):
    m = h2.shape[0]
    return pl.pallas_call(
        _mlp_kernel,
        grid=(m // tm, D_FF // TF_MLP),
        in_specs=[pl.BlockSpec((tm, D_MODEL), lambda mi, f: (mi, 0)),
                  pl.BlockSpec((D_MODEL, TF_MLP), lambda mi, f: (0, f)),
                  pl.BlockSpec((TF_MLP, D_MODEL), lambda mi, f: (f, 0)),
                  pl.BlockSpec((tm, D_MODEL), lambda mi, f: (mi, 0)),
                  gt_spec,
                  pl.BlockSpec((1, D_MODEL), lambda mi, f: (0, 0))],
        out_specs=pl.BlockSpec((tm, D_MODEL), lambda mi, f: (mi, 0)),
        out_shape=jax.ShapeDtypeStruct((m, D_MODEL), F32),
        scratch_shapes=[pltpu.VMEM((tm, D_MODEL), F32)],
        compiler_params=_cparams("arbitrary", "arbitrary"),
        name=name,
    )(h2, w_up, w_down, x1, gt_arr, norm_gf)


def _inproj_s_kernel(x_ref, sh_ref, sc_ref, g_ref, w_ref, o_ref, h_scr):
    @pl.when(pl.program_id(0) == 0)
    def _():
        y = _rms(x_ref[...]) * g_ref[...]
        h_scr[...] = (y * (1.0 + sc_ref[...]) + sh_ref[...]).astype(BF16)

    o_ref[...] = jnp.dot(h_scr[...], w_ref[...], preferred_element_type=F32)


def _inproj_sample(x_s, mod_s, norm_g, w_in):
    m = x_s.shape[0]
    tn = D_ATTN
    return pl.pallas_call(
        _inproj_s_kernel,
        grid=(D_IN // tn,),
        in_specs=[pl.BlockSpec((m, D_MODEL), lambda j: (0, 0)),
                  pl.BlockSpec((m, D_MODEL), lambda j: (0, 0)),
                  pl.BlockSpec((m, D_MODEL), lambda j: (0, 1)),
                  pl.BlockSpec((1, D_MODEL), lambda j: (0, 0)),
                  pl.BlockSpec((D_MODEL, tn), lambda j: (0, j))],
        out_specs=pl.BlockSpec((m, tn), lambda j: (0, j)),
        out_shape=jax.ShapeDtypeStruct((m, D_IN), F32),
        scratch_shapes=[pltpu.VMEM((m, D_MODEL), BF16)],
        compiler_params=_cparams("arbitrary"),
        name="inproj_sample",
    )(x_s, mod_s, mod_s, norm_g, w_in)


def _attn_s_kernel(bias_ref, q_ref, kn_ref, vn_ref, k0_ref, v0_ref, k1_ref, v1_ref, k2_ref, v2_ref, o_ref):
    k_refs = (k0_ref, k1_ref, k2_ref)
    v_refs = (v0_ref, v1_ref, v2_ref)

    def body(s, carry):
        outs, lses = [], []
        for g in range(N_GROUPS):
            hs = slice(g * HEADS_PER_GROUP, (g + 1) * HEADS_PER_GROUP)
            q = q_ref[s, hs, :]
            kn = kn_ref[s, hs, :]
            vn = vn_ref[s, hs, :]
            kt = k_refs[g][s]
            vt = v_refs[g][s]
            sh = jnp.sum(kt * q[None], axis=-1, keepdims=True) * SCALE + bias_ref[g]
            sn = jnp.sum(kn * q, axis=-1, keepdims=True) * SCALE
            mx = jnp.maximum(jnp.max(sh, axis=0), sn)
            p = jnp.exp(sh - mx[None])
            pn = jnp.exp(sn - mx)
            den = jnp.sum(p, axis=0) + pn
            num = jnp.sum(p * vt, axis=0) + pn * vn
            outs.append(num / den)
            lses.append(mx + jnp.log(den))
        mx = jnp.maximum(jnp.maximum(lses[0], lses[1]), lses[2])
        e = [jnp.exp(l - mx) for l in lses]
        tot = e[0] + e[1] + e[2]
        o_ref[s] = (e[0] * outs[0] + e[1] * outs[1] + e[2] * outs[2]) / tot
        return carry

    lax.fori_loop(0, q_ref.shape[0], body, 0)


def _attn_sample(bias_s, q3, kn3, vn3, caches):
    nb = q3.shape[0]
    bs = BS_ATTN
    tok = pl.BlockSpec((bs, N_HEADS, HEAD_DIM), lambda i: (i, 0, 0))
    in_specs = [_const_spec((N_GROUPS, BAND, HEADS_PER_GROUP, HEAD_DIM)), tok, tok, tok]
    args = [bias_s, q3, kn3, vn3]
    for c, (_, dil) in zip(caches, ATTN_GROUPS):
        cv = c.reshape(nb, BAND, dil * 2, HEADS_PER_GROUP, HEAD_DIM)
        for kv in range(2):
            in_specs.append(pl.BlockSpec((bs, BAND, None, HEADS_PER_GROUP, HEAD_DIM),
                                         lambda i, kv=kv: (i, 0, kv, 0, 0)))
            args.append(cv)
    return pl.pallas_call(
        _attn_s_kernel,
        grid=(nb // bs,),
        in_specs=in_specs,
        out_specs=pl.BlockSpec((bs, HEADS_PER_GROUP, HEAD_DIM), lambda i: (i, 0, 0)),
        out_shape=jax.ShapeDtypeStruct((nb, HEADS_PER_GROUP, HEAD_DIM), F32),
        compiler_params=_cparams("arbitrary"),
        name="attn_sample",
    )(*args)


def _mix_s_kernel(o_ref, u_ref, hist_ref, ga_ref, gb_ref, x_ref, gt1_ref, sh2_ref, sc2_ref,
                  wua_ref, wp_ref, ps_ref, wup_ref, wo_ref, g2_ref, x1_ref, h2_ref, np_ref):
    u = u_ref[...]
    zs = []
    for g, w in enumerate(POOL_WINDOWS):
        sl = slice(g * POOL_GROUP, (g + 1) * POOL_GROUP)
        win = u[:, sl]
        for jj in range(1, w):
            win = win + hist_ref[:, POOL_HIST - jj, sl]
        zs.append(win / float(w) - u[:, sl])
    p = _pool_project(zs, wp_ref, ps_ref)
    np_ref[:, 0:POOL_HIST - 1, :] = hist_ref[:, 1:POOL_HIST, :]
    np_ref[:, POOL_HIST - 1, :] = u
    _mix_tail(o_ref[...], p, jax.nn.sigmoid(ga_ref[...]), jax.nn.sigmoid(gb_ref[...]), x_ref[...],
              gt1_ref[...], sh2_ref[...], sc2_ref[...], wua_ref, wup_ref, wo_ref, g2_ref, x1_ref, h2_ref)


def _mix_sample(o_attn, proj_s, hist, x_s, mod_s, w_up_attn, w_pool, pool_scale, w_up_pool, w_out, norm_g2):
    m = x_s.shape[0]

    def cols(width, c):
        return pl.BlockSpec((m, width), lambda i: (0, c))

    u0 = 3 * D_ATTN
    u = lax.slice_in_dim(proj_s, u0, u0 + D_POOL, axis=1)
    ga = lax.slice_in_dim(proj_s, u0 + D_POOL, u0 + D_POOL + D_MODEL, axis=1)
    gb = lax.slice_in_dim(proj_s, u0 + D_POOL + D_MODEL, D_IN, axis=1)
    in_specs = [cols(D_GROUP, 0), cols(D_POOL, 0), _const_spec((m, POOL_HIST, D_POOL)),
                cols(D_MODEL, 0), cols(D_MODEL, 0),
                cols(D_MODEL, 0), cols(D_MODEL, 2), cols(D_MODEL, 3), cols(D_MODEL, 4),
                _const_spec((D_GROUP, D_MODEL)), _const_spec((len(POOL_WINDOWS), POOL_GROUP, POOL_GROUP)),
                _const_spec((1, D_POOL)), _const_spec((D_POOL, D_MODEL)), _const_spec((D_MODEL, D_MODEL)),
                _const_spec((1, D_MODEL))]
    return pl.pallas_call(
        _mix_s_kernel,
        grid=(1,),
        in_specs=in_specs,
        out_specs=[_const_spec((m, D_MODEL)), _const_spec((m, D_MODEL)), _const_spec((m, POOL_HIST, D_POOL))],
        out_shape=[jax.ShapeDtypeStruct((m, D_MODEL), F32), jax.ShapeDtypeStruct((m, D_MODEL), BF16),
                   jax.ShapeDtypeStruct((m, POOL_HIST, D_POOL), F32)],
        compiler_params=_cparams("arbitrary"),
        name="mix_sample",
    )(o_attn, u, hist, ga, gb, x_s, mod_s, mod_s, mod_s, w_up_attn, w_pool, pool_scale, w_up_pool, w_out, norm_g2)


class _ShiftStream:
    def __init__(self, cache, new, out, buf, sems, rows, ns):
        self.cache, self.new, self.out, self.buf, self.sems = cache, new, out, buf, sems
        self.rows, self.ns = rows, ns
        self.slots = buf.shape[0]
        nb, w = cache.shape[1], cache.shape[2]
        assert w % rows == 0 and nb % ns == 0 and (ns == 1 or rows == w)
        self.cps = w // rows
        self.n_chunks = (nb // ns) * self.cps

    def _where(self, k):
        return (k // self.cps) * self.ns, (k % self.cps) * self.rows, k % self.slots

    def _body_in(self, k):
        b0, w0, slot = self._where(k)
        return pltpu.make_async_copy(self.cache.at[0, pl.ds(b0, self.ns), pl.ds(w0 + 1, self.rows - 1)],
                                     self.buf.at[slot, :, pl.ds(0, self.rows - 1)], self.sems.at[0, slot])

    def _next_in(self, k):
        b0, w0, slot = self._where(k)
        return pltpu.make_async_copy(self.cache.at[0, pl.ds(b0, self.ns), pl.ds(w0 + self.rows, 1)],
                                     self.buf.at[slot, :, pl.ds(self.rows - 1, 1)], self.sems.at[1, slot])

    def _new_in(self, k):
        b0, _, slot = self._where(k)
        return pltpu.make_async_copy(self.new.at[pl.ds(b0, self.ns)],
                                     self.buf.at[slot, :, self.rows - 1], self.sems.at[1, slot])

    def _out(self, k):
        b0, w0, slot = self._where(k)
        return pltpu.make_async_copy(self.buf.at[slot],
                                     self.out.at[0, pl.ds(b0, self.ns), pl.ds(w0, self.rows)], self.sems.at[2, slot])

    def _last_row(self, k, act):
        if isinstance(k, int) or self.cps == 1:
            at_end = self.cps == 1 or k % self.cps == self.cps - 1
            getattr(self._new_in(k) if at_end else self._next_in(k), act)()
            return
        at_end = k % self.cps == self.cps - 1

        @pl.when(at_end)
        def _():
            getattr(self._new_in(k), act)()

        @pl.when(jnp.logical_not(at_end))
        def _():
            getattr(self._next_in(k), act)()

    def start_in(self, k):
        self._body_in(k).start()
        self._last_row(k, "start")

    def wait_in(self, k):
        self._body_in(k).wait()
        self._last_row(k, "wait")

    def start_out(self, k):
        self._out(k).start()

    def wait_out(self, k):
        self._out(k).wait()

    def step(self, k):
        self.wait_in(k)
        self.start_out(k)

        @pl.when(k >= 1)
        def _():
            self.wait_out(k - 1)

        @pl.when(k + self.slots - 1 < self.n_chunks)
        def _():
            self.start_in(k + self.slots - 1)

    def prime(self):
        for k in range(min(self.slots - 1, self.n_chunks)):
            self.start_in(k)

    def drain(self):
        self.wait_out(self.n_chunks - 1)


SHIFT_CHUNKS = ((128, 8), (512, 2), (1024, 1))
SHIFT_SLOTS = 3


def _shift_kernel(c0, c1, c2, n0, n1, n2, o0, o1, o2):
    for cache, new, out, (rows, ns) in zip((c0, c1, c2), (n0, n1, n2), (o0, o1, o2), SHIFT_CHUNKS):
        def run(buf, sems, cache=cache, new=new, out=out, rows=rows, ns=ns):
            stream = _ShiftStream(cache, new, out, buf, sems, rows, ns)
            stream.prime()

            def body(k, carry):
                stream.step(k)
                return carry

            lax.fori_loop(0, stream.n_chunks, body, 0)
            stream.drain()

        pl.run_scoped(run, pltpu.VMEM((SHIFT_SLOTS, ns, rows, 2, HEADS_PER_GROUP, HEAD_DIM), F32),
                      pltpu.SemaphoreType.DMA((3, SHIFT_SLOTS)))


def _shift_caches(caches, new_rows):
    any_spec = pl.BlockSpec(memory_space=pl.ANY)
    return pl.pallas_call(
        _shift_kernel,
        in_specs=[any_spec] * 6,
        out_specs=[any_spec] * 3,
        out_shape=[jax.ShapeDtypeStruct(c.shape, c.dtype) for c in caches],
        compiler_params=pltpu.CompilerParams(vmem_limit_bytes=VMEM_LIMIT_BYTES),
        name="shift_caches",
    )(*caches, *new_rows)


def _alibi_slopes():
    h = jnp.arange(1, N_HEADS + 1, dtype=F32)
    return jnp.exp2(-ALIBI_MAX_BIAS * h / N_HEADS)


def _prompt_bias(g):
    dil = ATTN_GROUPS[g][1]
    slopes = _alibi_slopes()[g * HEADS_PER_GROUP:(g + 1) * HEADS_PER_GROUP]
    a = jnp.arange(BAND)[:, None]
    b = jnp.arange(2 * BAND)[None, :]
    dist = a - b + BAND
    valid = (dist >= 0) & (dist <= BAND)
    bias = -slopes[:, None, None] * (dist * dil).astype(F32)
    return jnp.where(valid[None], bias, NEG)


def _sample_bias():
    slopes = _alibi_slopes().reshape(N_GROUPS, 1, HEADS_PER_GROUP, 1)
    dil = jnp.array([d for _, d in ATTN_GROUPS], F32).reshape(N_GROUPS, 1, 1, 1)
    back = (BAND - jnp.arange(BAND, dtype=F32)).reshape(1, BAND, 1, 1)
    return jnp.broadcast_to(-slopes * (back * dil), (N_GROUPS, BAND, HEADS_PER_GROUP, HEAD_DIM))


def kernel(x_prompt, x_sample, c_prompt, c_sample, cache_kv_w128, cache_kv_w512, cache_kv_w2048, state_pool,
           norm_mix_g, w_ada, b_ada, w_in, w_up_attn, w_pool, pool_scale, w_up_pool, w_out, norm_mlp_g,
           w_mlp_up, w_mlp_down, norm_final_g):
    batch, seq, _ = x_prompt.shape
    nb = x_sample.shape[0]
    depth = w_in.shape[0]
    assert depth == 1 and x_sample.shape[1] == 1
    caches = (cache_kv_w128, cache_kv_w512, cache_kv_w2048)
    for c, (w, dil) in zip(caches, ATTN_GROUPS):
        assert c.shape[2] == w == BAND * dil

    w_in_b = w_in[0].astype(BF16)
    w_ua_b = w_up_attn[0].astype(BF16)
    w_pool_b = w_pool[0].astype(BF16)
    w_up_b = w_up_pool[0].astype(BF16)
    w_out_b = w_out[0].astype(BF16)
    w_mu_b = w_mlp_up[0].astype(BF16)
    w_md_b = w_mlp_down[0].astype(BF16)
    g1 = norm_mix_g[0].reshape(1, D_MODEL)
    g2 = norm_mlp_g[0].reshape(1, D_MODEL)
    gf = norm_final_g.reshape(1, D_MODEL)
    ps = pool_scale[0].reshape(1, D_POOL)

    mod = _ada(jnp.concatenate([c_sample, c_prompt], axis=0), w_ada[0], b_ada[0])
    mod_s = mod[:nb]
    mod_p = mod[nb:].reshape(batch, 1, 6 * D_MODEL)

    xs = x_sample.reshape(nb, D_MODEL)
    proj_s = _inproj_sample(xs, mod_s, g1, w_in_b)
    q3 = proj_s[:, 0:D_ATTN].reshape(nb, N_HEADS, HEAD_DIM)
    kn3 = proj_s[:, D_ATTN:2 * D_ATTN].reshape(nb, N_HEADS, HEAD_DIM)
    vn3 = proj_s[:, 2 * D_ATTN:3 * D_ATTN].reshape(nb, N_HEADS, HEAD_DIM)
    o_s = _attn_sample(_sample_bias(), q3, kn3, vn3, [c[0] for c in caches])
    new_rows = [jnp.stack([kn3[:, g * HEADS_PER_GROUP:(g + 1) * HEADS_PER_GROUP],
                           vn3[:, g * HEADS_PER_GROUP:(g + 1) * HEADS_PER_GROUP]], axis=1)
                for g in range(N_GROUPS)]
    kv_s = _shift_caches(caches, new_rows)
    x1_s, h2_s, pool_s = _mix_sample(o_s.reshape(nb, D_GROUP), proj_s, state_pool[0], xs, mod_s,
                                     w_ua_b, w_pool_b, ps, w_up_b, w_out_b, g2)
    y_s = _mlp(h2_s, x1_s, pl.BlockSpec((nb, D_MODEL), lambda mi, f: (0, 5)), mod_s, w_mu_b, w_md_b, gf,
               nb, "mlp_sample")

    x2d = x_prompt.reshape(batch * seq, D_MODEL)
    *qkv_sub, u_p, gates, kv2, kv1, kv0 = _inproj_prompt(x2d, mod_p, g1, w_in_b, batch, seq)
    outs, lses = [], []
    for g in range(N_GROUPS):
        o, l = _attn_prompt_group(qkv_sub[g], _prompt_bias(g), g, batch, seq)
        outs.append(o)
        lses.append(l)
    x1_p, h2_p = _mix_prompt(outs, lses, u_p, gates, x2d, mod_p, w_ua_b, w_pool_b, ps, w_up_b, w_out_b, g2, seq)
    tpb = seq // TM_MLP
    y_p = _mlp(h2_p, x1_p, pl.BlockSpec((None, 1, D_MODEL), lambda mi, f: (mi // tpb, 0, 5)), mod_p,
               w_mu_b, w_md_b, gf, TM_MLP, "mlp_prompt")

    pool_p = u_p.reshape(batch, seq, D_POOL)[:, seq - POOL_HIST:][None]
    return (y_p.reshape(batch, seq, D_MODEL), y_s.reshape(nb, 1, D_MODEL), kv0, kv1, kv2, pool_p,
            kv_s[0], kv_s[1], kv_s[2], pool_s[None])
```

```python
import functools
from typing import NamedTuple

import jax
import jax.numpy as jnp
from jax import lax
from jax.experimental import pallas as pl
from jax.experimental.pallas import tpu as pltpu

F32 = jnp.float32
BF16 = jnp.bfloat16

D_MODEL = 2048
HEAD_DIM = 128
HEADS_PER_GROUP = 4
ATTN_GROUPS = ((128, 1), (512, 4), (2048, 16))
N_GROUPS = len(ATTN_GROUPS)
N_HEADS = HEADS_PER_GROUP * N_GROUPS
D_ATTN = N_HEADS * HEAD_DIM
D_GROUP = HEADS_PER_GROUP * HEAD_DIM
BAND = 128
POOL_WINDOWS = (2, 4, 8, 16)
POOL_GROUP = 128
D_POOL = POOL_GROUP * len(POOL_WINDOWS)
POOL_HIST = max(POOL_WINDOWS) - 1
POOL_HALO = 16
D_FF = 4 * D_MODEL
D_IN = 3 * D_ATTN + D_POOL + 2 * D_MODEL
ALIBI_MAX_BIAS = 8.0
EPS = 1e-6
SCALE = HEAD_DIM ** -0.5
NEG = -1e30

VMEM_LIMIT_BYTES = 60 * 1024 * 1024

IN_TILE = D_GROUP
N_IN_TILES = D_IN // IN_TILE
QKV_TILES = 3 * N_GROUPS
POOL_TILE = QKV_TILES
GATE_TILE0 = POOL_TILE + 1

TM_IN = 1024
IN_SPLIT = 2
TM_MIX = 256
TM_MLP = 1024
TF_MLP = 512
TF_MLP_SAMPLE = 1024
BS_ATTN = 8


def _cparams(*sem):
    return pltpu.CompilerParams(dimension_semantics=sem, vmem_limit_bytes=VMEM_LIMIT_BYTES)


def _rms(x):
    return x * lax.rsqrt(jnp.mean(x * x, axis=-1, keepdims=True) + EPS)


def _ada_kernel(c_ref, w_ref, b_ref, o_ref):
    c = c_ref[...]
    a = (c * jax.nn.sigmoid(c)).astype(BF16)
    o_ref[...] = jnp.dot(a, w_ref[...].astype(BF16), preferred_element_type=F32) + b_ref[...]


def _ada(c_all, w_ada, b_ada):
    m = c_all.shape[0]
    tn = 1024
    return pl.pallas_call(
        _ada_kernel,
        grid=(6 * D_MODEL // tn,),
        in_specs=[pl.BlockSpec((m, D_MODEL), lambda j: (0, 0)),
                  pl.BlockSpec((D_MODEL, tn), lambda j: (0, j)),
                  pl.BlockSpec((1, tn), lambda j: (0, j))],
        out_specs=pl.BlockSpec((m, tn), lambda j: (0, j)),
        out_shape=jax.ShapeDtypeStruct((m, 6 * D_MODEL), F32),
        compiler_params=_cparams("arbitrary"),
        name="ada",
    )(c_all, w_ada, b_ada.reshape(1, 6 * D_MODEL))


def _store_heads(ref, row0, val):
    for h in range(HEADS_PER_GROUP):
        ref[row0:row0 + val.shape[0], h, :] = val[:, h * HEAD_DIM:(h + 1) * HEAD_DIM]


def _inproj_p_kernel(x_ref, sh_ref, sc_ref, g_ref, w_ref,
                     qkv0_ref, qkv1_ref, qkv2_ref, u_ref, gate_ref, kv2_ref, kv1_ref, kv0_ref,
                     h_scr, acc_scr, *, tiles_per_batch):
    j = pl.program_id(1)
    tm = x_ref.shape[0]

    @pl.when(j == 0)
    def _():
        y = _rms(x_ref[...]) * g_ref[...]
        h_scr[...] = (y * (1.0 + sc_ref[...]) + sh_ref[...]).astype(BF16)

    th = tm // IN_SPLIT

    def halves():
        for s in range(IN_SPLIT):
            yield s, jnp.dot(h_scr[s * th:(s + 1) * th, :], w_ref[...], preferred_element_type=F32)

    def store_sub(qkv_ref, dil, s, acc):
        if dil == 1:
            qkv_ref[0, s * th:(s + 1) * th, :] = acc.astype(BF16)
            return
        n = th // dil
        for h in range(HEADS_PER_GROUP):
            sl = slice(h * HEAD_DIM, (h + 1) * HEAD_DIM)
            acc_scr[h, s * th:(s + 1) * th, :] = acc[:, sl]
            for r in range(dil):
                qkv_ref[r, s * n:(s + 1) * n, sl] = acc_scr[h, pl.ds(s * th + r, n, stride=dil), :].astype(BF16)

    last = pl.program_id(0) % tiles_per_batch == tiles_per_batch - 1
    kv_refs = (kv0_ref, kv1_ref, kv2_ref)
    for g, (qkv_ref, (_, dil)) in enumerate(zip((qkv0_ref, qkv1_ref, qkv2_ref), ATTN_GROUPS)):
        @pl.when(j == g)
        def _(qkv_ref=qkv_ref, dil=dil):
            for s, acc in halves():
                store_sub(qkv_ref, dil, s, acc)

        @pl.when((j == N_GROUPS + g) | (j == 2 * N_GROUPS + g))
        def _(g=g, qkv_ref=qkv_ref, dil=dil):
            kv_ref = kv_refs[g]
            keep = kv_ref.shape[0]
            for s, acc in halves():
                store_sub(qkv_ref, dil, s, acc)
                if keep == tm:
                    _store_heads(kv_ref, s * th, acc)
                elif s == IN_SPLIT - 1:
                    assert keep <= th

                    @pl.when(last)
                    def _(acc=acc):
                        _store_heads(kv_ref, 0, acc[th - keep:, :])

    @pl.when(j == POOL_TILE)
    def _():
        for s, acc in halves():
            u_ref[s * th:(s + 1) * th, :] = acc

    @pl.when(j >= GATE_TILE0)
    def _():
        for s, acc in halves():
            gate_ref[s * th:(s + 1) * th, :] = (0.5 * jnp.tanh(0.5 * acc) + 0.5).astype(BF16)


def _inproj_prompt(x2d, mod_p, norm_g, w_in, batch, seq, hosted):
    m = x2d.shape[0]
    tm = TM_IN
    tpb = seq // tm
    keep = [min(w, seq) for w, _ in ATTN_GROUPS]
    assert keep[2] == seq and tm >= keep[1] and seq % tm == 0

    def sub_spec(g):
        dil = ATTN_GROUPS[g][1]
        return pl.BlockSpec((None, None, dil, tm // dil, D_GROUP),
                            lambda mi, j: (jnp.clip((j - g) // N_GROUPS, 0, 2), mi // tpb, 0, mi % tpb, 0))

    def sub_shape(g):
        dil = ATTN_GROUPS[g][1]
        return jax.ShapeDtypeStruct((3, batch, dil, seq // dil, D_GROUP), BF16)

    def kv_spec(rows, first_v_tile, tail_only):
        def index(mi, j):
            is_v = j >= first_v_tile
            if tail_only:
                return (0, mi // tpb, 0, jnp.where(is_v & (mi % tpb == tpb - 1), 1, 0), 0, 0)
            return (0, mi // tpb, mi % tpb, jnp.where(is_v, 1, 0), 0, 0)
        return pl.BlockSpec((None, None, rows, None, HEADS_PER_GROUP, HEAD_DIM), index)

    def kv_shape(rows):
        return jax.ShapeDtypeStruct((1, batch, rows, 2, HEADS_PER_GROUP, HEAD_DIM), F32)

    in_specs = [
        pl.BlockSpec((tm, D_MODEL), lambda mi, j: (mi, 0)),
        pl.BlockSpec((None, 1, D_MODEL), lambda mi, j: (mi // tpb, 0, 0)),
        pl.BlockSpec((None, 1, D_MODEL), lambda mi, j: (mi // tpb, 0, 1)),
        pl.BlockSpec((1, D_MODEL), lambda mi, j: (0, 0)),
        pl.BlockSpec((D_MODEL, IN_TILE), lambda mi, j: (0, j)),
    ]
    out_specs = [
        sub_spec(0), sub_spec(1), sub_spec(2),
        pl.BlockSpec((tm, IN_TILE), lambda mi, j: (mi, 0)),
        pl.BlockSpec((tm, IN_TILE), lambda mi, j: (mi, jnp.clip(j - GATE_TILE0, 0, N_IN_TILES - GATE_TILE0 - 1))),
        kv_spec(tm, 2 * N_GROUPS + 2, False),
        kv_spec(keep[1], 2 * N_GROUPS + 1, True),
        kv_spec(keep[0], 2 * N_GROUPS, True),
    ]
    out_shape = [
        sub_shape(0), sub_shape(1), sub_shape(2),
        jax.ShapeDtypeStruct((m, D_POOL), F32),
        jax.ShapeDtypeStruct((m, 2 * D_MODEL), BF16),
        kv_shape(keep[2]), kv_shape(keep[1]), kv_shape(keep[0]),
    ]
    s_args, s_in_specs, s_out_shape, s_out_specs, s_scratch, aliases = _stream_operands(
        hosted, len(in_specs), len(out_specs))
    kernel = _host_streams(functools.partial(_inproj_p_kernel, tiles_per_batch=tpb), len(in_specs), len(out_specs),
                           2, hosted, (m // tm) * N_IN_TILES,
                           lambda: pl.program_id(0) * N_IN_TILES + pl.program_id(1))
    return pl.pallas_call(
        kernel,
        grid=(m // tm, N_IN_TILES),
        in_specs=in_specs + s_in_specs, out_specs=out_specs + s_out_specs, out_shape=out_shape + s_out_shape,
        scratch_shapes=[pltpu.VMEM((tm, D_MODEL), BF16),
                        pltpu.VMEM((HEADS_PER_GROUP, tm, HEAD_DIM), F32)] + s_scratch,
        input_output_aliases=aliases,
        compiler_params=_cparams("arbitrary", "arbitrary"),
        name="inproj_prompt",
    )(x2d, mod_p, mod_p, norm_g, w_in, *s_args)


def _attn_block(q, k, v, bias_fn):
    lane = lax.broadcasted_iota(jnp.int32, (BAND, HEAD_DIM), 1)
    outs = []
    lse = jnp.zeros((BAND, HEAD_DIM), F32)
    for h in range(HEADS_PER_GROUP):
        sl = slice(h * HEAD_DIM, (h + 1) * HEAD_DIM)
        s = lax.dot_general(q[:, sl], k[:, sl], (((1,), (1,)), ((), ())), preferred_element_type=F32)
        s = s * SCALE + bias_fn(h)
        mx = jnp.max(s, axis=-1, keepdims=True)
        p = jnp.exp(s - mx)
        den = jnp.sum(p, axis=-1, keepdims=True)
        outs.append(jnp.dot(p.astype(BF16), v[:, sl], preferred_element_type=F32) / den)
        lse = jnp.where(lane == h, mx + jnp.log(den), lse)
    return outs, lse


def _attn_p_kernel(bias_ref, q_ref, k_ref, v_ref, o_ref, l_ref, o_scr):
    dil, n, _ = q_ref.shape
    nb = n // BAND

    def rows(start):
        return pl.ds(start, BAND) if dil == 1 else pl.ds(start, BAND, stride=dil)

    def put(start, outs, lse):
        for h in range(HEADS_PER_GROUP):
            o_scr[h, rows(start), :] = outs[h]
        l_ref[rows(start), :] = lse

    for r in range(dil):
        put(r, *_attn_block(q_ref[r, 0:BAND, :], k_ref[r, 0:BAND, :], v_ref[r, 0:BAND, :],
                            lambda h: bias_ref[h, :, BAND:2 * BAND]))

        if nb > 1:
            def body(i, carry, r=r):
                r0 = pl.multiple_of(i * BAND, BAND)
                rk = pl.multiple_of((i - 1) * BAND, BAND)
                put(r0 * dil + r, *_attn_block(q_ref[r, pl.ds(r0, BAND), :], k_ref[r, pl.ds(rk, 2 * BAND), :],
                                               v_ref[r, pl.ds(rk, 2 * BAND), :], lambda h: bias_ref[h]))
                return carry

            lax.fori_loop(1, nb, body, 0)

    for h in range(HEADS_PER_GROUP):
        o_ref[:, h * HEAD_DIM:(h + 1) * HEAD_DIM] = o_scr[h].astype(BF16)


def _attn_prompt_group(qkv_sub, bias, g, batch, seq):
    dil = ATTN_GROUPS[g][1]
    n = seq // dil

    def sub(which):
        return pl.BlockSpec((None, None, dil, n, D_GROUP), lambda b: (which, b, 0, 0, 0))

    return pl.pallas_call(
        _attn_p_kernel,
        grid=(batch,),
        in_specs=[pl.BlockSpec((HEADS_PER_GROUP, BAND, 2 * BAND), lambda b: (0, 0, 0)),
                  sub(0), sub(1), sub(2)],
        out_specs=[pl.BlockSpec((seq, D_GROUP), lambda b: (b, 0)),
                   pl.BlockSpec((seq, HEAD_DIM), lambda b: (b, 0))],
        out_shape=[jax.ShapeDtypeStruct((batch * seq, D_GROUP), BF16),
                   jax.ShapeDtypeStruct((batch * seq, HEAD_DIM), F32)],
        scratch_shapes=[pltpu.VMEM((HEADS_PER_GROUP, seq, HEAD_DIM), F32)],
        compiler_params=_cparams("arbitrary"),
        name=f"attn_prompt_g{g}",
    )(bias, qkv_sub, qkv_sub, qkv_sub)


def _merge_heads(outs, lses):
    cols = []
    for h in range(HEADS_PER_GROUP):
        sl = slice(h * HEAD_DIM, (h + 1) * HEAD_DIM)
        l = [lg[:, h:h + 1] for lg in lses]
        mx = jnp.maximum(jnp.maximum(l[0], l[1]), l[2])
        e = [jnp.exp(x - mx) for x in l]
        tot = e[0] + e[1] + e[2]
        cols.append(sum((e[g] / tot) * outs[g][:, sl].astype(F32) for g in range(N_GROUPS)))
    return jnp.concatenate(cols, axis=-1)


def _mix_tail(o_attn, p, gates_a, gates_b, x, gt1, sh2, sc2, wua_ref, wup_ref, wo_ref, g2_ref, x1_ref, h2_ref):
    a = jnp.dot(o_attn.astype(BF16), wua_ref[...], preferred_element_type=F32)
    b = jnp.dot(p.astype(BF16), wup_ref[...], preferred_element_type=F32)
    mix = (gates_a * a + gates_b * b).astype(BF16)
    x1 = x + gt1 * jnp.dot(mix, wo_ref[...], preferred_element_type=F32)
    x1_ref[...] = x1
    h2_ref[...] = (_rms(x1) * g2_ref[...] * (1.0 + sc2) + sh2).astype(BF16)


def _pool_project(z_groups, wp_ref, ps_ref):
    cols = [jnp.dot(z.astype(BF16), wp_ref[g], preferred_element_type=F32) for g, z in enumerate(z_groups)]
    return jnp.concatenate(cols, axis=-1) * ps_ref[...]


def _mix_p_kernel(o0_ref, o1_ref, o2_ref, l0_ref, l1_ref, l2_ref, u_ref, uh_ref, gate_ref, x_ref,
                  gt1_ref, sh2_ref, sc2_ref, wua_ref, wp_ref, ps_ref, wup_ref, wo_ref, g2_ref,
                  x1_ref, h2_ref, ext_scr, *, tiles_per_batch):
    tm = u_ref.shape[0]
    t = pl.program_id(0) % tiles_per_batch
    o_attn = _merge_heads([o0_ref[...], o1_ref[...], o2_ref[...]], [l0_ref[...], l1_ref[...], l2_ref[...]])

    u = u_ref[...]
    ext_scr[0:POOL_HALO, :] = jnp.where(t == 0, 0.0, uh_ref[...])
    ext_scr[POOL_HALO:, :] = u
    pos = t * tm + lax.broadcasted_iota(jnp.int32, (tm, 1), 0)
    zs = []
    for g, w in enumerate(POOL_WINDOWS):
        sl = slice(g * POOL_GROUP, (g + 1) * POOL_GROUP)
        win = u[:, sl]
        for jj in range(1, w):
            win = win + ext_scr[POOL_HALO - jj:POOL_HALO - jj + tm, sl]
        cnt = jnp.minimum(pos + 1, w).astype(F32)
        zs.append(win / cnt - u[:, sl])
    p = _pool_project(zs, wp_ref, ps_ref)

    gates = gate_ref[...]
    _mix_tail(o_attn, p, gates[:, :D_MODEL].astype(F32), gates[:, D_MODEL:].astype(F32), x_ref[...],
              gt1_ref[...], sh2_ref[...], sc2_ref[...], wua_ref, wup_ref, wo_ref, g2_ref, x1_ref, h2_ref)


def _const_spec(shape):
    nd = len(shape)
    return pl.BlockSpec(shape, lambda *_: (0,) * nd)


def _mix_prompt(outs, lses, u, gates, x2d, mod_p, w_up_attn, w_pool, pool_scale, w_up_pool, w_out, norm_g2, seq,
                hosted):
    m = x2d.shape[0]
    tm = TM_MIX
    tpb = seq // tm
    hb = tm // POOL_HALO

    def row(width):
        return pl.BlockSpec((tm, width), lambda mi: (mi, 0))

    def mod(c):
        return pl.BlockSpec((None, 1, D_MODEL), lambda mi: (mi // tpb, 0, c))

    in_specs = ([row(D_GROUP)] * 3 + [row(HEAD_DIM)] * 3 + [
        row(D_POOL),
        pl.BlockSpec((POOL_HALO, D_POOL), lambda mi: (jnp.maximum(mi * hb - 1, 0), 0)),
        row(2 * D_MODEL), row(D_MODEL), mod(2), mod(3), mod(4),
        _const_spec((D_GROUP, D_MODEL)), _const_spec((len(POOL_WINDOWS), POOL_GROUP, POOL_GROUP)),
        _const_spec((1, D_POOL)), _const_spec((D_POOL, D_MODEL)), _const_spec((D_MODEL, D_MODEL)),
        _const_spec((1, D_MODEL))])
    s_args, s_in_specs, s_out_shape, s_out_specs, s_scratch, aliases = _stream_operands(hosted, len(in_specs), 2)
    kernel = _host_streams(functools.partial(_mix_p_kernel, tiles_per_batch=tpb), len(in_specs), 2, 1, hosted,
                           m // tm, lambda: pl.program_id(0))
    return pl.pallas_call(
        kernel,
        grid=(m // tm,),
        in_specs=in_specs + s_in_specs,
        out_specs=[row(D_MODEL), row(D_MODEL)] + s_out_specs,
        out_shape=[jax.ShapeDtypeStruct((m, D_MODEL), F32), jax.ShapeDtypeStruct((m, D_MODEL), BF16)] + s_out_shape,
        scratch_shapes=[pltpu.VMEM((POOL_HALO + tm, D_POOL), F32)] + s_scratch,
        input_output_aliases=aliases,
        compiler_params=_cparams("arbitrary"),
        name="mix_prompt",
    )(*outs, *lses, u, u, gates, x2d, mod_p, mod_p, mod_p, w_up_attn, w_pool, pool_scale, w_up_pool, w_out, norm_g2,
      *s_args)


def _mlp_kernel(h_ref, wu_ref, wd_ref, x1_ref, gt_ref, gf_ref, y_ref, acc_ref):
    f = pl.program_id(1)
    a = jnp.dot(h_ref[...], wu_ref[...], preferred_element_type=F32)
    a = jnp.square(jnp.maximum(a, 0.0)).astype(BF16)
    part = jnp.dot(a, wd_ref[...], preferred_element_type=F32)

    @pl.when(f == 0)
    def _():
        acc_ref[...] = part

    @pl.when(f > 0)
    def _():
        acc_ref[...] += part

    @pl.when(f == pl.num_programs(1) - 1)
    def _():
        x2 = x1_ref[...] + gt_ref[...] * acc_ref[...]
        y_ref[...] = _rms(x2) * gf_ref[...]


def _mlp_sample(h2, x1, mod_s, w_up, w_down, norm_gf):
    m = h2.shape[0]
    tf = TF_MLP_SAMPLE
    return pl.pallas_call(
        _mlp_kernel,
        grid=(1, D_FF // tf),
        in_specs=[pl.BlockSpec((m, D_MODEL), lambda mi, f: (0, 0)),
                  pl.BlockSpec((D_MODEL, tf), lambda mi, f: (0, f)),
                  pl.BlockSpec((tf, D_MODEL), lambda mi, f: (f, 0)),
                  pl.BlockSpec((m, D_MODEL), lambda mi, f: (0, 0)),
                  pl.BlockSpec((m, D_MODEL), lambda mi, f: (0, 5)),
                  pl.BlockSpec((1, D_MODEL), lambda mi, f: (0, 0))],
        out_specs=pl.BlockSpec((m, D_MODEL), lambda mi, f: (0, 0)),
        out_shape=jax.ShapeDtypeStruct((m, D_MODEL), F32),
        scratch_shapes=[pltpu.VMEM((m, D_MODEL), F32)],
        compiler_params=_cparams("arbitrary", "arbitrary"),
        name="mlp_sample",
    )(h2, w_up, w_down, x1, mod_s, norm_gf)


def _mlp_p_kernel(h_ref, wu_ref, wd_ref, gt_ref, gf_ref, x1_hbm, y_hbm, acc_ref, xy_ref, sems):
    mi, f = pl.program_id(0), pl.program_id(1)
    nm, nf = pl.num_programs(0), pl.num_programs(1)
    tm = acc_ref.shape[0]

    def x1_copy(i):
        return pltpu.make_async_copy(x1_hbm.at[pl.ds(i * tm, tm)], xy_ref, sems.at[0])

    def y_copy(i):
        return pltpu.make_async_copy(xy_ref, y_hbm.at[pl.ds(i * tm, tm)], sems.at[1])

    @pl.when(f == 1)
    def _():
        @pl.when(mi > 0)
        def _():
            y_copy(mi - 1).wait()

        x1_copy(mi).start()

    @pl.when(f == 0)
    def _():
        acc_ref[...] = jnp.zeros_like(acc_ref)

    a = jnp.dot(h_ref[...], wu_ref[...], preferred_element_type=F32)
    a = jnp.square(jnp.maximum(a, 0.0)).astype(BF16)
    acc_ref[...] += jnp.dot(a, wd_ref[...], preferred_element_type=F32)

    @pl.when(f == nf - 1)
    def _():
        x1_copy(mi).wait()
        x2 = xy_ref[...] + gt_ref[...] * acc_ref[...]
        xy_ref[...] = _rms(x2) * gf_ref[...]
        y_copy(mi).start()

        @pl.when(mi == nm - 1)
        def _():
            y_copy(mi).wait()


def _mlp_prompt(h2, x1, mod_p, w_up, w_down, norm_gf, seq, hosted):
    m = h2.shape[0]
    tm, tf = TM_MLP, TF_MLP
    tpb = seq // tm
    nm, nf = m // tm, D_FF // tf
    assert nf >= 3
    any_spec = pl.BlockSpec(memory_space=pl.ANY)
    s_args, s_in_specs, s_out_shape, s_out_specs, s_scratch, aliases = _stream_operands(hosted, 6, 1)
    kernel = _host_streams(_mlp_p_kernel, 6, 1, 3, hosted, nm * nf,
                           lambda: pl.program_id(0) * nf + pl.program_id(1))
    return pl.pallas_call(
        kernel,
        grid=(nm, nf),
        in_specs=[pl.BlockSpec((tm, D_MODEL), lambda mi, f: (mi, 0)),
                  pl.BlockSpec((D_MODEL, tf), lambda mi, f: (0, f)),
                  pl.BlockSpec((tf, D_MODEL), lambda mi, f: (f, 0)),
                  pl.BlockSpec((None, 1, D_MODEL), lambda mi, f: (mi // tpb, 0, 5)),
                  pl.BlockSpec((1, D_MODEL), lambda mi, f: (0, 0)),
                  any_spec] + s_in_specs,
        out_specs=[any_spec] + s_out_specs,
        out_shape=[jax.ShapeDtypeStruct((m, D_MODEL), F32)] + s_out_shape,
        scratch_shapes=[pltpu.VMEM((tm, D_MODEL), F32), pltpu.VMEM((tm, D_MODEL), F32),
                        pltpu.SemaphoreType.DMA((2,))] + s_scratch,
        input_output_aliases=aliases,
        compiler_params=_cparams("arbitrary", "arbitrary"),
        name="mlp_prompt",
    )(h2, w_up, w_down, mod_p, norm_gf, x1, *s_args)


def _inproj_s_kernel(x_ref, sh_ref, sc_ref, g_ref, w_ref, o_ref, h_scr):
    @pl.when(pl.program_id(0) == 0)
    def _():
        y = _rms(x_ref[...]) * g_ref[...]
        h_scr[...] = (y * (1.0 + sc_ref[...]) + sh_ref[...]).astype(BF16)

    o_ref[...] = jnp.dot(h_scr[...], w_ref[...], preferred_element_type=F32)


def _inproj_sample(x_s, mod_s, norm_g, w_in):
    m = x_s.shape[0]
    tn = D_ATTN
    return pl.pallas_call(
        _inproj_s_kernel,
        grid=(D_IN // tn,),
        in_specs=[pl.BlockSpec((m, D_MODEL), lambda j: (0, 0)),
                  pl.BlockSpec((m, D_MODEL), lambda j: (0, 0)),
                  pl.BlockSpec((m, D_MODEL), lambda j: (0, 1)),
                  pl.BlockSpec((1, D_MODEL), lambda j: (0, 0)),
                  pl.BlockSpec((D_MODEL, tn), lambda j: (0, j))],
        out_specs=pl.BlockSpec((m, tn), lambda j: (0, j)),
        out_shape=jax.ShapeDtypeStruct((m, D_IN), F32),
        scratch_shapes=[pltpu.VMEM((m, D_MODEL), BF16)],
        compiler_params=_cparams("arbitrary"),
        name="inproj_sample",
    )(x_s, mod_s, mod_s, norm_g, w_in)


def _attn_s_kernel(bias_ref, q_ref, kn_ref, vn_ref, k0_ref, v0_ref, k1_ref, v1_ref, k2_ref, v2_ref, o_ref):
    k_refs = (k0_ref, k1_ref, k2_ref)
    v_refs = (v0_ref, v1_ref, v2_ref)

    def body(s, carry):
        outs, lses = [], []
        for g in range(N_GROUPS):
            hs = slice(g * HEADS_PER_GROUP, (g + 1) * HEADS_PER_GROUP)
            q = q_ref[s, hs, :]
            kn = kn_ref[s, hs, :]
            vn = vn_ref[s, hs, :]
            kt = k_refs[g][s]
            vt = v_refs[g][s]
            sh = jnp.sum(kt * q[None], axis=-1, keepdims=True) * SCALE + bias_ref[g]
            sn = jnp.sum(kn * q, axis=-1, keepdims=True) * SCALE
            mx = jnp.maximum(jnp.max(sh, axis=0), sn)
            p = jnp.exp(sh - mx[None])
            pn = jnp.exp(sn - mx)
            den = jnp.sum(p, axis=0) + pn
            num = jnp.sum(p * vt, axis=0) + pn * vn
            outs.append(num / den)
            lses.append(mx + jnp.log(den))
        mx = jnp.maximum(jnp.maximum(lses[0], lses[1]), lses[2])
        e = [jnp.exp(l - mx) for l in lses]
        tot = e[0] + e[1] + e[2]
        o_ref[s] = (e[0] * outs[0] + e[1] * outs[1] + e[2] * outs[2]) / tot
        return carry

    lax.fori_loop(0, q_ref.shape[0], body, 0)


def _attn_sample(bias_s, q3, kn3, vn3, caches):
    nb = q3.shape[0]
    bs = BS_ATTN
    tok = pl.BlockSpec((bs, N_HEADS, HEAD_DIM), lambda i: (i, 0, 0))
    in_specs = [_const_spec((N_GROUPS, BAND, HEADS_PER_GROUP, HEAD_DIM)), tok, tok, tok]
    args = [bias_s, q3, kn3, vn3]
    for c, (_, dil) in zip(caches, ATTN_GROUPS):
        cv = c.reshape(nb, BAND, dil * 2, HEADS_PER_GROUP, HEAD_DIM)
        for kv in range(2):
            in_specs.append(pl.BlockSpec((bs, BAND, None, HEADS_PER_GROUP, HEAD_DIM),
                                         lambda i, kv=kv: (i, 0, kv, 0, 0)))
            args.append(cv)
    return pl.pallas_call(
        _attn_s_kernel,
        grid=(nb // bs,),
        in_specs=in_specs,
        out_specs=pl.BlockSpec((bs, HEADS_PER_GROUP, HEAD_DIM), lambda i: (i, 0, 0)),
        out_shape=jax.ShapeDtypeStruct((nb, HEADS_PER_GROUP, HEAD_DIM), F32),
        compiler_params=_cparams("arbitrary"),
        name="attn_sample",
    )(*args)


def _mix_s_kernel(o_ref, u_ref, hist_ref, ga_ref, gb_ref, x_ref, gt1_ref, sh2_ref, sc2_ref,
                  wua_ref, wp_ref, ps_ref, wup_ref, wo_ref, g2_ref, x1_ref, h2_ref, np_ref):
    u = u_ref[...]
    zs = []
    for g, w in enumerate(POOL_WINDOWS):
        sl = slice(g * POOL_GROUP, (g + 1) * POOL_GROUP)
        win = u[:, sl]
        for jj in range(1, w):
            win = win + hist_ref[:, POOL_HIST - jj, sl]
        zs.append(win / float(w) - u[:, sl])
    p = _pool_project(zs, wp_ref, ps_ref)
    np_ref[:, 0:POOL_HIST - 1, :] = hist_ref[:, 1:POOL_HIST, :]
    np_ref[:, POOL_HIST - 1, :] = u
    _mix_tail(o_ref[...], p, jax.nn.sigmoid(ga_ref[...]), jax.nn.sigmoid(gb_ref[...]), x_ref[...],
              gt1_ref[...], sh2_ref[...], sc2_ref[...], wua_ref, wup_ref, wo_ref, g2_ref, x1_ref, h2_ref)


def _mix_sample(o_attn, proj_s, hist, x_s, mod_s, w_up_attn, w_pool, pool_scale, w_up_pool, w_out, norm_g2):
    m = x_s.shape[0]

    def cols(width, c):
        return pl.BlockSpec((m, width), lambda i: (0, c))

    u0 = 3 * D_ATTN
    u = lax.slice_in_dim(proj_s, u0, u0 + D_POOL, axis=1)
    ga = lax.slice_in_dim(proj_s, u0 + D_POOL, u0 + D_POOL + D_MODEL, axis=1)
    gb = lax.slice_in_dim(proj_s, u0 + D_POOL + D_MODEL, D_IN, axis=1)
    in_specs = [cols(D_GROUP, 0), cols(D_POOL, 0), _const_spec((m, POOL_HIST, D_POOL)),
                cols(D_MODEL, 0), cols(D_MODEL, 0),
                cols(D_MODEL, 0), cols(D_MODEL, 2), cols(D_MODEL, 3), cols(D_MODEL, 4),
                _const_spec((D_GROUP, D_MODEL)), _const_spec((len(POOL_WINDOWS), POOL_GROUP, POOL_GROUP)),
                _const_spec((1, D_POOL)), _const_spec((D_POOL, D_MODEL)), _const_spec((D_MODEL, D_MODEL)),
                _const_spec((1, D_MODEL))]
    return pl.pallas_call(
        _mix_s_kernel,
        grid=(1,),
        in_specs=in_specs,
        out_specs=[_const_spec((m, D_MODEL)), _const_spec((m, D_MODEL)), _const_spec((m, POOL_HIST, D_POOL))],
        out_shape=[jax.ShapeDtypeStruct((m, D_MODEL), F32), jax.ShapeDtypeStruct((m, D_MODEL), BF16),
                   jax.ShapeDtypeStruct((m, POOL_HIST, D_POOL), F32)],
        compiler_params=_cparams("arbitrary"),
        name="mix_sample",
    )(o_attn, u, hist, ga, gb, x_s, mod_s, mod_s, mod_s, w_up_attn, w_pool, pool_scale, w_up_pool, w_out, norm_g2)


class _ShiftStream:
    def __init__(self, cache, new, out, buf, sems, rows, ns, lo, hi):
        self.cache, self.new, self.out, self.buf, self.sems = cache, new, out, buf, sems
        self.rows, self.ns = rows, ns
        self.slots = buf.shape[0]
        self.cps = cache.shape[2] // rows
        self.lo, self.hi = lo, hi

    @staticmethod
    def total_chunks(cache_shape, rows, ns):
        nb, w = cache_shape[1], cache_shape[2]
        assert w % rows == 0 and nb % ns == 0 and (ns == 1 or rows == w)
        return (nb // ns) * (w // rows)

    def _where(self, k):
        return (k // self.cps) * self.ns, (k % self.cps) * self.rows, k % self.slots

    def _body_in(self, k):
        b0, w0, slot = self._where(k)
        return pltpu.make_async_copy(self.cache.at[0, pl.ds(b0, self.ns), pl.ds(w0 + 1, self.rows - 1)],
                                     self.buf.at[slot, :, pl.ds(0, self.rows - 1)], self.sems.at[0, slot])

    def _next_in(self, k):
        b0, w0, slot = self._where(k)
        return pltpu.make_async_copy(self.cache.at[0, pl.ds(b0, self.ns), pl.ds(w0 + self.rows, 1)],
                                     self.buf.at[slot, :, pl.ds(self.rows - 1, 1)], self.sems.at[1, slot])

    def _new_in(self, k):
        b0, _, slot = self._where(k)
        return pltpu.make_async_copy(self.new.at[pl.ds(b0, self.ns)],
                                     self.buf.at[slot, :, self.rows - 1], self.sems.at[1, slot])

    def _out(self, k):
        b0, w0, slot = self._where(k)
        return pltpu.make_async_copy(self.buf.at[slot],
                                     self.out.at[0, pl.ds(b0, self.ns), pl.ds(w0, self.rows)], self.sems.at[2, slot])

    def _last_row(self, k, act):
        if isinstance(k, int) or self.cps == 1:
            at_end = self.cps == 1 or k % self.cps == self.cps - 1
            getattr(self._new_in(k) if at_end else self._next_in(k), act)()
            return
        at_end = k % self.cps == self.cps - 1

        @pl.when(at_end)
        def _():
            getattr(self._new_in(k), act)()

        @pl.when(jnp.logical_not(at_end))
        def _():
            getattr(self._next_in(k), act)()

    def start_in(self, k):
        self._body_in(k).start()
        self._last_row(k, "start")

    def wait_in(self, k):
        self._body_in(k).wait()
        self._last_row(k, "wait")

    def start_out(self, k):
        self._out(k).start()

    def wait_out(self, k):
        self._out(k).wait()

    def step(self, k):
        self.wait_in(k)
        self.start_out(k)

        @pl.when(k >= self.lo + 1)
        def _():
            self.wait_out(k - 1)

        @pl.when(k + self.slots - 1 < self.hi)
        def _():
            self.start_in(k + self.slots - 1)

    def prime(self):
        for k in range(self.lo, min(self.lo + self.slots - 1, self.hi)):
            self.start_in(k)

    def drain(self):
        self.wait_out(self.hi - 1)

    def tick(self, s, n_steps):
        n = self.hi - self.lo
        per = -(-n // n_steps)
        stride = max(n_steps // n, 1)
        assert self.slots > per

        @pl.when(s == 0)
        def _():
            self.prime()

        for i in range(per):
            k = self.lo + (s // stride) * per + i

            @pl.when((s % stride == 0) & (k < self.hi))
            def _(k=k):
                self.step(k)

        @pl.when(s == n_steps - 1)
        def _():
            self.drain()


class _Hosted(NamedTuple):
    cache: jax.Array
    new: jax.Array
    prev: object
    rows: int
    ns: int
    slots: int
    frac_lo: float
    frac_hi: float

    def chunk_range(self):
        total = _ShiftStream.total_chunks(self.cache.shape, self.rows, self.ns)
        return int(round(self.frac_lo * total)), int(round(self.frac_hi * total))


def _host_streams(body, n_in, n_out, n_scr, hosted, n_steps, step_index):
    n_sin = sum(2 if h.prev is None else 3 for h in hosted)
    plans = [(h.rows, h.ns, *h.chunk_range(), h.prev is not None) for h in hosted]

    def kernel(*refs):
        refs = list(refs)
        ins, s_in = refs[:n_in], refs[n_in:n_in + n_sin]
        o0 = n_in + n_sin
        outs, s_out = refs[o0:o0 + n_out], refs[o0 + n_out:o0 + n_out + len(plans)]
        c0 = o0 + n_out + len(plans)
        scr, s_scr = refs[c0:c0 + n_scr], refs[c0 + n_scr:]
        s = step_index()
        pos = 0
        for i, (rows, ns, lo, hi, has_prev) in enumerate(plans):
            _ShiftStream(s_in[pos], s_in[pos + 1], s_out[i], s_scr[2 * i], s_scr[2 * i + 1],
                         rows, ns, lo, hi).tick(s, n_steps)
            pos += 3 if has_prev else 2
        body(*ins, *outs, *scr)

    return kernel


def _stream_operands(hosted, n_in, n_out):
    any_spec = pl.BlockSpec(memory_space=pl.ANY)
    args, aliases = [], {}
    for i, h in enumerate(hosted):
        args += [h.cache, h.new]
        if h.prev is not None:
            aliases[n_in + len(args)] = n_out + i
            args.append(h.prev)
    out_shape = [jax.ShapeDtypeStruct(h.cache.shape, h.cache.dtype) for h in hosted]
    scratch = [s for h in hosted for s in (
        pltpu.VMEM((h.slots, h.ns, h.rows, 2, HEADS_PER_GROUP, HEAD_DIM), F32),
        pltpu.SemaphoreType.DMA((3, h.slots)))]
    return args, [any_spec] * len(args), out_shape, [any_spec] * len(hosted), scratch, aliases


def _alibi_slopes():
    h = jnp.arange(1, N_HEADS + 1, dtype=F32)
    return jnp.exp2(-ALIBI_MAX_BIAS * h / N_HEADS)


def _prompt_bias(g):
    dil = ATTN_GROUPS[g][1]
    slopes = _alibi_slopes()[g * HEADS_PER_GROUP:(g + 1) * HEADS_PER_GROUP]
    a = jnp.arange(BAND)[:, None]
    b = jnp.arange(2 * BAND)[None, :]
    dist = a - b + BAND
    valid = (dist >= 0) & (dist <= BAND)
    bias = -slopes[:, None, None] * (dist * dil).astype(F32)
    return jnp.where(valid[None], bias, NEG)


def _sample_bias():
    slopes = _alibi_slopes().reshape(N_GROUPS, 1, HEADS_PER_GROUP, 1)
    dil = jnp.array([d for _, d in ATTN_GROUPS], F32).reshape(N_GROUPS, 1, 1, 1)
    back = (BAND - jnp.arange(BAND, dtype=F32)).reshape(1, BAND, 1, 1)
    return jnp.broadcast_to(-slopes * (back * dil), (N_GROUPS, BAND, HEADS_PER_GROUP, HEAD_DIM))


def kernel(x_prompt, x_sample, c_prompt, c_sample, cache_kv_w128, cache_kv_w512, cache_kv_w2048, state_pool,
           norm_mix_g, w_ada, b_ada, w_in, w_up_attn, w_pool, pool_scale, w_up_pool, w_out, norm_mlp_g,
           w_mlp_up, w_mlp_down, norm_final_g):
    batch, seq, _ = x_prompt.shape
    nb = x_sample.shape[0]
    depth = w_in.shape[0]
    assert depth == 1 and x_sample.shape[1] == 1
    caches = (cache_kv_w128, cache_kv_w512, cache_kv_w2048)
    for c, (w, dil) in zip(caches, ATTN_GROUPS):
        assert c.shape[2] == w == BAND * dil

    w_in_b = w_in[0].astype(BF16)
    w_ua_b = w_up_attn[0].astype(BF16)
    w_pool_b = w_pool[0].astype(BF16)
    w_up_b = w_up_pool[0].astype(BF16)
    w_out_b = w_out[0].astype(BF16)
    w_mu_b = w_mlp_up[0].astype(BF16)
    w_md_b = w_mlp_down[0].astype(BF16)
    g1 = norm_mix_g[0].reshape(1, D_MODEL)
    g2 = norm_mlp_g[0].reshape(1, D_MODEL)
    gf = norm_final_g.reshape(1, D_MODEL)
    ps = pool_scale[0].reshape(1, D_POOL)

    mod = _ada(jnp.concatenate([c_sample, c_prompt], axis=0), w_ada[0], b_ada[0])
    mod_s = mod[:nb]
    mod_p = mod[nb:].reshape(batch, 1, 6 * D_MODEL)

    xs = x_sample.reshape(nb, D_MODEL)
    proj_s = _inproj_sample(xs, mod_s, g1, w_in_b)
    q3 = proj_s[:, 0:D_ATTN].reshape(nb, N_HEADS, HEAD_DIM)
    kn3 = proj_s[:, D_ATTN:2 * D_ATTN].reshape(nb, N_HEADS, HEAD_DIM)
    vn3 = proj_s[:, 2 * D_ATTN:3 * D_ATTN].reshape(nb, N_HEADS, HEAD_DIM)
    o_s = _attn_sample(_sample_bias(), q3, kn3, vn3, [c[0] for c in caches])
    new_rows = [jnp.stack([kn3[:, g * HEADS_PER_GROUP:(g + 1) * HEADS_PER_GROUP],
                           vn3[:, g * HEADS_PER_GROUP:(g + 1) * HEADS_PER_GROUP]], axis=1)
                for g in range(N_GROUPS)]
    x1_s, h2_s, pool_s = _mix_sample(o_s.reshape(nb, D_GROUP), proj_s, state_pool[0], xs, mod_s,
                                     w_ua_b, w_pool_b, ps, w_up_b, w_out_b, g2)
    y_s = _mlp_sample(h2_s, x1_s, mod_s, w_mu_b, w_md_b, gf)

    x2d = x_prompt.reshape(batch * seq, D_MODEL)
    *qkv_sub, u_p, gates, kv2, kv1, kv0, kv_s1 = _inproj_prompt(
        x2d, mod_p, g1, w_in_b, batch, seq,
        (_Hosted(caches[1], new_rows[1], None, rows=256, ns=1, slots=4, frac_lo=0.0, frac_hi=1.0),))
    outs, lses = [], []
    for g in range(N_GROUPS):
        o, l = _attn_prompt_group(qkv_sub[g], _prompt_bias(g), g, batch, seq)
        outs.append(o)
        lses.append(l)
    x1_p, h2_p, kv_s0 = _mix_prompt(
        outs, lses, u_p, gates, x2d, mod_p, w_ua_b, w_pool_b, ps, w_up_b, w_out_b, g2, seq,
        (_Hosted(caches[0], new_rows[0], None, rows=128, ns=4, slots=3, frac_lo=0.0, frac_hi=1.0),))
    y_p, kv_s2 = _mlp_prompt(
        h2_p, x1_p, mod_p, w_mu_b, w_md_b, gf, seq,
        (_Hosted(caches[2], new_rows[2], None, rows=512, ns=1, slots=3, frac_lo=0.0, frac_hi=1.0),))

    pool_p = u_p.reshape(batch, seq, D_POOL)[:, seq - POOL_HIST:][None]
    return (y_p.reshape(batch, seq, D_MODEL), y_s.reshape(nb, 1, D_MODEL), kv0, kv1, kv2, pool_p,
            kv_s0, kv_s1, kv_s2, pool_s[None])
```

```python
import functools
from typing import NamedTuple

import jax
import jax.numpy as jnp
from jax import lax
from jax.experimental import pallas as pl
from jax.experimental.pallas import tpu as pltpu

F32 = jnp.float32
BF16 = jnp.bfloat16

D_MODEL = 2048
HEAD_DIM = 128
HEADS_PER_GROUP = 4
ATTN_GROUPS = ((128, 1), (512, 4), (2048, 16))
N_GROUPS = len(ATTN_GROUPS)
N_HEADS = HEADS_PER_GROUP * N_GROUPS
D_ATTN = N_HEADS * HEAD_DIM
D_GROUP = HEADS_PER_GROUP * HEAD_DIM
BAND = 128
POOL_WINDOWS = (2, 4, 8, 16)
POOL_GROUP = 128
D_POOL = POOL_GROUP * len(POOL_WINDOWS)
POOL_HIST = max(POOL_WINDOWS) - 1
POOL_HALO = 16
D_FF = 4 * D_MODEL
D_IN = 3 * D_ATTN + D_POOL + 2 * D_MODEL
ALIBI_MAX_BIAS = 8.0
EPS = 1e-6
SCALE = HEAD_DIM ** -0.5
NEG = -1e30

VMEM_LIMIT_BYTES = 60 * 1024 * 1024

IN_TILE = D_GROUP
N_IN_TILES = D_IN // IN_TILE
QKV_TILES = 3 * N_GROUPS
POOL_TILE = QKV_TILES
GATE_TILE0 = POOL_TILE + 1

TM_IN = 1024
IN_SPLIT = 2
TM_MIX = 256
TM_MLP = 1024
TF_MLP = 512
TF_MLP_SAMPLE = 1024
BS_ATTN = 8
ATTN_UNROLL = 3


def _cparams(*sem):
    return pltpu.CompilerParams(dimension_semantics=sem, vmem_limit_bytes=VMEM_LIMIT_BYTES)


def _rms(x):
    return x * lax.rsqrt(jnp.mean(x * x, axis=-1, keepdims=True) + EPS)


def _ada_kernel(c_ref, w_ref, b_ref, o_ref):
    c = c_ref[...]
    a = (c * jax.nn.sigmoid(c)).astype(BF16)
    o_ref[...] = jnp.dot(a, w_ref[...].astype(BF16), preferred_element_type=F32) + b_ref[...]


def _ada(c_all, w_ada, b_ada):
    m = c_all.shape[0]
    tn = 1024
    return pl.pallas_call(
        _ada_kernel,
        grid=(6 * D_MODEL // tn,),
        in_specs=[pl.BlockSpec((m, D_MODEL), lambda j: (0, 0)),
                  pl.BlockSpec((D_MODEL, tn), lambda j: (0, j)),
                  pl.BlockSpec((1, tn), lambda j: (0, j))],
        out_specs=pl.BlockSpec((m, tn), lambda j: (0, j)),
        out_shape=jax.ShapeDtypeStruct((m, 6 * D_MODEL), F32),
        compiler_params=_cparams("arbitrary"),
        name="ada",
    )(c_all, w_ada, b_ada.reshape(1, 6 * D_MODEL))


def _store_heads(ref, row0, val):
    for h in range(HEADS_PER_GROUP):
        ref[row0:row0 + val.shape[0], h, :] = val[:, h * HEAD_DIM:(h + 1) * HEAD_DIM]


def _inproj_p_kernel(x_ref, sh_ref, sc_ref, g_ref, w_ref,
                     qkv0_ref, qkv1_ref, qkv2_ref, u_ref, gate_ref, kv2_ref, kv1_ref, kv0_ref,
                     h_scr, acc_scr, *, tiles_per_batch):
    j = pl.program_id(1)
    tm = x_ref.shape[0]

    @pl.when(j == 0)
    def _():
        y = _rms(x_ref[...]) * g_ref[...]
        h_scr[...] = (y * (1.0 + sc_ref[...]) + sh_ref[...]).astype(BF16)

    th = tm // IN_SPLIT

    def halves():
        for s in range(IN_SPLIT):
            yield s, jnp.dot(h_scr[s * th:(s + 1) * th, :], w_ref[...], preferred_element_type=F32)

    def store_sub(qkv_ref, dil, s, acc):
        if dil == 1:
            qkv_ref[0, s * th:(s + 1) * th, :] = acc.astype(BF16)
            return
        n = th // dil
        for h in range(HEADS_PER_GROUP):
            sl = slice(h * HEAD_DIM, (h + 1) * HEAD_DIM)
            acc_scr[h, s * th:(s + 1) * th, :] = acc[:, sl]
            for r in range(dil):
                qkv_ref[r, s * n:(s + 1) * n, sl] = acc_scr[h, pl.ds(s * th + r, n, stride=dil), :].astype(BF16)

    last = pl.program_id(0) % tiles_per_batch == tiles_per_batch - 1
    kv_refs = (kv0_ref, kv1_ref, kv2_ref)
    for g, (qkv_ref, (_, dil)) in enumerate(zip((qkv0_ref, qkv1_ref, qkv2_ref), ATTN_GROUPS)):
        @pl.when(j == g)
        def _(qkv_ref=qkv_ref, dil=dil):
            for s, acc in halves():
                store_sub(qkv_ref, dil, s, acc)

        @pl.when((j == N_GROUPS + g) | (j == 2 * N_GROUPS + g))
        def _(g=g, qkv_ref=qkv_ref, dil=dil):
            kv_ref = kv_refs[g]
            first_kept = tm - kv_ref.shape[0]
            for s, acc in halves():
                store_sub(qkv_ref, dil, s, acc)
                row0 = max(s * th, first_kept)
                if first_kept == 0:
                    _store_heads(kv_ref, s * th, acc)
                elif row0 < (s + 1) * th:
                    @pl.when(last)
                    def _(acc=acc, row0=row0, s=s):
                        _store_heads(kv_ref, row0 - first_kept, acc[row0 - s * th:, :])

    @pl.when(j == POOL_TILE)
    def _():
        for s, acc in halves():
            u_ref[s * th:(s + 1) * th, :] = acc

    @pl.when(j >= GATE_TILE0)
    def _():
        for s, acc in halves():
            gate_ref[s * th:(s + 1) * th, :] = (0.5 * jnp.tanh(0.5 * acc) + 0.5).astype(BF16)


def _inproj_prompt(x2d, mod_p, norm_g, w_in, batch, seq, hosted):
    m = x2d.shape[0]
    tm = TM_IN
    tpb = seq // tm
    keep = [min(w, seq) for w, _ in ATTN_GROUPS]
    assert keep[2] == seq and tm >= keep[1] and seq % tm == 0

    def sub_spec(g):
        dil = ATTN_GROUPS[g][1]
        return pl.BlockSpec((None, None, dil, tm // dil, D_GROUP),
                            lambda mi, j: (jnp.clip((j - g) // N_GROUPS, 0, 2), mi // tpb, 0, mi % tpb, 0))

    def sub_shape(g):
        dil = ATTN_GROUPS[g][1]
        return jax.ShapeDtypeStruct((3, batch, dil, seq // dil, D_GROUP), BF16)

    def kv_spec(rows, first_v_tile, tail_only):
        def index(mi, j):
            is_v = j >= first_v_tile
            if tail_only:
                return (0, mi // tpb, 0, jnp.where(is_v & (mi % tpb == tpb - 1), 1, 0), 0, 0)
            return (0, mi // tpb, mi % tpb, jnp.where(is_v, 1, 0), 0, 0)
        return pl.BlockSpec((None, None, rows, None, HEADS_PER_GROUP, HEAD_DIM), index)

    def kv_shape(rows):
        return jax.ShapeDtypeStruct((1, batch, rows, 2, HEADS_PER_GROUP, HEAD_DIM), F32)

    in_specs = [
        pl.BlockSpec((tm, D_MODEL), lambda mi, j: (mi, 0)),
        pl.BlockSpec((None, 1, D_MODEL), lambda mi, j: (mi // tpb, 0, 0)),
        pl.BlockSpec((None, 1, D_MODEL), lambda mi, j: (mi // tpb, 0, 1)),
        pl.BlockSpec((1, D_MODEL), lambda mi, j: (0, 0)),
        pl.BlockSpec((D_MODEL, IN_TILE), lambda mi, j: (0, j)),
    ]
    out_specs = [
        sub_spec(0), sub_spec(1), sub_spec(2),
        pl.BlockSpec((tm, IN_TILE), lambda mi, j: (mi, 0)),
        pl.BlockSpec((tm, IN_TILE), lambda mi, j: (mi, jnp.clip(j - GATE_TILE0, 0, N_IN_TILES - GATE_TILE0 - 1))),
        kv_spec(tm, 2 * N_GROUPS + 2, False),
        kv_spec(keep[1], 2 * N_GROUPS + 1, True),
        kv_spec(keep[0], 2 * N_GROUPS, True),
    ]
    out_shape = [
        sub_shape(0), sub_shape(1), sub_shape(2),
        jax.ShapeDtypeStruct((m, D_POOL), F32),
        jax.ShapeDtypeStruct((m, 2 * D_MODEL), BF16),
        kv_shape(keep[2]), kv_shape(keep[1]), kv_shape(keep[0]),
    ]
    s_args, s_in_specs, s_out_shape, s_out_specs, s_scratch, aliases = _stream_operands(
        hosted, len(in_specs), len(out_specs))
    kernel = _host_streams(functools.partial(_inproj_p_kernel, tiles_per_batch=tpb), len(in_specs), len(out_specs),
                           2, hosted, (m // tm) * N_IN_TILES,
                           lambda: pl.program_id(0) * N_IN_TILES + pl.program_id(1))
    return pl.pallas_call(
        kernel,
        grid=(m // tm, N_IN_TILES),
        in_specs=in_specs + s_in_specs, out_specs=out_specs + s_out_specs, out_shape=out_shape + s_out_shape,
        scratch_shapes=[pltpu.VMEM((tm, D_MODEL), BF16),
                        pltpu.VMEM((HEADS_PER_GROUP, tm, HEAD_DIM), F32)] + s_scratch,
        input_output_aliases=aliases,
        compiler_params=_cparams("arbitrary", "arbitrary"),
        name="inproj_prompt",
    )(x2d, mod_p, mod_p, norm_g, w_in, *s_args)


def _attn_block(q, k, v, bias_fn):
    lane = lax.broadcasted_iota(jnp.int32, (BAND, HEAD_DIM), 1)
    heads = [slice(h * HEAD_DIM, (h + 1) * HEAD_DIM) for h in range(HEADS_PER_GROUP)]
    scores = [lax.dot_general(q[:, sl], k[:, sl], (((1,), (1,)), ((), ())), preferred_element_type=F32)
              for sl in heads]
    scores = [s * SCALE + bias_fn(h) for h, s in enumerate(scores)]
    maxes = [jnp.max(s, axis=-1, keepdims=True) for s in scores]
    probs = [jnp.exp(s - mx) for s, mx in zip(scores, maxes)]
    dens = [jnp.sum(p, axis=-1, keepdims=True) for p in probs]
    outs = [jnp.dot(p.astype(BF16), v[:, sl], preferred_element_type=F32) / den
            for p, sl, den in zip(probs, heads, dens)]
    lse = jnp.zeros((BAND, HEAD_DIM), F32)
    for h, (mx, den) in enumerate(zip(maxes, dens)):
        lse = jnp.where(lane == h, mx + jnp.log(den), lse)
    return outs, lse


def _attn_p_kernel(bias_ref, q_ref, k_ref, v_ref, o_ref, l_ref, o_scr):
    dil, n, _ = q_ref.shape
    nb = n // BAND

    def rows(start):
        return pl.ds(start, BAND) if dil == 1 else pl.ds(start, BAND, stride=dil)

    def put(start, outs, lse):
        for h in range(HEADS_PER_GROUP):
            o_scr[h, rows(start), :] = outs[h]
        l_ref[rows(start), :] = lse

    for r in range(dil):
        put(r, *_attn_block(q_ref[r, 0:BAND, :], k_ref[r, 0:BAND, :], v_ref[r, 0:BAND, :],
                            lambda h: bias_ref[h, :, BAND:2 * BAND]))

        if nb > 1:
            def body(i, carry, r=r):
                r0 = pl.multiple_of(i * BAND, BAND)
                rk = pl.multiple_of((i - 1) * BAND, BAND)
                put(r0 * dil + r, *_attn_block(q_ref[r, pl.ds(r0, BAND), :], k_ref[r, pl.ds(rk, 2 * BAND), :],
                                               v_ref[r, pl.ds(rk, 2 * BAND), :], lambda h: bias_ref[h]))
                return carry

            lax.fori_loop(1, nb, body, 0, unroll=ATTN_UNROLL)

    for h in range(HEADS_PER_GROUP):
        o_ref[:, h * HEAD_DIM:(h + 1) * HEAD_DIM] = o_scr[h].astype(BF16)


def _attn_prompt_group(qkv_sub, bias, g, batch, seq):
    dil = ATTN_GROUPS[g][1]
    n = seq // dil

    def sub(which):
        return pl.BlockSpec((None, None, dil, n, D_GROUP), lambda b: (which, b, 0, 0, 0))

    return pl.pallas_call(
        _attn_p_kernel,
        grid=(batch,),
        in_specs=[pl.BlockSpec((HEADS_PER_GROUP, BAND, 2 * BAND), lambda b: (0, 0, 0)),
                  sub(0), sub(1), sub(2)],
        out_specs=[pl.BlockSpec((seq, D_GROUP), lambda b: (b, 0)),
                   pl.BlockSpec((seq, HEAD_DIM), lambda b: (b, 0))],
        out_shape=[jax.ShapeDtypeStruct((batch * seq, D_GROUP), BF16),
                   jax.ShapeDtypeStruct((batch * seq, HEAD_DIM), F32)],
        scratch_shapes=[pltpu.VMEM((HEADS_PER_GROUP, seq, HEAD_DIM), F32)],
        compiler_params=_cparams("arbitrary"),
        name=f"attn_prompt_g{g}",
    )(bias, qkv_sub, qkv_sub, qkv_sub)


def _merge_heads(outs, lses):
    cols = []
    for h in range(HEADS_PER_GROUP):
        sl = slice(h * HEAD_DIM, (h + 1) * HEAD_DIM)
        l = [lg[:, h:h + 1] for lg in lses]
        mx = jnp.maximum(jnp.maximum(l[0], l[1]), l[2])
        e = [jnp.exp(x - mx) for x in l]
        tot = e[0] + e[1] + e[2]
        cols.append(sum((e[g] / tot) * outs[g][:, sl].astype(F32) for g in range(N_GROUPS)))
    return jnp.concatenate(cols, axis=-1)


def _mix_tail(o_attn, p, gates_a, gates_b, x, gt1, sh2, sc2, wua_ref, wup_ref, wo_ref, g2_ref, x1_ref, h2_ref):
    a = jnp.dot(o_attn.astype(BF16), wua_ref[...], preferred_element_type=F32)
    b = jnp.dot(p.astype(BF16), wup_ref[...], preferred_element_type=F32)
    mix = (gates_a * a + gates_b * b).astype(BF16)
    x1 = x + gt1 * jnp.dot(mix, wo_ref[...], preferred_element_type=F32)
    x1_ref[...] = x1
    h2_ref[...] = (_rms(x1) * g2_ref[...] * (1.0 + sc2) + sh2).astype(BF16)


def _pool_project(z_groups, wp_ref, ps_ref):
    cols = [jnp.dot(z.astype(BF16), wp_ref[g], preferred_element_type=F32) for g, z in enumerate(z_groups)]
    return jnp.concatenate(cols, axis=-1) * ps_ref[...]


def _mix_p_kernel(o0_ref, o1_ref, o2_ref, l0_ref, l1_ref, l2_ref, u_ref, uh_ref, gate_ref, x_ref,
                  gt1_ref, sh2_ref, sc2_ref, wua_ref, wp_ref, ps_ref, wup_ref, wo_ref, g2_ref,
                  x1_ref, h2_ref, ext_scr, *, tiles_per_batch):
    tm = u_ref.shape[0]
    t = pl.program_id(0) % tiles_per_batch
    o_attn = _merge_heads([o0_ref[...], o1_ref[...], o2_ref[...]], [l0_ref[...], l1_ref[...], l2_ref[...]])

    u = u_ref[...]
    ext_scr[0:POOL_HALO, :] = jnp.where(t == 0, 0.0, uh_ref[...])
    ext_scr[POOL_HALO:, :] = u
    pos = t * tm + lax.broadcasted_iota(jnp.int32, (tm, 1), 0)
    zs = []
    for g, w in enumerate(POOL_WINDOWS):
        sl = slice(g * POOL_GROUP, (g + 1) * POOL_GROUP)
        win = u[:, sl]
        for jj in range(1, w):
            win = win + ext_scr[POOL_HALO - jj:POOL_HALO - jj + tm, sl]
        cnt = jnp.minimum(pos + 1, w).astype(F32)
        zs.append(win / cnt - u[:, sl])
    p = _pool_project(zs, wp_ref, ps_ref)

    gates = gate_ref[...]
    _mix_tail(o_attn, p, gates[:, :D_MODEL].astype(F32), gates[:, D_MODEL:].astype(F32), x_ref[...],
              gt1_ref[...], sh2_ref[...], sc2_ref[...], wua_ref, wup_ref, wo_ref, g2_ref, x1_ref, h2_ref)


def _const_spec(shape):
    nd = len(shape)
    return pl.BlockSpec(shape, lambda *_: (0,) * nd)


def _mix_prompt(outs, lses, u, gates, x2d, mod_p, w_up_attn, w_pool, pool_scale, w_up_pool, w_out, norm_g2, seq,
                hosted):
    m = x2d.shape[0]
    tm = TM_MIX
    tpb = seq // tm
    hb = tm // POOL_HALO

    def row(width):
        return pl.BlockSpec((tm, width), lambda mi: (mi, 0))

    def mod(c):
        return pl.BlockSpec((None, 1, D_MODEL), lambda mi: (mi // tpb, 0, c))

    in_specs = ([row(D_GROUP)] * 3 + [row(HEAD_DIM)] * 3 + [
        row(D_POOL),
        pl.BlockSpec((POOL_HALO, D_POOL), lambda mi: (jnp.maximum(mi * hb - 1, 0), 0)),
        row(2 * D_MODEL), row(D_MODEL), mod(2), mod(3), mod(4),
        _const_spec((D_GROUP, D_MODEL)), _const_spec((len(POOL_WINDOWS), POOL_GROUP, POOL_GROUP)),
        _const_spec((1, D_POOL)), _const_spec((D_POOL, D_MODEL)), _const_spec((D_MODEL, D_MODEL)),
        _const_spec((1, D_MODEL))])
    s_args, s_in_specs, s_out_shape, s_out_specs, s_scratch, aliases = _stream_operands(hosted, len(in_specs), 2)
    kernel = _host_streams(functools.partial(_mix_p_kernel, tiles_per_batch=tpb), len(in_specs), 2, 1, hosted,
                           m // tm, lambda: pl.program_id(0))
    return pl.pallas_call(
        kernel,
        grid=(m // tm,),
        in_specs=in_specs + s_in_specs,
        out_specs=[row(D_MODEL), row(D_MODEL)] + s_out_specs,
        out_shape=[jax.ShapeDtypeStruct((m, D_MODEL), F32), jax.ShapeDtypeStruct((m, D_MODEL), BF16)] + s_out_shape,
        scratch_shapes=[pltpu.VMEM((POOL_HALO + tm, D_POOL), F32)] + s_scratch,
        input_output_aliases=aliases,
        compiler_params=_cparams("arbitrary"),
        name="mix_prompt",
    )(*outs, *lses, u, u, gates, x2d, mod_p, mod_p, mod_p, w_up_attn, w_pool, pool_scale, w_up_pool, w_out, norm_g2,
      *s_args)


def _mlp_kernel(h_ref, wu_ref, wd_ref, x1_ref, gt_ref, gf_ref, y_ref, acc_ref):
    f = pl.program_id(1)
    a = jnp.dot(h_ref[...], wu_ref[...], preferred_element_type=F32)
    a = jnp.square(jnp.maximum(a, 0.0)).astype(BF16)
    part = jnp.dot(a, wd_ref[...], preferred_element_type=F32)

    @pl.when(f == 0)
    def _():
        acc_ref[...] = part

    @pl.when(f > 0)
    def _():
        acc_ref[...] += part

    @pl.when(f == pl.num_programs(1) - 1)
    def _():
        x2 = x1_ref[...] + gt_ref[...] * acc_ref[...]
        y_ref[...] = _rms(x2) * gf_ref[...]


def _mlp_sample(h2, x1, mod_s, w_up, w_down, norm_gf):
    m = h2.shape[0]
    tf = TF_MLP_SAMPLE
    return pl.pallas_call(
        _mlp_kernel,
        grid=(1, D_FF // tf),
        in_specs=[pl.BlockSpec((m, D_MODEL), lambda mi, f: (0, 0)),
                  pl.BlockSpec((D_MODEL, tf), lambda mi, f: (0, f)),
                  pl.BlockSpec((tf, D_MODEL), lambda mi, f: (f, 0)),
                  pl.BlockSpec((m, D_MODEL), lambda mi, f: (0, 0)),
                  pl.BlockSpec((m, D_MODEL), lambda mi, f: (0, 5)),
                  pl.BlockSpec((1, D_MODEL), lambda mi, f: (0, 0))],
        out_specs=pl.BlockSpec((m, D_MODEL), lambda mi, f: (0, 0)),
        out_shape=jax.ShapeDtypeStruct((m, D_MODEL), F32),
        scratch_shapes=[pltpu.VMEM((m, D_MODEL), F32)],
        compiler_params=_cparams("arbitrary", "arbitrary"),
        name="mlp_sample",
    )(h2, w_up, w_down, x1, mod_s, norm_gf)


def _mlp_p_kernel(h_ref, wu_ref, wd_ref, gt_ref, gf_ref, x1_hbm, y_hbm, acc_ref, xy_ref, sems):
    mi, f = pl.program_id(0), pl.program_id(1)
    nm, nf = pl.num_programs(0), pl.num_programs(1)
    tm = acc_ref.shape[0]

    def x1_copy(i):
        return pltpu.make_async_copy(x1_hbm.at[pl.ds(i * tm, tm)], xy_ref, sems.at[0])

    def y_copy(i):
        return pltpu.make_async_copy(xy_ref, y_hbm.at[pl.ds(i * tm, tm)], sems.at[1])

    @pl.when(f == 1)
    def _():
        @pl.when(mi > 0)
        def _():
            y_copy(mi - 1).wait()

        x1_copy(mi).start()

    @pl.when(f == 0)
    def _():
        acc_ref[...] = jnp.zeros_like(acc_ref)

    a = jnp.dot(h_ref[...], wu_ref[...], preferred_element_type=F32)
    a = jnp.square(jnp.maximum(a, 0.0)).astype(BF16)
    acc_ref[...] += jnp.dot(a, wd_ref[...], preferred_element_type=F32)

    @pl.when(f == nf - 1)
    def _():
        x1_copy(mi).wait()
        x2 = xy_ref[...] + gt_ref[...] * acc_ref[...]
        xy_ref[...] = _rms(x2) * gf_ref[...]
        y_copy(mi).start()

        @pl.when(mi == nm - 1)
        def _():
            y_copy(mi).wait()


def _mlp_prompt(h2, x1, mod_p, w_up, w_down, norm_gf, seq, hosted):
    m = h2.shape[0]
    tm, tf = TM_MLP, TF_MLP
    tpb = seq // tm
    nm, nf = m // tm, D_FF // tf
    assert nf >= 3
    any_spec = pl.BlockSpec(memory_space=pl.ANY)
    s_args, s_in_specs, s_out_shape, s_out_specs, s_scratch, aliases = _stream_operands(hosted, 6, 1)
    kernel = _host_streams(_mlp_p_kernel, 6, 1, 3, hosted, nm * nf,
                           lambda: pl.program_id(0) * nf + pl.program_id(1))
    return pl.pallas_call(
        kernel,
        grid=(nm, nf),
        in_specs=[pl.BlockSpec((tm, D_MODEL), lambda mi, f: (mi, 0)),
                  pl.BlockSpec((D_MODEL, tf), lambda mi, f: (0, f)),
                  pl.BlockSpec((tf, D_MODEL), lambda mi, f: (f, 0)),
                  pl.BlockSpec((None, 1, D_MODEL), lambda mi, f: (mi // tpb, 0, 5)),
                  pl.BlockSpec((1, D_MODEL), lambda mi, f: (0, 0)),
                  any_spec] + s_in_specs,
        out_specs=[any_spec] + s_out_specs,
        out_shape=[jax.ShapeDtypeStruct((m, D_MODEL), F32)] + s_out_shape,
        scratch_shapes=[pltpu.VMEM((tm, D_MODEL), F32), pltpu.VMEM((tm, D_MODEL), F32),
                        pltpu.SemaphoreType.DMA((2,))] + s_scratch,
        input_output_aliases=aliases,
        compiler_params=_cparams("arbitrary", "arbitrary"),
        name="mlp_prompt",
    )(h2, w_up, w_down, mod_p, norm_gf, x1, *s_args)


def _inproj_s_kernel(x_ref, sh_ref, sc_ref, g_ref, w_ref, o_ref, h_scr):
    @pl.when(pl.program_id(0) == 0)
    def _():
        y = _rms(x_ref[...]) * g_ref[...]
        h_scr[...] = (y * (1.0 + sc_ref[...]) + sh_ref[...]).astype(BF16)

    o_ref[...] = jnp.dot(h_scr[...], w_ref[...], preferred_element_type=F32)


def _inproj_sample(x_s, mod_s, norm_g, w_in):
    m = x_s.shape[0]
    tn = D_ATTN
    return pl.pallas_call(
        _inproj_s_kernel,
        grid=(D_IN // tn,),
        in_specs=[pl.BlockSpec((m, D_MODEL), lambda j: (0, 0)),
                  pl.BlockSpec((m, D_MODEL), lambda j: (0, 0)),
                  pl.BlockSpec((m, D_MODEL), lambda j: (0, 1)),
                  pl.BlockSpec((1, D_MODEL), lambda j: (0, 0)),
                  pl.BlockSpec((D_MODEL, tn), lambda j: (0, j))],
        out_specs=pl.BlockSpec((m, tn), lambda j: (0, j)),
        out_shape=jax.ShapeDtypeStruct((m, D_IN), F32),
        scratch_shapes=[pltpu.VMEM((m, D_MODEL), BF16)],
        compiler_params=_cparams("arbitrary"),
        name="inproj_sample",
    )(x_s, mod_s, mod_s, norm_g, w_in)


def _attn_s_kernel(bias_ref, q_ref, kn_ref, vn_ref, k0_ref, v0_ref, k1_ref, v1_ref, k2_ref, v2_ref, o_ref):
    k_refs = (k0_ref, k1_ref, k2_ref)
    v_refs = (v0_ref, v1_ref, v2_ref)

    def body(s, carry):
        outs, lses = [], []
        for g in range(N_GROUPS):
            hs = slice(g * HEADS_PER_GROUP, (g + 1) * HEADS_PER_GROUP)
            q = q_ref[s, hs, :]
            kn = kn_ref[s, hs, :]
            vn = vn_ref[s, hs, :]
            kt = k_refs[g][s]
            vt = v_refs[g][s]
            sh = jnp.sum(kt * q[None], axis=-1, keepdims=True) * SCALE + bias_ref[g]
            sn = jnp.sum(kn * q, axis=-1, keepdims=True) * SCALE
            mx = jnp.maximum(jnp.max(sh, axis=0), sn)
            p = jnp.exp(sh - mx[None])
            pn = jnp.exp(sn - mx)
            den = jnp.sum(p, axis=0) + pn
            num = jnp.sum(p * vt, axis=0) + pn * vn
            outs.append(num / den)
            lses.append(mx + jnp.log(den))
        mx = jnp.maximum(jnp.maximum(lses[0], lses[1]), lses[2])
        e = [jnp.exp(l - mx) for l in lses]
        tot = e[0] + e[1] + e[2]
        o_ref[s] = (e[0] * outs[0] + e[1] * outs[1] + e[2] * outs[2]) / tot
        return carry

    lax.fori_loop(0, q_ref.shape[0], body, 0)


def _attn_sample(bias_s, q3, kn3, vn3, caches):
    nb = q3.shape[0]
    bs = BS_ATTN
    tok = pl.BlockSpec((bs, N_HEADS, HEAD_DIM), lambda i: (i, 0, 0))
    in_specs = [_const_spec((N_GROUPS, BAND, HEADS_PER_GROUP, HEAD_DIM)), tok, tok, tok]
    args = [bias_s, q3, kn3, vn3]
    for c, (_, dil) in zip(caches, ATTN_GROUPS):
        cv = c.reshape(nb, BAND, dil * 2, HEADS_PER_GROUP, HEAD_DIM)
        for kv in range(2):
            in_specs.append(pl.BlockSpec((bs, BAND, None, HEADS_PER_GROUP, HEAD_DIM),
                                         lambda i, kv=kv: (i, 0, kv, 0, 0)))
            args.append(cv)
    return pl.pallas_call(
        _attn_s_kernel,
        grid=(nb // bs,),
        in_specs=in_specs,
        out_specs=pl.BlockSpec((bs, HEADS_PER_GROUP, HEAD_DIM), lambda i: (i, 0, 0)),
        out_shape=jax.ShapeDtypeStruct((nb, HEADS_PER_GROUP, HEAD_DIM), F32),
        compiler_params=_cparams("arbitrary"),
        name="attn_sample",
    )(*args)


def _mix_s_kernel(o_ref, u_ref, hist_ref, ga_ref, gb_ref, x_ref, gt1_ref, sh2_ref, sc2_ref,
                  wua_ref, wp_ref, ps_ref, wup_ref, wo_ref, g2_ref, x1_ref, h2_ref, np_ref):
    u = u_ref[...]
    zs = []
    for g, w in enumerate(POOL_WINDOWS):
        sl = slice(g * POOL_GROUP, (g + 1) * POOL_GROUP)
        win = u[:, sl]
        for jj in range(1, w):
            win = win + hist_ref[:, POOL_HIST - jj, sl]
        zs.append(win / float(w) - u[:, sl])
    p = _pool_project(zs, wp_ref, ps_ref)
    np_ref[:, 0:POOL_HIST - 1, :] = hist_ref[:, 1:POOL_HIST, :]
    np_ref[:, POOL_HIST - 1, :] = u
    _mix_tail(o_ref[...], p, jax.nn.sigmoid(ga_ref[...]), jax.nn.sigmoid(gb_ref[...]), x_ref[...],
              gt1_ref[...], sh2_ref[...], sc2_ref[...], wua_ref, wup_ref, wo_ref, g2_ref, x1_ref, h2_ref)


def _mix_sample(o_attn, proj_s, hist, x_s, mod_s, w_up_attn, w_pool, pool_scale, w_up_pool, w_out, norm_g2):
    m = x_s.shape[0]

    def cols(width, c):
        return pl.BlockSpec((m, width), lambda i: (0, c))

    u0 = 3 * D_ATTN
    u = lax.slice_in_dim(proj_s, u0, u0 + D_POOL, axis=1)
    ga = lax.slice_in_dim(proj_s, u0 + D_POOL, u0 + D_POOL + D_MODEL, axis=1)
    gb = lax.slice_in_dim(proj_s, u0 + D_POOL + D_MODEL, D_IN, axis=1)
    in_specs = [cols(D_GROUP, 0), cols(D_POOL, 0), _const_spec((m, POOL_HIST, D_POOL)),
                cols(D_MODEL, 0), cols(D_MODEL, 0),
                cols(D_MODEL, 0), cols(D_MODEL, 2), cols(D_MODEL, 3), cols(D_MODEL, 4),
                _const_spec((D_GROUP, D_MODEL)), _const_spec((len(POOL_WINDOWS), POOL_GROUP, POOL_GROUP)),
                _const_spec((1, D_POOL)), _const_spec((D_POOL, D_MODEL)), _const_spec((D_MODEL, D_MODEL)),
                _const_spec((1, D_MODEL))]
    return pl.pallas_call(
        _mix_s_kernel,
        grid=(1,),
        in_specs=in_specs,
        out_specs=[_const_spec((m, D_MODEL)), _const_spec((m, D_MODEL)), _const_spec((m, POOL_HIST, D_POOL))],
        out_shape=[jax.ShapeDtypeStruct((m, D_MODEL), F32), jax.ShapeDtypeStruct((m, D_MODEL), BF16),
                   jax.ShapeDtypeStruct((m, POOL_HIST, D_POOL), F32)],
        compiler_params=_cparams("arbitrary"),
        name="mix_sample",
    )(o_attn, u, hist, ga, gb, x_s, mod_s, mod_s, mod_s, w_up_attn, w_pool, pool_scale, w_up_pool, w_out, norm_g2)


class _ShiftStream:
    def __init__(self, cache, new, out, buf, sems, rows, ns, lo, hi):
        self.cache, self.new, self.out, self.buf, self.sems = cache, new, out, buf, sems
        self.rows, self.ns = rows, ns
        self.slots = buf.shape[0]
        self.cps = cache.shape[2] // rows
        self.lo, self.hi = lo, hi

    @staticmethod
    def total_chunks(cache_shape, rows, ns):
        nb, w = cache_shape[1], cache_shape[2]
        assert w % rows == 0 and nb % ns == 0 and (ns == 1 or rows == w)
        return (nb // ns) * (w // rows)

    def _where(self, k):
        return (k // self.cps) * self.ns, (k % self.cps) * self.rows, k % self.slots

    def _body_in(self, k):
        b0, w0, slot = self._where(k)
        return pltpu.make_async_copy(self.cache.at[0, pl.ds(b0, self.ns), pl.ds(w0 + 1, self.rows - 1)],
                                     self.buf.at[slot, :, pl.ds(0, self.rows - 1)], self.sems.at[0, slot])

    def _next_in(self, k):
        b0, w0, slot = self._where(k)
        return pltpu.make_async_copy(self.cache.at[0, pl.ds(b0, self.ns), pl.ds(w0 + self.rows, 1)],
                                     self.buf.at[slot, :, pl.ds(self.rows - 1, 1)], self.sems.at[1, slot])

    def _new_in(self, k):
        b0, _, slot = self._where(k)
        return pltpu.make_async_copy(self.new.at[pl.ds(b0, self.ns)],
                                     self.buf.at[slot, :, self.rows - 1], self.sems.at[1, slot])

    def _out(self, k):
        b0, w0, slot = self._where(k)
        return pltpu.make_async_copy(self.buf.at[slot],
                                     self.out.at[0, pl.ds(b0, self.ns), pl.ds(w0, self.rows)], self.sems.at[2, slot])

    def _last_row(self, k, act):
        if isinstance(k, int) or self.cps == 1:
            at_end = self.cps == 1 or k % self.cps == self.cps - 1
            getattr(self._new_in(k) if at_end else self._next_in(k), act)()
            return
        at_end = k % self.cps == self.cps - 1

        @pl.when(at_end)
        def _():
            getattr(self._new_in(k), act)()

        @pl.when(jnp.logical_not(at_end))
        def _():
            getattr(self._next_in(k), act)()

    def start_in(self, k):
        self._body_in(k).start()
        self._last_row(k, "start")

    def wait_in(self, k):
        self._body_in(k).wait()
        self._last_row(k, "wait")

    def start_out(self, k):
        self._out(k).start()

    def wait_out(self, k):
        self._out(k).wait()

    def step(self, k, lag):
        self.wait_in(k)
        self.start_out(k)

        @pl.when(k - lag >= self.lo)
        def _():
            self.wait_out(k - lag)

        @pl.when(k - lag + self.slots < self.hi)
        def _():
            self.start_in(k - lag + self.slots)

    def tick(self, s, n_steps):
        n = self.hi - self.lo
        per = -(-n // n_steps)
        stride = max(n_steps // n, 1)
        assert self.slots >= 2 * per and n >= self.slots

        @pl.when(s == 0)
        def _():
            for k in range(self.lo, self.lo + self.slots - per):
                self.start_in(k)

        for i in range(per):
            k = self.lo + (s // stride) * per + i

            @pl.when((s % stride == 0) & (k < self.hi))
            def _(k=k):
                self.step(k, per)

        @pl.when(s == n_steps - 1)
        def _():
            for k in range(self.hi - per, self.hi):
                self.wait_out(k)


class _Hosted(NamedTuple):
    cache: jax.Array
    new: jax.Array
    prev: object
    rows: int
    ns: int
    slots: int
    frac_lo: float
    frac_hi: float

    def chunk_range(self):
        total = _ShiftStream.total_chunks(self.cache.shape, self.rows, self.ns)
        return int(round(self.frac_lo * total)), int(round(self.frac_hi * total))


def _host_streams(body, n_in, n_out, n_scr, hosted, n_steps, step_index):
    n_sin = sum(2 if h.prev is None else 3 for h in hosted)
    plans = [(h.rows, h.ns, *h.chunk_range(), h.prev is not None) for h in hosted]

    def kernel(*refs):
        refs = list(refs)
        ins, s_in = refs[:n_in], refs[n_in:n_in + n_sin]
        o0 = n_in + n_sin
        outs, s_out = refs[o0:o0 + n_out], refs[o0 + n_out:o0 + n_out + len(plans)]
        c0 = o0 + n_out + len(plans)
        scr, s_scr = refs[c0:c0 + n_scr], refs[c0 + n_scr:]
        s = step_index()
        pos = 0
        for i, (rows, ns, lo, hi, has_prev) in enumerate(plans):
            _ShiftStream(s_in[pos], s_in[pos + 1], s_out[i], s_scr[2 * i], s_scr[2 * i + 1],
                         rows, ns, lo, hi).tick(s, n_steps)
            pos += 3 if has_prev else 2
        body(*ins, *outs, *scr)

    return kernel


def _stream_operands(hosted, n_in, n_out):
    any_spec = pl.BlockSpec(memory_space=pl.ANY)
    args, aliases = [], {}
    for i, h in enumerate(hosted):
        args += [h.cache, h.new]
        if h.prev is not None:
            aliases[n_in + len(args)] = n_out + i
            args.append(h.prev)
    out_shape = [jax.ShapeDtypeStruct(h.cache.shape, h.cache.dtype) for h in hosted]
    scratch = [s for h in hosted for s in (
        pltpu.VMEM((h.slots, h.ns, h.rows, 2, HEADS_PER_GROUP, HEAD_DIM), F32),
        pltpu.SemaphoreType.DMA((3, h.slots)))]
    return args, [any_spec] * len(args), out_shape, [any_spec] * len(hosted), scratch, aliases


def _alibi_slopes():
    h = jnp.arange(1, N_HEADS + 1, dtype=F32)
    return jnp.exp2(-ALIBI_MAX_BIAS * h / N_HEADS)


def _prompt_bias(g):
    dil = ATTN_GROUPS[g][1]
    slopes = _alibi_slopes()[g * HEADS_PER_GROUP:(g + 1) * HEADS_PER_GROUP]
    a = jnp.arange(BAND)[:, None]
    b = jnp.arange(2 * BAND)[None, :]
    dist = a - b + BAND
    valid = (dist >= 0) & (dist <= BAND)
    bias = -slopes[:, None, None] * (dist * dil).astype(F32)
    return jnp.where(valid[None], bias, NEG)


def _sample_bias():
    slopes = _alibi_slopes().reshape(N_GROUPS, 1, HEADS_PER_GROUP, 1)
    dil = jnp.array([d for _, d in ATTN_GROUPS], F32).reshape(N_GROUPS, 1, 1, 1)
    back = (BAND - jnp.arange(BAND, dtype=F32)).reshape(1, BAND, 1, 1)
    return jnp.broadcast_to(-slopes * (back * dil), (N_GROUPS, BAND, HEADS_PER_GROUP, HEAD_DIM))


def kernel(x_prompt, x_sample, c_prompt, c_sample, cache_kv_w128, cache_kv_w512, cache_kv_w2048, state_pool,
           norm_mix_g, w_ada, b_ada, w_in, w_up_attn, w_pool, pool_scale, w_up_pool, w_out, norm_mlp_g,
           w_mlp_up, w_mlp_down, norm_final_g):
    batch, seq, _ = x_prompt.shape
    nb = x_sample.shape[0]
    depth = w_in.shape[0]
    assert depth == 1 and x_sample.shape[1] == 1
    caches = (cache_kv_w128, cache_kv_w512, cache_kv_w2048)
    for c, (w, dil) in zip(caches, ATTN_GROUPS):
        assert c.shape[2] == w == BAND * dil

    w_in_b = w_in[0].astype(BF16)
    w_ua_b = w_up_attn[0].astype(BF16)
    w_pool_b = w_pool[0].astype(BF16)
    w_up_b = w_up_pool[0].astype(BF16)
    w_out_b = w_out[0].astype(BF16)
    w_mu_b = w_mlp_up[0].astype(BF16)
    w_md_b = w_mlp_down[0].astype(BF16)
    g1 = norm_mix_g[0].reshape(1, D_MODEL)
    g2 = norm_mlp_g[0].reshape(1, D_MODEL)
    gf = norm_final_g.reshape(1, D_MODEL)
    ps = pool_scale[0].reshape(1, D_POOL)

    mod = _ada(jnp.concatenate([c_sample, c_prompt], axis=0), w_ada[0], b_ada[0])
    mod_s = mod[:nb]
    mod_p = mod[nb:].reshape(batch, 1, 6 * D_MODEL)

    xs = x_sample.reshape(nb, D_MODEL)
    proj_s = _inproj_sample(xs, mod_s, g1, w_in_b)
    q3 = proj_s[:, 0:D_ATTN].reshape(nb, N_HEADS, HEAD_DIM)
    kn3 = proj_s[:, D_ATTN:2 * D_ATTN].reshape(nb, N_HEADS, HEAD_DIM)
    vn3 = proj_s[:, 2 * D_ATTN:3 * D_ATTN].reshape(nb, N_HEADS, HEAD_DIM)
    o_s = _attn_sample(_sample_bias(), q3, kn3, vn3, [c[0] for c in caches])
    new_rows = [jnp.stack([kn3[:, g * HEADS_PER_GROUP:(g + 1) * HEADS_PER_GROUP],
                           vn3[:, g * HEADS_PER_GROUP:(g + 1) * HEADS_PER_GROUP]], axis=1)
                for g in range(N_GROUPS)]
    x1_s, h2_s, pool_s = _mix_sample(o_s.reshape(nb, D_GROUP), proj_s, state_pool[0], xs, mod_s,
                                     w_ua_b, w_pool_b, ps, w_up_b, w_out_b, g2)
    y_s = _mlp_sample(h2_s, x1_s, mod_s, w_mu_b, w_md_b, gf)

    x2d = x_prompt.reshape(batch * seq, D_MODEL)
    *qkv_sub, u_p, gates, kv2, kv1, kv0, kv_s1 = _inproj_prompt(
        x2d, mod_p, g1, w_in_b, batch, seq,
        (_Hosted(caches[1], new_rows[1], None, rows=256, ns=1, slots=4, frac_lo=0.0, frac_hi=1.0),))
    outs, lses = [], []
    for g in range(N_GROUPS):
        o, l = _attn_prompt_group(qkv_sub[g], _prompt_bias(g), g, batch, seq)
        outs.append(o)
        lses.append(l)
    x1_p, h2_p, kv_s0 = _mix_prompt(
        outs, lses, u_p, gates, x2d, mod_p, w_ua_b, w_pool_b, ps, w_up_b, w_out_b, g2, seq,
        (_Hosted(caches[0], new_rows[0], None, rows=128, ns=4, slots=3, frac_lo=0.0, frac_hi=1.0),))
    y_p, kv_s2 = _mlp_prompt(
        h2_p, x1_p, mod_p, w_mu_b, w_md_b, gf, seq,
        (_Hosted(caches[2], new_rows[2], None, rows=512, ns=1, slots=5, frac_lo=0.0, frac_hi=1.0),))

    pool_p = u_p.reshape(batch, seq, D_POOL)[:, seq - POOL_HIST:][None]
    return (y_p.reshape(batch, seq, D_MODEL), y_s.reshape(nb, 1, D_MODEL), kv0, kv1, kv2, pool_p,
            kv_s0, kv_s1, kv_s2, pool_s[None])
```

```python
import functools
from typing import NamedTuple

import jax
import jax.numpy as jnp
from jax import lax
from jax.experimental import pallas as pl
from jax.experimental.pallas import tpu as pltpu

F32 = jnp.float32
BF16 = jnp.bfloat16

D_MODEL = 2048
HEAD_DIM = 128
HEADS_PER_GROUP = 4
ATTN_GROUPS = ((128, 1), (512, 4), (2048, 16))
N_GROUPS = len(ATTN_GROUPS)
N_HEADS = HEADS_PER_GROUP * N_GROUPS
D_ATTN = N_HEADS * HEAD_DIM
D_GROUP = HEADS_PER_GROUP * HEAD_DIM
BAND = 128
POOL_WINDOWS = (2, 4, 8, 16)
POOL_GROUP = 128
D_POOL = POOL_GROUP * len(POOL_WINDOWS)
POOL_HIST = max(POOL_WINDOWS) - 1
POOL_HALO = 16
D_FF = 4 * D_MODEL
D_IN = 3 * D_ATTN + D_POOL + 2 * D_MODEL
ALIBI_MAX_BIAS = 8.0
EPS = 1e-6
SCALE = HEAD_DIM ** -0.5
NEG = -1e30

VMEM_LIMIT_BYTES = 60 * 1024 * 1024

IN_TILE = D_GROUP
N_IN_TILES = D_IN // IN_TILE
QKV_TILES = 3 * N_GROUPS
POOL_TILE = QKV_TILES
GATE_TILE0 = POOL_TILE + 1

TM_IN = 1024
IN_SPLIT = 2
TM_MIX = 256
MIX_SPLIT = 2
TM_MLP = 1024
TF_MLP = 512
TF_MLP_SAMPLE = 1024
BS_ATTN = 8
ATTN_UNROLL = 3


def _cparams(*sem):
    return pltpu.CompilerParams(dimension_semantics=sem, vmem_limit_bytes=VMEM_LIMIT_BYTES)


def _rms(x):
    return x * lax.rsqrt(jnp.mean(x * x, axis=-1, keepdims=True) + EPS)


def _ada_kernel(c_ref, w_ref, b_ref, o_ref):
    c = c_ref[...]
    a = (c * jax.nn.sigmoid(c)).astype(BF16)
    o_ref[...] = jnp.dot(a, w_ref[...].astype(BF16), preferred_element_type=F32) + b_ref[...]


def _ada(c_all, w_ada, b_ada):
    m = c_all.shape[0]
    tn = 1024
    return pl.pallas_call(
        _ada_kernel,
        grid=(6 * D_MODEL // tn,),
        in_specs=[pl.BlockSpec((m, D_MODEL), lambda j: (0, 0)),
                  pl.BlockSpec((D_MODEL, tn), lambda j: (0, j)),
                  pl.BlockSpec((1, tn), lambda j: (0, j))],
        out_specs=pl.BlockSpec((m, tn), lambda j: (0, j)),
        out_shape=jax.ShapeDtypeStruct((m, 6 * D_MODEL), F32),
        compiler_params=_cparams("arbitrary"),
        name="ada",
    )(c_all, w_ada, b_ada.reshape(1, 6 * D_MODEL))


KV_ROWS = 2 * HEADS_PER_GROUP


def _store_heads(ref, row0, kv, val):
    for h in range(HEADS_PER_GROUP):
        start = row0 * KV_ROWS + kv * HEADS_PER_GROUP + h
        ref[pl.ds(start, val.shape[0], stride=KV_ROWS), :] = val[:, h * HEAD_DIM:(h + 1) * HEAD_DIM]


def _inproj_p_kernel(sh_ref, sc_ref, g_ref, w_ref, x_hbm,
                     qkv0_ref, qkv1_ref, qkv2_ref, u_ref, gate_ref, kv2_ref, kv1_ref, kv0_ref,
                     h_scr, acc_scr, x_scr, x_sem, *, tiles_per_batch):
    mi, j = pl.program_id(0), pl.program_id(1)
    tm = x_scr.shape[0]

    def x_copy(i):
        return pltpu.make_async_copy(x_hbm.at[pl.ds(i * tm, tm)], x_scr, x_sem.at[0])

    @pl.when(j == 0)
    def _():
        @pl.when(mi == 0)
        def _():
            x_copy(mi).start()

        x_copy(mi).wait()
        y = _rms(x_scr[...]) * g_ref[...]
        h_scr[...] = (y * (1.0 + sc_ref[...]) + sh_ref[...]).astype(BF16)

    @pl.when((j == 1) & (mi + 1 < pl.num_programs(0)))
    def _():
        x_copy(mi + 1).start()

    th = tm // IN_SPLIT

    def halves():
        for s in range(IN_SPLIT):
            yield s, jnp.dot(h_scr[s * th:(s + 1) * th, :], w_ref[...], preferred_element_type=F32)

    def store_sub(qkv_ref, dil, s, acc):
        if dil == 1:
            qkv_ref[0, s * th:(s + 1) * th, :] = acc.astype(BF16)
            return
        n = th // dil
        for h in range(HEADS_PER_GROUP):
            sl = slice(h * HEAD_DIM, (h + 1) * HEAD_DIM)
            acc_scr[s, h] = acc[:, sl]
            for r in range(dil):
                qkv_ref[r, s * n:(s + 1) * n, sl] = acc_scr[s, h, pl.ds(r, n, stride=dil), :].astype(BF16)

    last = pl.program_id(0) % tiles_per_batch == tiles_per_batch - 1
    kv_refs = (kv0_ref, kv1_ref, kv2_ref)
    for g, (qkv_ref, (_, dil)) in enumerate(zip((qkv0_ref, qkv1_ref, qkv2_ref), ATTN_GROUPS)):
        @pl.when(j == g)
        def _(qkv_ref=qkv_ref, dil=dil):
            for s, acc in halves():
                store_sub(qkv_ref, dil, s, acc)

        @pl.when((j == N_GROUPS + g) | (j == 2 * N_GROUPS + g))
        def _(g=g, qkv_ref=qkv_ref, dil=dil):
            kv_ref = kv_refs[g]
            first_kept = tm - kv_ref.shape[0] // KV_ROWS
            is_v = (j >= 2 * N_GROUPS).astype(jnp.int32)
            for s, acc in halves():
                store_sub(qkv_ref, dil, s, acc)
                row0 = max(s * th, first_kept)
                if first_kept == 0:
                    _store_heads(kv_ref, s * th, is_v, acc)
                elif row0 < (s + 1) * th:
                    @pl.when(last)
                    def _(acc=acc, row0=row0, s=s):
                        _store_heads(kv_ref, row0 - first_kept, is_v, acc[row0 - s * th:, :])

    @pl.when(j == POOL_TILE)
    def _():
        for s, acc in halves():
            u_ref[s * th:(s + 1) * th, :] = acc

    @pl.when(j >= GATE_TILE0)
    def _():
        for s, acc in halves():
            gate_ref[s * th:(s + 1) * th, :] = (0.5 * jnp.tanh(0.5 * acc) + 0.5).astype(BF16)


def _inproj_prompt(x2d, mod_p, norm_g, w_in, batch, seq, hosted):
    m = x2d.shape[0]
    tm = TM_IN
    tpb = seq // tm
    keep = [min(w, seq) for w, _ in ATTN_GROUPS]
    assert keep[2] == seq and tm >= keep[1] and seq % tm == 0

    def sub_spec(g):
        dil = ATTN_GROUPS[g][1]
        return pl.BlockSpec((None, None, dil, tm // dil, D_GROUP),
                            lambda mi, j: (jnp.clip((j - g) // N_GROUPS, 0, 2), mi // tpb, 0, mi % tpb, 0))

    def sub_shape(g):
        dil = ATTN_GROUPS[g][1]
        return jax.ShapeDtypeStruct((3, batch, dil, seq // dil, D_GROUP), BF16)

    def kv_spec(rows, tail_only):
        return pl.BlockSpec((None, rows * KV_ROWS, HEAD_DIM),
                            lambda mi, j: (mi // tpb, 0 if tail_only else mi % tpb, 0))

    def kv_shape(rows):
        return jax.ShapeDtypeStruct((batch, rows * KV_ROWS, HEAD_DIM), F32)

    in_specs = [
        pl.BlockSpec((None, 1, D_MODEL), lambda mi, j: (mi // tpb, 0, 0)),
        pl.BlockSpec((None, 1, D_MODEL), lambda mi, j: (mi // tpb, 0, 1)),
        pl.BlockSpec((1, D_MODEL), lambda mi, j: (0, 0)),
        pl.BlockSpec((D_MODEL, IN_TILE), lambda mi, j: (0, j)),
        pl.BlockSpec(memory_space=pl.ANY),
    ]
    out_specs = [
        sub_spec(0), sub_spec(1), sub_spec(2),
        pl.BlockSpec((tm, IN_TILE), lambda mi, j: (mi, 0)),
        pl.BlockSpec((tm, IN_TILE), lambda mi, j: (mi, jnp.clip(j - GATE_TILE0, 0, N_IN_TILES - GATE_TILE0 - 1))),
        kv_spec(tm, False), kv_spec(keep[1], True), kv_spec(keep[0], True),
    ]
    out_shape = [
        sub_shape(0), sub_shape(1), sub_shape(2),
        jax.ShapeDtypeStruct((m, D_POOL), F32),
        jax.ShapeDtypeStruct((m, 2 * D_MODEL), BF16),
        kv_shape(keep[2]), kv_shape(keep[1]), kv_shape(keep[0]),
    ]
    s_args, s_in_specs, s_out_shape, s_out_specs, s_scratch = _stream_operands(hosted)
    kernel = _host_streams(functools.partial(_inproj_p_kernel, tiles_per_batch=tpb), len(in_specs), len(out_specs),
                           4, hosted, (m // tm) * N_IN_TILES,
                           lambda: pl.program_id(0) * N_IN_TILES + pl.program_id(1))
    return pl.pallas_call(
        kernel,
        grid=(m // tm, N_IN_TILES),
        in_specs=in_specs + s_in_specs, out_specs=out_specs + s_out_specs, out_shape=out_shape + s_out_shape,
        scratch_shapes=[pltpu.VMEM((tm, D_MODEL), BF16),
                        pltpu.VMEM((IN_SPLIT, HEADS_PER_GROUP, tm // IN_SPLIT, HEAD_DIM), F32),
                        pltpu.VMEM((tm, D_MODEL), F32), pltpu.SemaphoreType.DMA((1,))] + s_scratch,
        compiler_params=_cparams("arbitrary", "arbitrary"),
        name="inproj_prompt",
    )(mod_p, mod_p, norm_g, w_in, x2d, *s_args)


def _attn_block(q, k, v, bias_fn):
    lane = lax.broadcasted_iota(jnp.int32, (BAND, HEAD_DIM), 1)
    heads = [slice(h * HEAD_DIM, (h + 1) * HEAD_DIM) for h in range(HEADS_PER_GROUP)]
    scores = [lax.dot_general(q[:, sl], k[:, sl], (((1,), (1,)), ((), ())), preferred_element_type=F32)
              for sl in heads]
    scores = [s * SCALE + bias_fn(h) for h, s in enumerate(scores)]
    maxes = [jnp.max(s, axis=-1, keepdims=True) for s in scores]
    probs = [jnp.exp(s - mx) for s, mx in zip(scores, maxes)]
    dens = [jnp.sum(p, axis=-1, keepdims=True) for p in probs]
    outs = [jnp.dot(p.astype(BF16), v[:, sl], preferred_element_type=F32) / den
            for p, sl, den in zip(probs, heads, dens)]
    lse = jnp.zeros((BAND, HEAD_DIM), F32)
    for h, (mx, den) in enumerate(zip(maxes, dens)):
        lse = jnp.where(lane == h, mx + jnp.log(den), lse)
    return outs, lse


def _attn_p_kernel(bias_ref, q_ref, k_ref, v_ref, o_ref, l_ref, o_scr):
    dil, n, _ = q_ref.shape
    nb = n // BAND

    def rows(start):
        return pl.ds(start, BAND) if dil == 1 else pl.ds(start, BAND, stride=dil)

    def put(start, outs, lse):
        for h in range(HEADS_PER_GROUP):
            o_scr[h, rows(start), :] = outs[h]
        l_ref[rows(start), :] = lse

    for r in range(dil):
        put(r, *_attn_block(q_ref[r, 0:BAND, :], k_ref[r, 0:BAND, :], v_ref[r, 0:BAND, :],
                            lambda h: bias_ref[h, :, BAND:2 * BAND]))

        if nb > 1:
            def body(i, carry, r=r):
                r0 = pl.multiple_of(i * BAND, BAND)
                rk = pl.multiple_of((i - 1) * BAND, BAND)
                put(r0 * dil + r, *_attn_block(q_ref[r, pl.ds(r0, BAND), :], k_ref[r, pl.ds(rk, 2 * BAND), :],
                                               v_ref[r, pl.ds(rk, 2 * BAND), :], lambda h: bias_ref[h]))
                return carry

            lax.fori_loop(1, nb, body, 0, unroll=ATTN_UNROLL)

    for h in range(HEADS_PER_GROUP):
        o_ref[:, h * HEAD_DIM:(h + 1) * HEAD_DIM] = o_scr[h].astype(BF16)


def _attn_prompt_group(qkv_sub, bias, g, batch, seq):
    dil = ATTN_GROUPS[g][1]
    n = seq // dil

    def sub(which):
        return pl.BlockSpec((None, None, dil, n, D_GROUP), lambda b: (which, b, 0, 0, 0))

    return pl.pallas_call(
        _attn_p_kernel,
        grid=(batch,),
        in_specs=[pl.BlockSpec((HEADS_PER_GROUP, BAND, 2 * BAND), lambda b: (0, 0, 0)),
                  sub(0), sub(1), sub(2)],
        out_specs=[pl.BlockSpec((seq, D_GROUP), lambda b: (b, 0)),
                   pl.BlockSpec((seq, HEAD_DIM), lambda b: (b, 0))],
        out_shape=[jax.ShapeDtypeStruct((batch * seq, D_GROUP), BF16),
                   jax.ShapeDtypeStruct((batch * seq, HEAD_DIM), F32)],
        scratch_shapes=[pltpu.VMEM((HEADS_PER_GROUP, seq, HEAD_DIM), F32)],
        compiler_params=_cparams("arbitrary"),
        name=f"attn_prompt_g{g}",
    )(bias, qkv_sub, qkv_sub, qkv_sub)


def _merge_heads(outs, lses):
    cols = []
    for h in range(HEADS_PER_GROUP):
        sl = slice(h * HEAD_DIM, (h + 1) * HEAD_DIM)
        l = [lg[:, h:h + 1] for lg in lses]
        mx = jnp.maximum(jnp.maximum(l[0], l[1]), l[2])
        e = [jnp.exp(x - mx) for x in l]
        tot = e[0] + e[1] + e[2]
        cols.append(sum((e[g] / tot) * outs[g][:, sl].astype(F32) for g in range(N_GROUPS)))
    return jnp.concatenate(cols, axis=-1)


def _mix_tail(parts, o_attn, p, load_gates, load_x, load_mod, wua_ref, wup_ref, wo_ref, g2_ref, x1_ref, h2_ref):
    a = [jnp.dot(o.astype(BF16), wua_ref[...], preferred_element_type=F32) for o in o_attn]
    b = [jnp.dot(q.astype(BF16), wup_ref[...], preferred_element_type=F32) for q in p]
    mix = []
    for rs, ai, bi in zip(parts, a, b):
        ga, gb = load_gates(rs)
        mix.append((ga * ai + gb * bi).astype(BF16))
    y = [jnp.dot(m, wo_ref[...], preferred_element_type=F32) for m in mix]
    for rs, yi in zip(parts, y):
        gt1, sh2, sc2 = load_mod(rs)
        x1 = load_x(rs) + gt1 * yi
        x1_ref[rs, :] = x1
        h2_ref[rs, :] = (_rms(x1) * g2_ref[...] * (1.0 + sc2) + sh2).astype(BF16)


def _pool_project(z_groups, wp_ref, ps_ref):
    cols = [jnp.dot(z.astype(BF16), wp_ref[g], preferred_element_type=F32) for g, z in enumerate(z_groups)]
    return jnp.concatenate(cols, axis=-1) * ps_ref[...]


def _mix_p_kernel(o0_ref, o1_ref, o2_ref, l0_ref, l1_ref, l2_ref, u_ref, uh_ref, gate_ref, x_ref,
                  gt1_ref, sh2_ref, sc2_ref, wua_ref, wp_ref, ps_ref, wup_ref, wo_ref, g2_ref,
                  x1_ref, h2_ref, ext_scr, *, tiles_per_batch):
    tm = u_ref.shape[0]
    tp = tm // MIX_SPLIT
    t = pl.program_id(0) % tiles_per_batch
    parts = [slice(i * tp, (i + 1) * tp) for i in range(MIX_SPLIT)]
    o_attn = [_merge_heads([o0_ref[rs, :], o1_ref[rs, :], o2_ref[rs, :]], [l0_ref[rs, :], l1_ref[rs, :], l2_ref[rs, :]])
              for rs in parts]

    ext_scr[0:POOL_HALO, :] = jnp.where(t == 0, 0.0, uh_ref[...])
    ext_scr[POOL_HALO:, :] = u_ref[...]
    p = []
    for i, rs in enumerate(parts):
        pos = t * tm + i * tp + lax.broadcasted_iota(jnp.int32, (tp, 1), 0)
        zs = []
        for g, w in enumerate(POOL_WINDOWS):
            sl = slice(g * POOL_GROUP, (g + 1) * POOL_GROUP)
            u = u_ref[rs, sl]
            win = u
            for jj in range(1, w):
                r0 = POOL_HALO + i * tp - jj
                win = win + ext_scr[r0:r0 + tp, sl]
            cnt = jnp.minimum(pos + 1, w).astype(F32)
            zs.append(win / cnt - u)
        p.append(_pool_project(zs, wp_ref, ps_ref))

    _mix_tail(parts, o_attn, p,
              lambda rs: (gate_ref[rs, :D_MODEL].astype(F32), gate_ref[rs, D_MODEL:].astype(F32)),
              lambda rs: x_ref[rs, :],
              lambda rs: (gt1_ref[...], sh2_ref[...], sc2_ref[...]),
              wua_ref, wup_ref, wo_ref, g2_ref, x1_ref, h2_ref)


def _const_spec(shape):
    nd = len(shape)
    return pl.BlockSpec(shape, lambda *_: (0,) * nd)


def _mix_prompt(outs, lses, u, gates, x2d, mod_p, w_up_attn, w_pool, pool_scale, w_up_pool, w_out, norm_g2, seq,
                hosted):
    m = x2d.shape[0]
    tm = TM_MIX
    tpb = seq // tm
    hb = tm // POOL_HALO

    def row(width):
        return pl.BlockSpec((tm, width), lambda mi: (mi, 0))

    def mod(c):
        return pl.BlockSpec((None, 1, D_MODEL), lambda mi: (mi // tpb, 0, c))

    in_specs = ([row(D_GROUP)] * 3 + [row(HEAD_DIM)] * 3 + [
        row(D_POOL),
        pl.BlockSpec((POOL_HALO, D_POOL), lambda mi: (jnp.maximum(mi * hb - 1, 0), 0)),
        row(2 * D_MODEL), row(D_MODEL), mod(2), mod(3), mod(4),
        _const_spec((D_GROUP, D_MODEL)), _const_spec((len(POOL_WINDOWS), POOL_GROUP, POOL_GROUP)),
        _const_spec((1, D_POOL)), _const_spec((D_POOL, D_MODEL)), _const_spec((D_MODEL, D_MODEL)),
        _const_spec((1, D_MODEL))])
    s_args, s_in_specs, s_out_shape, s_out_specs, s_scratch = _stream_operands(hosted)
    kernel = _host_streams(functools.partial(_mix_p_kernel, tiles_per_batch=tpb), len(in_specs), 2, 1, hosted,
                           m // tm, lambda: pl.program_id(0))
    return pl.pallas_call(
        kernel,
        grid=(m // tm,),
        in_specs=in_specs + s_in_specs,
        out_specs=[row(D_MODEL), row(D_MODEL)] + s_out_specs,
        out_shape=[jax.ShapeDtypeStruct((m, D_MODEL), F32), jax.ShapeDtypeStruct((m, D_MODEL), BF16)] + s_out_shape,
        scratch_shapes=[pltpu.VMEM((POOL_HALO + tm, D_POOL), F32)] + s_scratch,
        compiler_params=_cparams("arbitrary"),
        name="mix_prompt",
    )(*outs, *lses, u, u, gates, x2d, mod_p, mod_p, mod_p, w_up_attn, w_pool, pool_scale, w_up_pool, w_out, norm_g2,
      *s_args)


def _mlp_kernel(h_ref, wu_ref, wd_ref, x1_ref, gt_ref, gf_ref, y_ref, acc_ref):
    f = pl.program_id(1)
    a = jnp.dot(h_ref[...], wu_ref[...], preferred_element_type=F32)
    a = jnp.square(jnp.maximum(a, 0.0)).astype(BF16)
    part = jnp.dot(a, wd_ref[...], preferred_element_type=F32)

    @pl.when(f == 0)
    def _():
        acc_ref[...] = part

    @pl.when(f > 0)
    def _():
        acc_ref[...] += part

    @pl.when(f == pl.num_programs(1) - 1)
    def _():
        x2 = x1_ref[...] + gt_ref[...] * acc_ref[...]
        y_ref[...] = _rms(x2) * gf_ref[...]


def _mlp_sample(h2, x1, mod_s, w_up, w_down, norm_gf):
    m = h2.shape[0]
    tf = TF_MLP_SAMPLE
    return pl.pallas_call(
        _mlp_kernel,
        grid=(1, D_FF // tf),
        in_specs=[pl.BlockSpec((m, D_MODEL), lambda mi, f: (0, 0)),
                  pl.BlockSpec((D_MODEL, tf), lambda mi, f: (0, f)),
                  pl.BlockSpec((tf, D_MODEL), lambda mi, f: (f, 0)),
                  pl.BlockSpec((m, D_MODEL), lambda mi, f: (0, 0)),
                  pl.BlockSpec((m, D_MODEL), lambda mi, f: (0, 5)),
                  pl.BlockSpec((1, D_MODEL), lambda mi, f: (0, 0))],
        out_specs=pl.BlockSpec((m, D_MODEL), lambda mi, f: (0, 0)),
        out_shape=jax.ShapeDtypeStruct((m, D_MODEL), F32),
        scratch_shapes=[pltpu.VMEM((m, D_MODEL), F32)],
        compiler_params=_cparams("arbitrary", "arbitrary"),
        name="mlp_sample",
    )(h2, w_up, w_down, x1, mod_s, norm_gf)


def _mlp_p_kernel(h_ref, wu_ref, wd_ref, gt_ref, gf_ref, x1_hbm, y_hbm, acc_ref, xy_ref, sems):
    mi, f = pl.program_id(0), pl.program_id(1)
    nm, nf = pl.num_programs(0), pl.num_programs(1)
    tm = acc_ref.shape[0]

    def x1_copy(i):
        return pltpu.make_async_copy(x1_hbm.at[pl.ds(i * tm, tm)], xy_ref, sems.at[0])

    def y_copy(i):
        return pltpu.make_async_copy(xy_ref, y_hbm.at[pl.ds(i * tm, tm)], sems.at[1])

    @pl.when(f == 1)
    def _():
        @pl.when(mi > 0)
        def _():
            y_copy(mi - 1).wait()

        x1_copy(mi).start()

    @pl.when(f == 0)
    def _():
        acc_ref[...] = jnp.zeros_like(acc_ref)

    a = jnp.dot(h_ref[...], wu_ref[...], preferred_element_type=F32)
    a = jnp.square(jnp.maximum(a, 0.0)).astype(BF16)
    acc_ref[...] += jnp.dot(a, wd_ref[...], preferred_element_type=F32)

    @pl.when(f == nf - 1)
    def _():
        x1_copy(mi).wait()
        x2 = xy_ref[...] + gt_ref[...] * acc_ref[...]
        xy_ref[...] = _rms(x2) * gf_ref[...]
        y_copy(mi).start()

        @pl.when(mi == nm - 1)
        def _():
            y_copy(mi).wait()


def _mlp_prompt(h2, x1, mod_p, w_up, w_down, norm_gf, seq, hosted):
    m = h2.shape[0]
    tm, tf = TM_MLP, TF_MLP
    tpb = seq // tm
    nm, nf = m // tm, D_FF // tf
    assert nf >= 3
    any_spec = pl.BlockSpec(memory_space=pl.ANY)
    s_args, s_in_specs, s_out_shape, s_out_specs, s_scratch = _stream_operands(hosted)
    kernel = _host_streams(_mlp_p_kernel, 6, 1, 3, hosted, nm * nf,
                           lambda: pl.program_id(0) * nf + pl.program_id(1))
    return pl.pallas_call(
        kernel,
        grid=(nm, nf),
        in_specs=[pl.BlockSpec((tm, D_MODEL), lambda mi, f: (mi, 0)),
                  pl.BlockSpec((D_MODEL, tf), lambda mi, f: (0, f)),
                  pl.BlockSpec((tf, D_MODEL), lambda mi, f: (f, 0)),
                  pl.BlockSpec((None, 1, D_MODEL), lambda mi, f: (mi // tpb, 0, 5)),
                  pl.BlockSpec((1, D_MODEL), lambda mi, f: (0, 0)),
                  any_spec] + s_in_specs,
        out_specs=[any_spec] + s_out_specs,
        out_shape=[jax.ShapeDtypeStruct((m, D_MODEL), F32)] + s_out_shape,
        scratch_shapes=[pltpu.VMEM((tm, D_MODEL), F32), pltpu.VMEM((tm, D_MODEL), F32),
                        pltpu.SemaphoreType.DMA((2,))] + s_scratch,
        compiler_params=_cparams("arbitrary", "arbitrary"),
        name="mlp_prompt",
    )(h2, w_up, w_down, mod_p, norm_gf, x1, *s_args)


def _inproj_s_kernel(x_ref, sh_ref, sc_ref, g_ref, w_ref, o_ref, h_scr):
    @pl.when(pl.program_id(0) == 0)
    def _():
        y = _rms(x_ref[...]) * g_ref[...]
        h_scr[...] = (y * (1.0 + sc_ref[...]) + sh_ref[...]).astype(BF16)

    o_ref[...] = jnp.dot(h_scr[...], w_ref[...], preferred_element_type=F32)


def _inproj_sample(x_s, mod_s, norm_g, w_in):
    m = x_s.shape[0]
    tn = D_ATTN
    return pl.pallas_call(
        _inproj_s_kernel,
        grid=(D_IN // tn,),
        in_specs=[pl.BlockSpec((m, D_MODEL), lambda j: (0, 0)),
                  pl.BlockSpec((m, D_MODEL), lambda j: (0, 0)),
                  pl.BlockSpec((m, D_MODEL), lambda j: (0, 1)),
                  pl.BlockSpec((1, D_MODEL), lambda j: (0, 0)),
                  pl.BlockSpec((D_MODEL, tn), lambda j: (0, j))],
        out_specs=pl.BlockSpec((m, tn), lambda j: (0, j)),
        out_shape=jax.ShapeDtypeStruct((m, D_IN), F32),
        scratch_shapes=[pltpu.VMEM((m, D_MODEL), BF16)],
        compiler_params=_cparams("arbitrary"),
        name="inproj_sample",
    )(x_s, mod_s, mod_s, norm_g, w_in)


def _attn_s_kernel(bias_ref, q_ref, kn_ref, vn_ref, k0_ref, v0_ref, k1_ref, v1_ref, k2_ref, v2_ref, o_ref):
    k_refs = (k0_ref, k1_ref, k2_ref)
    v_refs = (v0_ref, v1_ref, v2_ref)

    def body(s, carry):
        outs, lses = [], []
        for g in range(N_GROUPS):
            hs = slice(g * HEADS_PER_GROUP, (g + 1) * HEADS_PER_GROUP)
            q = q_ref[s, hs, :]
            kn = kn_ref[s, hs, :]
            vn = vn_ref[s, hs, :]
            kt = k_refs[g][s]
            vt = v_refs[g][s]
            sh = jnp.sum(kt * q[None], axis=-1, keepdims=True) * SCALE + bias_ref[g]
            sn = jnp.sum(kn * q, axis=-1, keepdims=True) * SCALE
            mx = jnp.maximum(jnp.max(sh, axis=0), sn)
            p = jnp.exp(sh - mx[None])
            pn = jnp.exp(sn - mx)
            den = jnp.sum(p, axis=0) + pn
            num = jnp.sum(p * vt, axis=0) + pn * vn
            outs.append(num / den)
            lses.append(mx + jnp.log(den))
        mx = jnp.maximum(jnp.maximum(lses[0], lses[1]), lses[2])
        e = [jnp.exp(l - mx) for l in lses]
        tot = e[0] + e[1] + e[2]
        o_ref[s] = (e[0] * outs[0] + e[1] * outs[1] + e[2] * outs[2]) / tot
        return carry

    lax.fori_loop(0, q_ref.shape[0], body, 0)


def _attn_sample(bias_s, q3, kn3, vn3, caches):
    nb = q3.shape[0]
    bs = BS_ATTN
    tok = pl.BlockSpec((bs, N_HEADS, HEAD_DIM), lambda i: (i, 0, 0))
    in_specs = [_const_spec((N_GROUPS, BAND, HEADS_PER_GROUP, HEAD_DIM)), tok, tok, tok]
    args = [bias_s, q3, kn3, vn3]
    for c, (_, dil) in zip(caches, ATTN_GROUPS):
        cv = c.reshape(nb, BAND, dil * 2, HEADS_PER_GROUP, HEAD_DIM)
        for kv in range(2):
            in_specs.append(pl.BlockSpec((bs, BAND, None, HEADS_PER_GROUP, HEAD_DIM),
                                         lambda i, kv=kv: (i, 0, kv, 0, 0)))
            args.append(cv)
    return pl.pallas_call(
        _attn_s_kernel,
        grid=(nb // bs,),
        in_specs=in_specs,
        out_specs=pl.BlockSpec((bs, HEADS_PER_GROUP, HEAD_DIM), lambda i: (i, 0, 0)),
        out_shape=jax.ShapeDtypeStruct((nb, HEADS_PER_GROUP, HEAD_DIM), F32),
        compiler_params=_cparams("arbitrary"),
        name="attn_sample",
    )(*args)


def _mix_s_kernel(o_ref, u_ref, hist_ref, ga_ref, gb_ref, x_ref, gt1_ref, sh2_ref, sc2_ref,
                  wua_ref, wp_ref, ps_ref, wup_ref, wo_ref, g2_ref, x1_ref, h2_ref, np_ref):
    u = u_ref[...]
    zs = []
    for g, w in enumerate(POOL_WINDOWS):
        sl = slice(g * POOL_GROUP, (g + 1) * POOL_GROUP)
        win = u[:, sl]
        for jj in range(1, w):
            win = win + hist_ref[:, POOL_HIST - jj, sl]
        zs.append(win / float(w) - u[:, sl])
    p = _pool_project(zs, wp_ref, ps_ref)
    np_ref[:, 0:POOL_HIST - 1, :] = hist_ref[:, 1:POOL_HIST, :]
    np_ref[:, POOL_HIST - 1, :] = u
    _mix_tail([slice(None)], [o_ref[...]], [p],
              lambda rs: (jax.nn.sigmoid(ga_ref[rs, :]), jax.nn.sigmoid(gb_ref[rs, :])),
              lambda rs: x_ref[rs, :],
              lambda rs: (gt1_ref[rs, :], sh2_ref[rs, :], sc2_ref[rs, :]),
              wua_ref, wup_ref, wo_ref, g2_ref, x1_ref, h2_ref)


def _mix_sample(o_attn, proj_s, hist, x_s, mod_s, w_up_attn, w_pool, pool_scale, w_up_pool, w_out, norm_g2):
    m = x_s.shape[0]

    def cols(width, c):
        return pl.BlockSpec((m, width), lambda i: (0, c))

    u0 = 3 * D_ATTN
    u = lax.slice_in_dim(proj_s, u0, u0 + D_POOL, axis=1)
    ga = lax.slice_in_dim(proj_s, u0 + D_POOL, u0 + D_POOL + D_MODEL, axis=1)
    gb = lax.slice_in_dim(proj_s, u0 + D_POOL + D_MODEL, D_IN, axis=1)
    in_specs = [cols(D_GROUP, 0), cols(D_POOL, 0), _const_spec((m, POOL_HIST, D_POOL)),
                cols(D_MODEL, 0), cols(D_MODEL, 0),
                cols(D_MODEL, 0), cols(D_MODEL, 2), cols(D_MODEL, 3), cols(D_MODEL, 4),
                _const_spec((D_GROUP, D_MODEL)), _const_spec((len(POOL_WINDOWS), POOL_GROUP, POOL_GROUP)),
                _const_spec((1, D_POOL)), _const_spec((D_POOL, D_MODEL)), _const_spec((D_MODEL, D_MODEL)),
                _const_spec((1, D_MODEL))]
    return pl.pallas_call(
        _mix_s_kernel,
        grid=(1,),
        in_specs=in_specs,
        out_specs=[_const_spec((m, D_MODEL)), _const_spec((m, D_MODEL)), _const_spec((m, POOL_HIST, D_POOL))],
        out_shape=[jax.ShapeDtypeStruct((m, D_MODEL), F32), jax.ShapeDtypeStruct((m, D_MODEL), BF16),
                   jax.ShapeDtypeStruct((m, POOL_HIST, D_POOL), F32)],
        compiler_params=_cparams("arbitrary"),
        name="mix_sample",
    )(o_attn, u, hist, ga, gb, x_s, mod_s, mod_s, mod_s, w_up_attn, w_pool, pool_scale, w_up_pool, w_out, norm_g2)


class _ShiftStream:
    def __init__(self, cache, new, out, buf, sems, rows, ns):
        self.cache, self.new, self.out, self.buf, self.sems = cache, new, out, buf, sems
        self.rows, self.ns = rows, ns
        self.slots = buf.shape[0]
        nb, w = cache.shape[1], cache.shape[2]
        assert w % rows == 0 and nb % ns == 0 and (ns == 1 or rows == w)
        self.cps = w // rows
        self.n_chunks = (nb // ns) * self.cps

    def _where(self, k):
        return (k // self.cps) * self.ns, (k % self.cps) * self.rows, k % self.slots

    def _body_in(self, k):
        b0, w0, slot = self._where(k)
        return pltpu.make_async_copy(self.cache.at[0, pl.ds(b0, self.ns), pl.ds(w0 + 1, self.rows - 1)],
                                     self.buf.at[slot, :, pl.ds(0, self.rows - 1)], self.sems.at[0, slot])

    def _next_in(self, k):
        b0, w0, slot = self._where(k)
        return pltpu.make_async_copy(self.cache.at[0, pl.ds(b0, self.ns), pl.ds(w0 + self.rows, 1)],
                                     self.buf.at[slot, :, pl.ds(self.rows - 1, 1)], self.sems.at[1, slot])

    def _new_in(self, k):
        b0, _, slot = self._where(k)
        return pltpu.make_async_copy(self.new.at[pl.ds(b0, self.ns)],
                                     self.buf.at[slot, :, self.rows - 1], self.sems.at[1, slot])

    def _out(self, k):
        b0, w0, slot = self._where(k)
        return pltpu.make_async_copy(self.buf.at[slot],
                                     self.out.at[0, pl.ds(b0, self.ns), pl.ds(w0, self.rows)], self.sems.at[2, slot])

    def _last_row(self, k, act):
        if isinstance(k, int) or self.cps == 1:
            at_end = self.cps == 1 or k % self.cps == self.cps - 1
            getattr(self._new_in(k) if at_end else self._next_in(k), act)()
            return
        at_end = k % self.cps == self.cps - 1

        @pl.when(at_end)
        def _():
            getattr(self._new_in(k), act)()

        @pl.when(jnp.logical_not(at_end))
        def _():
            getattr(self._next_in(k), act)()

    def start_in(self, k):
        self._body_in(k).start()
        self._last_row(k, "start")

    def wait_in(self, k):
        self._body_in(k).wait()
        self._last_row(k, "wait")

    def start_out(self, k):
        self._out(k).start()

    def wait_out(self, k):
        self._out(k).wait()

    def step(self, k, lag):
        self.wait_in(k)
        self.start_out(k)

        @pl.when(k - lag >= 0)
        def _():
            self.wait_out(k - lag)

        @pl.when(k - lag + self.slots < self.n_chunks)
        def _():
            self.start_in(k - lag + self.slots)

    def tick(self, s, n_steps):
        n = self.n_chunks
        per = -(-n // n_steps)
        stride = max(n_steps // n, 1)
        assert self.slots >= 2 * per and n >= self.slots

        @pl.when(s == 0)
        def _():
            for k in range(self.slots - per):
                self.start_in(k)

        for i in range(per):
            k = (s // stride) * per + i

            @pl.when((s % stride == 0) & (k < n))
            def _(k=k):
                self.step(k, per)

        @pl.when(s == n_steps - 1)
        def _():
            for k in range(n - per, n):
                self.wait_out(k)


class _Hosted(NamedTuple):
    cache: jax.Array
    new: jax.Array
    rows: int
    ns: int
    slots: int


def _host_streams(body, n_in, n_out, n_scr, hosted, n_steps, step_index):
    plans = [(h.rows, h.ns) for h in hosted]
    ns_ = len(plans)

    def kernel(*refs):
        refs = list(refs)
        ins, s_in = refs[:n_in], refs[n_in:n_in + 2 * ns_]
        o0 = n_in + 2 * ns_
        outs, s_out = refs[o0:o0 + n_out], refs[o0 + n_out:o0 + n_out + ns_]
        c0 = o0 + n_out + ns_
        scr, s_scr = refs[c0:c0 + n_scr], refs[c0 + n_scr:]
        s = step_index()
        for i, (rows, ns) in enumerate(plans):
            _ShiftStream(s_in[2 * i], s_in[2 * i + 1], s_out[i], s_scr[2 * i], s_scr[2 * i + 1],
                         rows, ns).tick(s, n_steps)
        body(*ins, *outs, *scr)

    return kernel


def _stream_operands(hosted):
    any_spec = pl.BlockSpec(memory_space=pl.ANY)
    args = [a for h in hosted for a in (h.cache, h.new)]
    out_shape = [jax.ShapeDtypeStruct(h.cache.shape, h.cache.dtype) for h in hosted]
    scratch = [s for h in hosted for s in (
        pltpu.VMEM((h.slots, h.ns, h.rows, 2, HEADS_PER_GROUP, HEAD_DIM), F32),
        pltpu.SemaphoreType.DMA((3, h.slots)))]
    return args, [any_spec] * len(args), out_shape, [any_spec] * len(hosted), scratch


def _alibi_slopes():
    h = jnp.arange(1, N_HEADS + 1, dtype=F32)
    return jnp.exp2(-ALIBI_MAX_BIAS * h / N_HEADS)


def _prompt_bias(g):
    dil = ATTN_GROUPS[g][1]
    slopes = _alibi_slopes()[g * HEADS_PER_GROUP:(g + 1) * HEADS_PER_GROUP]
    a = jnp.arange(BAND)[:, None]
    b = jnp.arange(2 * BAND)[None, :]
    dist = a - b + BAND
    valid = (dist >= 0) & (dist <= BAND)
    bias = -slopes[:, None, None] * (dist * dil).astype(F32)
    return jnp.where(valid[None], bias, NEG)


def _sample_bias():
    slopes = _alibi_slopes().reshape(N_GROUPS, 1, HEADS_PER_GROUP, 1)
    dil = jnp.array([d for _, d in ATTN_GROUPS], F32).reshape(N_GROUPS, 1, 1, 1)
    back = (BAND - jnp.arange(BAND, dtype=F32)).reshape(1, BAND, 1, 1)
    return jnp.broadcast_to(-slopes * (back * dil), (N_GROUPS, BAND, HEADS_PER_GROUP, HEAD_DIM))


def kernel(x_prompt, x_sample, c_prompt, c_sample, cache_kv_w128, cache_kv_w512, cache_kv_w2048, state_pool,
           norm_mix_g, w_ada, b_ada, w_in, w_up_attn, w_pool, pool_scale, w_up_pool, w_out, norm_mlp_g,
           w_mlp_up, w_mlp_down, norm_final_g):
    batch, seq, _ = x_prompt.shape
    nb = x_sample.shape[0]
    depth = w_in.shape[0]
    assert depth == 1 and x_sample.shape[1] == 1
    caches = (cache_kv_w128, cache_kv_w512, cache_kv_w2048)
    for c, (w, dil) in zip(caches, ATTN_GROUPS):
        assert c.shape[2] == w == BAND * dil

    w_in_b = w_in[0].astype(BF16)
    w_ua_b = w_up_attn[0].astype(BF16)
    w_pool_b = w_pool[0].astype(BF16)
    w_up_b = w_up_pool[0].astype(BF16)
    w_out_b = w_out[0].astype(BF16)
    w_mu_b = w_mlp_up[0].astype(BF16)
    w_md_b = w_mlp_down[0].astype(BF16)
    g1 = norm_mix_g[0].reshape(1, D_MODEL)
    g2 = norm_mlp_g[0].reshape(1, D_MODEL)
    gf = norm_final_g.reshape(1, D_MODEL)
    ps = pool_scale[0].reshape(1, D_POOL)

    mod = _ada(jnp.concatenate([c_sample, c_prompt], axis=0), w_ada[0], b_ada[0])
    mod_s = mod[:nb]
    mod_p = mod[nb:].reshape(batch, 1, 6 * D_MODEL)

    xs = x_sample.reshape(nb, D_MODEL)
    proj_s = _inproj_sample(xs, mod_s, g1, w_in_b)
    q3 = proj_s[:, 0:D_ATTN].reshape(nb, N_HEADS, HEAD_DIM)
    kn3 = proj_s[:, D_ATTN:2 * D_ATTN].reshape(nb, N_HEADS, HEAD_DIM)
    vn3 = proj_s[:, 2 * D_ATTN:3 * D_ATTN].reshape(nb, N_HEADS, HEAD_DIM)
    o_s = _attn_sample(_sample_bias(), q3, kn3, vn3, [c[0] for c in caches])
    new_rows = [jnp.stack([kn3[:, g * HEADS_PER_GROUP:(g + 1) * HEADS_PER_GROUP],
                           vn3[:, g * HEADS_PER_GROUP:(g + 1) * HEADS_PER_GROUP]], axis=1)
                for g in range(N_GROUPS)]
    x1_s, h2_s, pool_s = _mix_sample(o_s.reshape(nb, D_GROUP), proj_s, state_pool[0], xs, mod_s,
                                     w_ua_b, w_pool_b, ps, w_up_b, w_out_b, g2)
    y_s = _mlp_sample(h2_s, x1_s, mod_s, w_mu_b, w_md_b, gf)

    x2d = x_prompt.reshape(batch * seq, D_MODEL)
    *qkv_sub, u_p, gates, kv2, kv1, kv0, kv_s1 = _inproj_prompt(
        x2d, mod_p, g1, w_in_b, batch, seq,
        (_Hosted(caches[1], new_rows[1], rows=512, ns=1, slots=3),))
    outs, lses = [], []
    for g in range(N_GROUPS):
        o, l = _attn_prompt_group(qkv_sub[g], _prompt_bias(g), g, batch, seq)
        outs.append(o)
        lses.append(l)
    x1_p, h2_p, kv_s0 = _mix_prompt(
        outs, lses, u_p, gates, x2d, mod_p, w_ua_b, w_pool_b, ps, w_up_b, w_out_b, g2, seq,
        (_Hosted(caches[0], new_rows[0], rows=128, ns=4, slots=3),))
    y_p, kv_s2 = _mlp_prompt(
        h2_p, x1_p, mod_p, w_mu_b, w_md_b, gf, seq,
        (_Hosted(caches[2], new_rows[2], rows=1024, ns=1, slots=3),))

    pool_p = u_p.reshape(batch, seq, D_POOL)[:, seq - POOL_HIST:][None]
    kv0, kv1, kv2 = (a.reshape(1, batch, -1, 2, HEADS_PER_GROUP, HEAD_DIM) for a in (kv0, kv1, kv2))
    return (y_p.reshape(batch, seq, D_MODEL), y_s.reshape(nb, 1, D_MODEL), kv0, kv1, kv2, pool_p,
            kv_s0, kv_s1, kv_s2, pool_s[None])
```

```python
import functools
from typing import NamedTuple

import jax
import jax.numpy as jnp
from jax import lax
from jax.experimental import pallas as pl
from jax.experimental.pallas import tpu as pltpu

F32 = jnp.float32
BF16 = jnp.bfloat16

D_MODEL = 2048
HEAD_DIM = 128
HEADS_PER_GROUP = 4
ATTN_GROUPS = ((128, 1), (512, 4), (2048, 16))
N_GROUPS = len(ATTN_GROUPS)
N_HEADS = HEADS_PER_GROUP * N_GROUPS
D_ATTN = N_HEADS * HEAD_DIM
D_GROUP = HEADS_PER_GROUP * HEAD_DIM
BAND = 128
POOL_WINDOWS = (2, 4, 8, 16)
POOL_GROUP = 128
D_POOL = POOL_GROUP * len(POOL_WINDOWS)
POOL_HIST = max(POOL_WINDOWS) - 1
POOL_HALO = 16
D_FF = 4 * D_MODEL
D_IN = 3 * D_ATTN + D_POOL + 2 * D_MODEL
ALIBI_MAX_BIAS = 8.0
EPS = 1e-6
SCALE = HEAD_DIM ** -0.5
NEG = -1e30

VMEM_LIMIT_BYTES = 60 * 1024 * 1024

IN_TILE = D_GROUP
N_IN_TILES = D_IN // IN_TILE
QKV_TILES = 3 * N_GROUPS
POOL_TILE = QKV_TILES
GATE_TILE0 = POOL_TILE + 1

TM_IN = 1024
IN_SPLIT = 2
TM_MIX = 256
MIX_SPLIT = 2
TM_MLP = 1024
TF_MLP = 512
TF_MLP_SAMPLE = 512
BS_ATTN = 8
ATTN_UNROLL = 3


def _cparams(*sem):
    return pltpu.CompilerParams(dimension_semantics=sem, vmem_limit_bytes=VMEM_LIMIT_BYTES)


def _rms(x):
    return x * lax.rsqrt(jnp.mean(x * x, axis=-1, keepdims=True) + EPS)


def _ada_kernel(c_ref, w_ref, b_ref, o_ref):
    c = c_ref[...]
    a = (c * jax.nn.sigmoid(c)).astype(BF16)
    o_ref[...] = jnp.dot(a, w_ref[...].astype(BF16), preferred_element_type=F32) + b_ref[...]


def _ada(c_all, w_ada, b_ada):
    m = c_all.shape[0]
    tn = 1024
    return pl.pallas_call(
        _ada_kernel,
        grid=(6 * D_MODEL // tn,),
        in_specs=[pl.BlockSpec((m, D_MODEL), lambda j: (0, 0)),
                  pl.BlockSpec((D_MODEL, tn), lambda j: (0, j)),
                  pl.BlockSpec((1, tn), lambda j: (0, j))],
        out_specs=pl.BlockSpec((m, tn), lambda j: (0, j)),
        out_shape=jax.ShapeDtypeStruct((m, 6 * D_MODEL), F32),
        compiler_params=_cparams("arbitrary"),
        name="ada",
    )(c_all, w_ada, b_ada.reshape(1, 6 * D_MODEL))


KV_ROWS = 2 * HEADS_PER_GROUP


def _store_heads(ref, row0, kv, val):
    for h in range(HEADS_PER_GROUP):
        start = row0 * KV_ROWS + kv * HEADS_PER_GROUP + h
        ref[pl.ds(start, val.shape[0], stride=KV_ROWS), :] = val[:, h * HEAD_DIM:(h + 1) * HEAD_DIM]


def _inproj_p_kernel(sh_ref, sc_ref, g_ref, w_ref, x_hbm,
                     qkv0_ref, qkv1_ref, qkv2_ref, u_ref, gate_ref, kv2_ref, kv1_ref, kv0_ref,
                     h_scr, acc_scr, x_scr, x_sem, *, tiles_per_batch):
    mi, j = pl.program_id(0), pl.program_id(1)
    tm = x_scr.shape[0]

    def x_copy(i):
        return pltpu.make_async_copy(x_hbm.at[pl.ds(i * tm, tm)], x_scr, x_sem.at[0])

    @pl.when(j == 0)
    def _():
        @pl.when(mi == 0)
        def _():
            x_copy(mi).start()

        x_copy(mi).wait()
        y = _rms(x_scr[...]) * g_ref[...]
        h_scr[...] = (y * (1.0 + sc_ref[...]) + sh_ref[...]).astype(BF16)

    @pl.when((j == 1) & (mi + 1 < pl.num_programs(0)))
    def _():
        x_copy(mi + 1).start()

    th = tm // IN_SPLIT

    def halves():
        for s in range(IN_SPLIT):
            yield s, jnp.dot(h_scr[s * th:(s + 1) * th, :], w_ref[...], preferred_element_type=F32)

    def store_sub(qkv_ref, dil, s, acc):
        if dil == 1:
            qkv_ref[0, s * th:(s + 1) * th, :] = acc.astype(BF16)
            return
        n = th // dil
        for h in range(HEADS_PER_GROUP):
            sl = slice(h * HEAD_DIM, (h + 1) * HEAD_DIM)
            acc_scr[s, h] = acc[:, sl]
            for r in range(dil):
                qkv_ref[r, s * n:(s + 1) * n, sl] = acc_scr[s, h, pl.ds(r, n, stride=dil), :].astype(BF16)

    last = pl.program_id(0) % tiles_per_batch == tiles_per_batch - 1
    kv_refs = (kv0_ref, kv1_ref, kv2_ref)
    for g, (qkv_ref, (_, dil)) in enumerate(zip((qkv0_ref, qkv1_ref, qkv2_ref), ATTN_GROUPS)):
        @pl.when(j == g)
        def _(qkv_ref=qkv_ref, dil=dil):
            for s, acc in halves():
                store_sub(qkv_ref, dil, s, acc)

        @pl.when((j == N_GROUPS + g) | (j == 2 * N_GROUPS + g))
        def _(g=g, qkv_ref=qkv_ref, dil=dil):
            kv_ref = kv_refs[g]
            first_kept = tm - kv_ref.shape[0] // KV_ROWS
            is_v = (j >= 2 * N_GROUPS).astype(jnp.int32)
            for s, acc in halves():
                store_sub(qkv_ref, dil, s, acc)
                row0 = max(s * th, first_kept)
                if first_kept == 0:
                    _store_heads(kv_ref, s * th, is_v, acc)
                elif row0 < (s + 1) * th:
                    @pl.when(last)
                    def _(acc=acc, row0=row0, s=s):
                        _store_heads(kv_ref, row0 - first_kept, is_v, acc[row0 - s * th:, :])

    @pl.when(j == POOL_TILE)
    def _():
        for s, acc in halves():
            u_ref[s * th:(s + 1) * th, :] = acc

    @pl.when(j >= GATE_TILE0)
    def _():
        for s, acc in halves():
            gate_ref[s * th:(s + 1) * th, :] = (0.5 * jnp.tanh(0.5 * acc) + 0.5).astype(BF16)


def _inproj_prompt(x2d, mod_p, norm_g, w_in, batch, seq, hosted):
    m = x2d.shape[0]
    tm = TM_IN
    tpb = seq // tm
    keep = [min(w, seq) for w, _ in ATTN_GROUPS]
    assert keep[2] == seq and tm >= keep[1] and seq % tm == 0

    def sub_spec(g):
        dil = ATTN_GROUPS[g][1]
        return pl.BlockSpec((None, None, dil, tm // dil, D_GROUP),
                            lambda mi, j: (jnp.clip((j - g) // N_GROUPS, 0, 2), mi // tpb, 0, mi % tpb, 0))

    def sub_shape(g):
        dil = ATTN_GROUPS[g][1]
        return jax.ShapeDtypeStruct((3, batch, dil, seq // dil, D_GROUP), BF16)

    def kv_spec(rows, tail_only):
        return pl.BlockSpec((None, rows * KV_ROWS, HEAD_DIM),
                            lambda mi, j: (mi // tpb, 0 if tail_only else mi % tpb, 0))

    def kv_shape(rows):
        return jax.ShapeDtypeStruct((batch, rows * KV_ROWS, HEAD_DIM), F32)

    in_specs = [
        pl.BlockSpec((None, 1, D_MODEL), lambda mi, j: (mi // tpb, 0, 0)),
        pl.BlockSpec((None, 1, D_MODEL), lambda mi, j: (mi // tpb, 0, 1)),
        pl.BlockSpec((1, D_MODEL), lambda mi, j: (0, 0)),
        pl.BlockSpec((D_MODEL, IN_TILE), lambda mi, j: (0, j)),
        pl.BlockSpec(memory_space=pl.ANY),
    ]
    out_specs = [
        sub_spec(0), sub_spec(1), sub_spec(2),
        pl.BlockSpec((tm, IN_TILE), lambda mi, j: (mi, 0)),
        pl.BlockSpec((tm, IN_TILE), lambda mi, j: (mi, jnp.clip(j - GATE_TILE0, 0, N_IN_TILES - GATE_TILE0 - 1))),
        kv_spec(tm, False), kv_spec(keep[1], True), kv_spec(keep[0], True),
    ]
    out_shape = [
        sub_shape(0), sub_shape(1), sub_shape(2),
        jax.ShapeDtypeStruct((m, D_POOL), F32),
        jax.ShapeDtypeStruct((m, 2 * D_MODEL), BF16),
        kv_shape(keep[2]), kv_shape(keep[1]), kv_shape(keep[0]),
    ]
    s_args, s_in_specs, s_out_shape, s_out_specs, s_scratch = _stream_operands(hosted)
    kernel = _host_streams(functools.partial(_inproj_p_kernel, tiles_per_batch=tpb), len(in_specs), len(out_specs),
                           4, hosted, (m // tm) * N_IN_TILES,
                           lambda: pl.program_id(0) * N_IN_TILES + pl.program_id(1))
    return pl.pallas_call(
        kernel,
        grid=(m // tm, N_IN_TILES),
        in_specs=in_specs + s_in_specs, out_specs=out_specs + s_out_specs, out_shape=out_shape + s_out_shape,
        scratch_shapes=[pltpu.VMEM((tm, D_MODEL), BF16),
                        pltpu.VMEM((IN_SPLIT, HEADS_PER_GROUP, tm // IN_SPLIT, HEAD_DIM), F32),
                        pltpu.VMEM((tm, D_MODEL), F32), pltpu.SemaphoreType.DMA((1,))] + s_scratch,
        compiler_params=_cparams("arbitrary", "arbitrary"),
        name="inproj_prompt",
    )(mod_p, mod_p, norm_g, w_in, x2d, *s_args)


def _attn_block(q, k, v, bias_fn):
    lane = lax.broadcasted_iota(jnp.int32, (BAND, HEAD_DIM), 1)
    heads = [slice(h * HEAD_DIM, (h + 1) * HEAD_DIM) for h in range(HEADS_PER_GROUP)]
    scores = [lax.dot_general(q[:, sl], k[:, sl], (((1,), (1,)), ((), ())), preferred_element_type=F32)
              for sl in heads]
    scores = [s * SCALE + bias_fn(h) for h, s in enumerate(scores)]
    maxes = [jnp.max(s, axis=-1, keepdims=True) for s in scores]
    probs = [jnp.exp(s - mx) for s, mx in zip(scores, maxes)]
    dens = [jnp.sum(p, axis=-1, keepdims=True) for p in probs]
    outs = [jnp.dot(p.astype(BF16), v[:, sl], preferred_element_type=F32) / den
            for p, sl, den in zip(probs, heads, dens)]
    lse = jnp.zeros((BAND, HEAD_DIM), F32)
    for h, (mx, den) in enumerate(zip(maxes, dens)):
        lse = jnp.where(lane == h, mx + jnp.log(den), lse)
    return outs, lse


def _attn_p_kernel(bias_ref, q_ref, k_ref, v_ref, o_ref, l_ref, o_scr):
    dil, n, _ = q_ref.shape
    nb = n // BAND

    def rows(start):
        return pl.ds(start, BAND) if dil == 1 else pl.ds(start, BAND, stride=dil)

    def put(start, outs, lse):
        for h in range(HEADS_PER_GROUP):
            o_scr[h, rows(start), :] = outs[h]
        l_ref[rows(start), :] = lse

    for r in range(dil):
        put(r, *_attn_block(q_ref[r, 0:BAND, :], k_ref[r, 0:BAND, :], v_ref[r, 0:BAND, :],
                            lambda h: bias_ref[h, :, BAND:2 * BAND]))

        if nb > 1:
            def body(i, carry, r=r):
                r0 = pl.multiple_of(i * BAND, BAND)
                rk = pl.multiple_of((i - 1) * BAND, BAND)
                put(r0 * dil + r, *_attn_block(q_ref[r, pl.ds(r0, BAND), :], k_ref[r, pl.ds(rk, 2 * BAND), :],
                                               v_ref[r, pl.ds(rk, 2 * BAND), :], lambda h: bias_ref[h]))
                return carry

            lax.fori_loop(1, nb, body, 0, unroll=ATTN_UNROLL)

    for h in range(HEADS_PER_GROUP):
        o_ref[:, h * HEAD_DIM:(h + 1) * HEAD_DIM] = o_scr[h].astype(BF16)


def _attn_prompt_group(qkv_sub, bias, g, batch, seq):
    dil = ATTN_GROUPS[g][1]
    n = seq // dil

    def sub(which):
        return pl.BlockSpec((None, None, dil, n, D_GROUP), lambda b: (which, b, 0, 0, 0))

    return pl.pallas_call(
        _attn_p_kernel,
        grid=(batch,),
        in_specs=[pl.BlockSpec((HEADS_PER_GROUP, BAND, 2 * BAND), lambda b: (0, 0, 0)),
                  sub(0), sub(1), sub(2)],
        out_specs=[pl.BlockSpec((seq, D_GROUP), lambda b: (b, 0)),
                   pl.BlockSpec((seq, HEAD_DIM), lambda b: (b, 0))],
        out_shape=[jax.ShapeDtypeStruct((batch * seq, D_GROUP), BF16),
                   jax.ShapeDtypeStruct((batch * seq, HEAD_DIM), F32)],
        scratch_shapes=[pltpu.VMEM((HEADS_PER_GROUP, seq, HEAD_DIM), F32)],
        compiler_params=_cparams("arbitrary"),
        name=f"attn_prompt_g{g}",
    )(bias, qkv_sub, qkv_sub, qkv_sub)


def _merge_heads(outs, lses):
    cols = []
    for h in range(HEADS_PER_GROUP):
        sl = slice(h * HEAD_DIM, (h + 1) * HEAD_DIM)
        l = [lg[:, h:h + 1] for lg in lses]
        mx = jnp.maximum(jnp.maximum(l[0], l[1]), l[2])
        e = [jnp.exp(x - mx) for x in l]
        tot = e[0] + e[1] + e[2]
        cols.append(sum((e[g] / tot) * outs[g][:, sl].astype(F32) for g in range(N_GROUPS)))
    return jnp.concatenate(cols, axis=-1)


def _mix_tail(parts, o_attn, p, load_gates, load_x, load_mod, wua_ref, wup_ref, wo_ref, g2_ref, x1_ref, h2_ref):
    a = [jnp.dot(o.astype(BF16), wua_ref[...], preferred_element_type=F32) for o in o_attn]
    b = [jnp.dot(q.astype(BF16), wup_ref[...], preferred_element_type=F32) for q in p]
    mix = []
    for rs, ai, bi in zip(parts, a, b):
        ga, gb = load_gates(rs)
        mix.append((ga * ai + gb * bi).astype(BF16))
    y = [jnp.dot(m, wo_ref[...], preferred_element_type=F32) for m in mix]
    for rs, yi in zip(parts, y):
        gt1, sh2, sc2 = load_mod(rs)
        x1 = load_x(rs) + gt1 * yi
        x1_ref[rs, :] = x1
        h2_ref[rs, :] = (_rms(x1) * g2_ref[...] * (1.0 + sc2) + sh2).astype(BF16)


def _pool_project(z_groups, wp_ref, ps_ref):
    cols = [jnp.dot(z.astype(BF16), wp_ref[g], preferred_element_type=F32) for g, z in enumerate(z_groups)]
    return jnp.concatenate(cols, axis=-1) * ps_ref[...]


def _mix_p_kernel(o0_ref, o1_ref, o2_ref, l0_ref, l1_ref, l2_ref, u_ref, uh_ref, gate_ref, x_ref,
                  gt1_ref, sh2_ref, sc2_ref, wua_ref, wp_ref, ps_ref, wup_ref, wo_ref, g2_ref,
                  x1_ref, h2_ref, ext_scr, *, tiles_per_batch):
    tm = u_ref.shape[0]
    tp = tm // MIX_SPLIT
    t = pl.program_id(0) % tiles_per_batch
    parts = [slice(i * tp, (i + 1) * tp) for i in range(MIX_SPLIT)]
    o_attn = [_merge_heads([o0_ref[rs, :], o1_ref[rs, :], o2_ref[rs, :]], [l0_ref[rs, :], l1_ref[rs, :], l2_ref[rs, :]])
              for rs in parts]

    ext_scr[0:POOL_HALO, :] = jnp.where(t == 0, 0.0, uh_ref[...])
    ext_scr[POOL_HALO:, :] = u_ref[...]
    p = []
    for i, rs in enumerate(parts):
        pos = t * tm + i * tp + lax.broadcasted_iota(jnp.int32, (tp, 1), 0)
        zs = []
        for g, w in enumerate(POOL_WINDOWS):
            sl = slice(g * POOL_GROUP, (g + 1) * POOL_GROUP)
            u = u_ref[rs, sl]
            win = u
            for jj in range(1, w):
                r0 = POOL_HALO + i * tp - jj
                win = win + ext_scr[r0:r0 + tp, sl]
            cnt = jnp.minimum(pos + 1, w).astype(F32)
            zs.append(win / cnt - u)
        p.append(_pool_project(zs, wp_ref, ps_ref))

    _mix_tail(parts, o_attn, p,
              lambda rs: (gate_ref[rs, :D_MODEL].astype(F32), gate_ref[rs, D_MODEL:].astype(F32)),
              lambda rs: x_ref[rs, :],
              lambda rs: (gt1_ref[...], sh2_ref[...], sc2_ref[...]),
              wua_ref, wup_ref, wo_ref, g2_ref, x1_ref, h2_ref)


def _const_spec(shape):
    nd = len(shape)
    return pl.BlockSpec(shape, lambda *_: (0,) * nd)


def _mix_prompt(outs, lses, u, gates, x2d, mod_p, w_up_attn, w_pool, pool_scale, w_up_pool, w_out, norm_g2, seq,
                hosted):
    m = x2d.shape[0]
    tm = TM_MIX
    tpb = seq // tm
    hb = tm // POOL_HALO

    def row(width):
        return pl.BlockSpec((tm, width), lambda mi: (mi, 0))

    def mod(c):
        return pl.BlockSpec((None, 1, D_MODEL), lambda mi: (mi // tpb, 0, c))

    in_specs = ([row(D_GROUP)] * 3 + [row(HEAD_DIM)] * 3 + [
        row(D_POOL),
        pl.BlockSpec((POOL_HALO, D_POOL), lambda mi: (jnp.maximum(mi * hb - 1, 0), 0)),
        row(2 * D_MODEL), row(D_MODEL), mod(2), mod(3), mod(4),
        _const_spec((D_GROUP, D_MODEL)), _const_spec((len(POOL_WINDOWS), POOL_GROUP, POOL_GROUP)),
        _const_spec((1, D_POOL)), _const_spec((D_POOL, D_MODEL)), _const_spec((D_MODEL, D_MODEL)),
        _const_spec((1, D_MODEL))])
    s_args, s_in_specs, s_out_shape, s_out_specs, s_scratch = _stream_operands(hosted)
    kernel = _host_streams(functools.partial(_mix_p_kernel, tiles_per_batch=tpb), len(in_specs), 2, 1, hosted,
                           m // tm, lambda: pl.program_id(0))
    return pl.pallas_call(
        kernel,
        grid=(m // tm,),
        in_specs=in_specs + s_in_specs,
        out_specs=[row(D_MODEL), row(D_MODEL)] + s_out_specs,
        out_shape=[jax.ShapeDtypeStruct((m, D_MODEL), F32), jax.ShapeDtypeStruct((m, D_MODEL), BF16)] + s_out_shape,
        scratch_shapes=[pltpu.VMEM((POOL_HALO + tm, D_POOL), F32)] + s_scratch,
        compiler_params=_cparams("arbitrary"),
        name="mix_prompt",
    )(*outs, *lses, u, u, gates, x2d, mod_p, mod_p, mod_p, w_up_attn, w_pool, pool_scale, w_up_pool, w_out, norm_g2,
      *s_args)


def _mlp_kernel(h_ref, wu_ref, wd_ref, x1_ref, gt_ref, gf_ref, y_ref, wub_ref, wdb_ref, acc_ref):
    f = pl.program_id(1)
    wu = wu_ref[...].astype(BF16)
    wd = wd_ref[...].astype(BF16)
    wub_ref[...] = wu
    wdb_ref[...] = wd
    a = jnp.dot(h_ref[...], wu, preferred_element_type=F32)
    a = jnp.square(jnp.maximum(a, 0.0)).astype(BF16)
    part = jnp.dot(a, wd, preferred_element_type=F32)

    @pl.when(f == 0)
    def _():
        acc_ref[...] = part

    @pl.when(f > 0)
    def _():
        acc_ref[...] += part

    @pl.when(f == pl.num_programs(1) - 1)
    def _():
        x2 = x1_ref[...] + gt_ref[...] * acc_ref[...]
        y_ref[...] = _rms(x2) * gf_ref[...]


def _mlp_sample(h2, x1, mod_s, w_up, w_down, norm_gf):
    m = h2.shape[0]
    tf = TF_MLP_SAMPLE
    up_spec = pl.BlockSpec((D_MODEL, tf), lambda mi, f: (0, f))
    down_spec = pl.BlockSpec((tf, D_MODEL), lambda mi, f: (f, 0))
    return pl.pallas_call(
        _mlp_kernel,
        grid=(1, D_FF // tf),
        in_specs=[pl.BlockSpec((m, D_MODEL), lambda mi, f: (0, 0)),
                  up_spec, down_spec,
                  pl.BlockSpec((m, D_MODEL), lambda mi, f: (0, 0)),
                  pl.BlockSpec((m, D_MODEL), lambda mi, f: (0, 5)),
                  pl.BlockSpec((1, D_MODEL), lambda mi, f: (0, 0))],
        out_specs=[pl.BlockSpec((m, D_MODEL), lambda mi, f: (0, 0)), up_spec, down_spec],
        out_shape=[jax.ShapeDtypeStruct((m, D_MODEL), F32),
                   jax.ShapeDtypeStruct((D_MODEL, D_FF), BF16), jax.ShapeDtypeStruct((D_FF, D_MODEL), BF16)],
        scratch_shapes=[pltpu.VMEM((m, D_MODEL), F32)],
        compiler_params=_cparams("arbitrary", "arbitrary"),
        name="mlp_sample",
    )(h2, w_up, w_down, x1, mod_s, norm_gf)


def _mlp_p_kernel(h_ref, wu_ref, wd_ref, gt_ref, gf_ref, x1_hbm, y_hbm, acc_ref, xy_ref, sems):
    mi, f = pl.program_id(0), pl.program_id(1)
    nm, nf = pl.num_programs(0), pl.num_programs(1)
    tm = acc_ref.shape[0]

    def x1_copy(i):
        return pltpu.make_async_copy(x1_hbm.at[pl.ds(i * tm, tm)], xy_ref, sems.at[0])

    def y_copy(i):
        return pltpu.make_async_copy(xy_ref, y_hbm.at[pl.ds(i * tm, tm)], sems.at[1])

    @pl.when(f == 1)
    def _():
        @pl.when(mi > 0)
        def _():
            y_copy(mi - 1).wait()

        x1_copy(mi).start()

    @pl.when(f == 0)
    def _():
        acc_ref[...] = jnp.zeros_like(acc_ref)

    a = jnp.dot(h_ref[...], wu_ref[...], preferred_element_type=F32)
    a = jnp.square(jnp.maximum(a, 0.0)).astype(BF16)
    acc_ref[...] += jnp.dot(a, wd_ref[...], preferred_element_type=F32)

    @pl.when(f == nf - 1)
    def _():
        x1_copy(mi).wait()
        x2 = xy_ref[...] + gt_ref[...] * acc_ref[...]
        xy_ref[...] = _rms(x2) * gf_ref[...]
        y_copy(mi).start()

        @pl.when(mi == nm - 1)
        def _():
            y_copy(mi).wait()


def _mlp_prompt(h2, x1, mod_p, w_up, w_down, norm_gf, seq, hosted):
    m = h2.shape[0]
    tm, tf = TM_MLP, TF_MLP
    tpb = seq // tm
    nm, nf = m // tm, D_FF // tf
    assert nf >= 3
    any_spec = pl.BlockSpec(memory_space=pl.ANY)
    s_args, s_in_specs, s_out_shape, s_out_specs, s_scratch = _stream_operands(hosted)
    kernel = _host_streams(_mlp_p_kernel, 6, 1, 3, hosted, nm * nf,
                           lambda: pl.program_id(0) * nf + pl.program_id(1))
    return pl.pallas_call(
        kernel,
        grid=(nm, nf),
        in_specs=[pl.BlockSpec((tm, D_MODEL), lambda mi, f: (mi, 0)),
                  pl.BlockSpec((D_MODEL, tf), lambda mi, f: (0, f)),
                  pl.BlockSpec((tf, D_MODEL), lambda mi, f: (f, 0)),
                  pl.BlockSpec((None, 1, D_MODEL), lambda mi, f: (mi // tpb, 0, 5)),
                  pl.BlockSpec((1, D_MODEL), lambda mi, f: (0, 0)),
                  any_spec] + s_in_specs,
        out_specs=[any_spec] + s_out_specs,
        out_shape=[jax.ShapeDtypeStruct((m, D_MODEL), F32)] + s_out_shape,
        scratch_shapes=[pltpu.VMEM((tm, D_MODEL), F32), pltpu.VMEM((tm, D_MODEL), F32),
                        pltpu.SemaphoreType.DMA((2,))] + s_scratch,
        compiler_params=_cparams("arbitrary", "arbitrary"),
        name="mlp_prompt",
    )(h2, w_up, w_down, mod_p, norm_gf, x1, *s_args)


def _inproj_s_kernel(x_ref, sh_ref, sc_ref, g_ref, w_ref, o_ref, wb_ref, h_scr):
    @pl.when(pl.program_id(0) == 0)
    def _():
        y = _rms(x_ref[...]) * g_ref[...]
        h_scr[...] = (y * (1.0 + sc_ref[...]) + sh_ref[...]).astype(BF16)

    w = w_ref[...].astype(BF16)
    wb_ref[...] = w
    o_ref[...] = jnp.dot(h_scr[...], w, preferred_element_type=F32)


def _inproj_sample(x_s, mod_s, norm_g, w_in):
    m = x_s.shape[0]
    tn = D_ATTN
    return pl.pallas_call(
        _inproj_s_kernel,
        grid=(D_IN // tn,),
        in_specs=[pl.BlockSpec((m, D_MODEL), lambda j: (0, 0)),
                  pl.BlockSpec((m, D_MODEL), lambda j: (0, 0)),
                  pl.BlockSpec((m, D_MODEL), lambda j: (0, 1)),
                  pl.BlockSpec((1, D_MODEL), lambda j: (0, 0)),
                  pl.BlockSpec((D_MODEL, tn), lambda j: (0, j))],
        out_specs=[pl.BlockSpec((m, tn), lambda j: (0, j)),
                   pl.BlockSpec((D_MODEL, tn), lambda j: (0, j))],
        out_shape=[jax.ShapeDtypeStruct((m, D_IN), F32), jax.ShapeDtypeStruct((D_MODEL, D_IN), BF16)],
        scratch_shapes=[pltpu.VMEM((m, D_MODEL), BF16)],
        compiler_params=_cparams("arbitrary"),
        name="inproj_sample",
    )(x_s, mod_s, mod_s, norm_g, w_in)


def _attn_s_kernel(bias_ref, q_ref, kn_ref, vn_ref, k0_ref, v0_ref, k1_ref, v1_ref, k2_ref, v2_ref, o_ref):
    k_refs = (k0_ref, k1_ref, k2_ref)
    v_refs = (v0_ref, v1_ref, v2_ref)

    def body(s, carry):
        outs, lses = [], []
        for g in range(N_GROUPS):
            hs = slice(g * HEADS_PER_GROUP, (g + 1) * HEADS_PER_GROUP)
            q = q_ref[s, hs, :]
            kn = kn_ref[s, hs, :]
            vn = vn_ref[s, hs, :]
            kt = k_refs[g][s]
            vt = v_refs[g][s]
            sh = jnp.sum(kt * q[None], axis=-1, keepdims=True) * SCALE + bias_ref[g]
            sn = jnp.sum(kn * q, axis=-1, keepdims=True) * SCALE
            mx = jnp.maximum(jnp.max(sh, axis=0), sn)
            p = jnp.exp(sh - mx[None])
            pn = jnp.exp(sn - mx)
            den = jnp.sum(p, axis=0) + pn
            num = jnp.sum(p * vt, axis=0) + pn * vn
            outs.append(num / den)
            lses.append(mx + jnp.log(den))
        mx = jnp.maximum(jnp.maximum(lses[0], lses[1]), lses[2])
        e = [jnp.exp(l - mx) for l in lses]
        tot = e[0] + e[1] + e[2]
        o_ref[s] = (e[0] * outs[0] + e[1] * outs[1] + e[2] * outs[2]) / tot
        return carry

    lax.fori_loop(0, q_ref.shape[0], body, 0)


def _attn_sample(bias_s, q3, kn3, vn3, caches):
    nb = q3.shape[0]
    bs = BS_ATTN
    tok = pl.BlockSpec((bs, N_HEADS, HEAD_DIM), lambda i: (i, 0, 0))
    in_specs = [_const_spec((N_GROUPS, BAND, HEADS_PER_GROUP, HEAD_DIM)), tok, tok, tok]
    args = [bias_s, q3, kn3, vn3]
    for c, (_, dil) in zip(caches, ATTN_GROUPS):
        cv = c.reshape(nb, BAND, dil * 2, HEADS_PER_GROUP, HEAD_DIM)
        for kv in range(2):
            in_specs.append(pl.BlockSpec((bs, BAND, None, HEADS_PER_GROUP, HEAD_DIM),
                                         lambda i, kv=kv: (i, 0, kv, 0, 0)))
            args.append(cv)
    return pl.pallas_call(
        _attn_s_kernel,
        grid=(nb // bs,),
        in_specs=in_specs,
        out_specs=pl.BlockSpec((bs, HEADS_PER_GROUP, HEAD_DIM), lambda i: (i, 0, 0)),
        out_shape=jax.ShapeDtypeStruct((nb, HEADS_PER_GROUP, HEAD_DIM), F32),
        compiler_params=_cparams("arbitrary"),
        name="attn_sample",
    )(*args)


def _mix_s_kernel(o_ref, u_ref, hist_ref, ga_ref, gb_ref, x_ref, gt1_ref, sh2_ref, sc2_ref,
                  wua_ref, wp_ref, ps_ref, wup_ref, wo_ref, g2_ref, x1_ref, h2_ref, np_ref):
    u = u_ref[...]
    zs = []
    for g, w in enumerate(POOL_WINDOWS):
        sl = slice(g * POOL_GROUP, (g + 1) * POOL_GROUP)
        win = u[:, sl]
        for jj in range(1, w):
            win = win + hist_ref[:, POOL_HIST - jj, sl]
        zs.append(win / float(w) - u[:, sl])
    p = _pool_project(zs, wp_ref, ps_ref)
    np_ref[:, 0:POOL_HIST - 1, :] = hist_ref[:, 1:POOL_HIST, :]
    np_ref[:, POOL_HIST - 1, :] = u
    _mix_tail([slice(None)], [o_ref[...]], [p],
              lambda rs: (jax.nn.sigmoid(ga_ref[rs, :]), jax.nn.sigmoid(gb_ref[rs, :])),
              lambda rs: x_ref[rs, :],
              lambda rs: (gt1_ref[rs, :], sh2_ref[rs, :], sc2_ref[rs, :]),
              wua_ref, wup_ref, wo_ref, g2_ref, x1_ref, h2_ref)


def _mix_sample(o_attn, proj_s, hist, x_s, mod_s, w_up_attn, w_pool, pool_scale, w_up_pool, w_out, norm_g2):
    m = x_s.shape[0]

    def cols(width, c):
        return pl.BlockSpec((m, width), lambda i: (0, c))

    u0 = 3 * D_ATTN
    u = lax.slice_in_dim(proj_s, u0, u0 + D_POOL, axis=1)
    ga = lax.slice_in_dim(proj_s, u0 + D_POOL, u0 + D_POOL + D_MODEL, axis=1)
    gb = lax.slice_in_dim(proj_s, u0 + D_POOL + D_MODEL, D_IN, axis=1)
    in_specs = [cols(D_GROUP, 0), cols(D_POOL, 0), _const_spec((m, POOL_HIST, D_POOL)),
                cols(D_MODEL, 0), cols(D_MODEL, 0),
                cols(D_MODEL, 0), cols(D_MODEL, 2), cols(D_MODEL, 3), cols(D_MODEL, 4),
                _const_spec((D_GROUP, D_MODEL)), _const_spec((len(POOL_WINDOWS), POOL_GROUP, POOL_GROUP)),
                _const_spec((1, D_POOL)), _const_spec((D_POOL, D_MODEL)), _const_spec((D_MODEL, D_MODEL)),
                _const_spec((1, D_MODEL))]
    return pl.pallas_call(
        _mix_s_kernel,
        grid=(1,),
        in_specs=in_specs,
        out_specs=[_const_spec((m, D_MODEL)), _const_spec((m, D_MODEL)), _const_spec((m, POOL_HIST, D_POOL))],
        out_shape=[jax.ShapeDtypeStruct((m, D_MODEL), F32), jax.ShapeDtypeStruct((m, D_MODEL), BF16),
                   jax.ShapeDtypeStruct((m, POOL_HIST, D_POOL), F32)],
        compiler_params=_cparams("arbitrary"),
        name="mix_sample",
    )(o_attn, u, hist, ga, gb, x_s, mod_s, mod_s, mod_s, w_up_attn, w_pool, pool_scale, w_up_pool, w_out, norm_g2)


class _ShiftStream:
    def __init__(self, cache, new, out, buf, sems, rows, ns):
        self.cache, self.new, self.out, self.buf, self.sems = cache, new, out, buf, sems
        self.rows, self.ns = rows, ns
        self.slots = buf.shape[0]
        nb, w = cache.shape[1], cache.shape[2]
        assert w % rows == 0 and nb % ns == 0 and (ns == 1 or rows == w)
        self.cps = w // rows
        self.n_chunks = (nb // ns) * self.cps

    def _where(self, k):
        return (k // self.cps) * self.ns, (k % self.cps) * self.rows, k % self.slots

    def _body_in(self, k):
        b0, w0, slot = self._where(k)
        return pltpu.make_async_copy(self.cache.at[0, pl.ds(b0, self.ns), pl.ds(w0 + 1, self.rows - 1)],
                                     self.buf.at[slot, :, pl.ds(0, self.rows - 1)], self.sems.at[0, slot])

    def _next_in(self, k):
        b0, w0, slot = self._where(k)
        return pltpu.make_async_copy(self.cache.at[0, pl.ds(b0, self.ns), pl.ds(w0 + self.rows, 1)],
                                     self.buf.at[slot, :, pl.ds(self.rows - 1, 1)], self.sems.at[1, slot])

    def _new_in(self, k):
        b0, _, slot = self._where(k)
        return pltpu.make_async_copy(self.new.at[pl.ds(b0, self.ns)],
                                     self.buf.at[slot, :, self.rows - 1], self.sems.at[1, slot])

    def _out(self, k):
        b0, w0, slot = self._where(k)
        return pltpu.make_async_copy(self.buf.at[slot],
                                     self.out.at[0, pl.ds(b0, self.ns), pl.ds(w0, self.rows)], self.sems.at[2, slot])

    def _last_row(self, k, act):
        if isinstance(k, int) or self.cps == 1:
            at_end = self.cps == 1 or k % self.cps == self.cps - 1
            getattr(self._new_in(k) if at_end else self._next_in(k), act)()
            return
        at_end = k % self.cps == self.cps - 1

        @pl.when(at_end)
        def _():
            getattr(self._new_in(k), act)()

        @pl.when(jnp.logical_not(at_end))
        def _():
            getattr(self._next_in(k), act)()

    def start_in(self, k):
        self._body_in(k).start()
        self._last_row(k, "start")

    def wait_in(self, k):
        self._body_in(k).wait()
        self._last_row(k, "wait")

    def start_out(self, k):
        self._out(k).start()

    def wait_out(self, k):
        self._out(k).wait()

    def step(self, k, lag):
        self.wait_in(k)
        self.start_out(k)

        @pl.when(k - lag >= 0)
        def _():
            self.wait_out(k - lag)

        @pl.when(k - lag + self.slots < self.n_chunks)
        def _():
            self.start_in(k - lag + self.slots)

    def tick(self, s, n_steps):
        n = self.n_chunks
        per = -(-n // n_steps)
        stride = max(n_steps // n, 1)
        assert self.slots >= 2 * per and n >= self.slots

        @pl.when(s == 0)
        def _():
            for k in range(self.slots - per):
                self.start_in(k)

        for i in range(per):
            k = (s // stride) * per + i

            @pl.when((s % stride == 0) & (k < n))
            def _(k=k):
                self.step(k, per)

        @pl.when(s == n_steps - 1)
        def _():
            for k in range(n - per, n):
                self.wait_out(k)


class _Hosted(NamedTuple):
    cache: jax.Array
    new: jax.Array
    rows: int
    ns: int
    slots: int


def _host_streams(body, n_in, n_out, n_scr, hosted, n_steps, step_index):
    plans = [(h.rows, h.ns) for h in hosted]
    ns_ = len(plans)

    def kernel(*refs):
        refs = list(refs)
        ins, s_in = refs[:n_in], refs[n_in:n_in + 2 * ns_]
        o0 = n_in + 2 * ns_
        outs, s_out = refs[o0:o0 + n_out], refs[o0 + n_out:o0 + n_out + ns_]
        c0 = o0 + n_out + ns_
        scr, s_scr = refs[c0:c0 + n_scr], refs[c0 + n_scr:]
        s = step_index()
        for i, (rows, ns) in enumerate(plans):
            _ShiftStream(s_in[2 * i], s_in[2 * i + 1], s_out[i], s_scr[2 * i], s_scr[2 * i + 1],
                         rows, ns).tick(s, n_steps)
        body(*ins, *outs, *scr)

    return kernel


def _stream_operands(hosted):
    any_spec = pl.BlockSpec(memory_space=pl.ANY)
    args = [a for h in hosted for a in (h.cache, h.new)]
    out_shape = [jax.ShapeDtypeStruct(h.cache.shape, h.cache.dtype) for h in hosted]
    scratch = [s for h in hosted for s in (
        pltpu.VMEM((h.slots, h.ns, h.rows, 2, HEADS_PER_GROUP, HEAD_DIM), F32),
        pltpu.SemaphoreType.DMA((3, h.slots)))]
    return args, [any_spec] * len(args), out_shape, [any_spec] * len(hosted), scratch


def _alibi_slopes():
    h = jnp.arange(1, N_HEADS + 1, dtype=F32)
    return jnp.exp2(-ALIBI_MAX_BIAS * h / N_HEADS)


def _prompt_bias(g):
    dil = ATTN_GROUPS[g][1]
    slopes = _alibi_slopes()[g * HEADS_PER_GROUP:(g + 1) * HEADS_PER_GROUP]
    a = jnp.arange(BAND)[:, None]
    b = jnp.arange(2 * BAND)[None, :]
    dist = a - b + BAND
    valid = (dist >= 0) & (dist <= BAND)
    bias = -slopes[:, None, None] * (dist * dil).astype(F32)
    return jnp.where(valid[None], bias, NEG)


def _sample_bias():
    slopes = _alibi_slopes().reshape(N_GROUPS, 1, HEADS_PER_GROUP, 1)
    dil = jnp.array([d for _, d in ATTN_GROUPS], F32).reshape(N_GROUPS, 1, 1, 1)
    back = (BAND - jnp.arange(BAND, dtype=F32)).reshape(1, BAND, 1, 1)
    return jnp.broadcast_to(-slopes * (back * dil), (N_GROUPS, BAND, HEADS_PER_GROUP, HEAD_DIM))


def kernel(x_prompt, x_sample, c_prompt, c_sample, cache_kv_w128, cache_kv_w512, cache_kv_w2048, state_pool,
           norm_mix_g, w_ada, b_ada, w_in, w_up_attn, w_pool, pool_scale, w_up_pool, w_out, norm_mlp_g,
           w_mlp_up, w_mlp_down, norm_final_g):
    batch, seq, _ = x_prompt.shape
    nb = x_sample.shape[0]
    depth = w_in.shape[0]
    assert depth == 1 and x_sample.shape[1] == 1
    caches = (cache_kv_w128, cache_kv_w512, cache_kv_w2048)
    for c, (w, dil) in zip(caches, ATTN_GROUPS):
        assert c.shape[2] == w == BAND * dil

    w_ua_b = w_up_attn[0].astype(BF16)
    w_pool_b = w_pool[0].astype(BF16)
    w_up_b = w_up_pool[0].astype(BF16)
    w_out_b = w_out[0].astype(BF16)
    g1 = norm_mix_g[0].reshape(1, D_MODEL)
    g2 = norm_mlp_g[0].reshape(1, D_MODEL)
    gf = norm_final_g.reshape(1, D_MODEL)
    ps = pool_scale[0].reshape(1, D_POOL)

    mod = _ada(jnp.concatenate([c_sample, c_prompt], axis=0), w_ada[0], b_ada[0])
    mod_s = mod[:nb]
    mod_p = mod[nb:].reshape(batch, 1, 6 * D_MODEL)

    xs = x_sample.reshape(nb, D_MODEL)
    proj_s, w_in_b = _inproj_sample(xs, mod_s, g1, w_in[0])
    q3 = proj_s[:, 0:D_ATTN].reshape(nb, N_HEADS, HEAD_DIM)
    kn3 = proj_s[:, D_ATTN:2 * D_ATTN].reshape(nb, N_HEADS, HEAD_DIM)
    vn3 = proj_s[:, 2 * D_ATTN:3 * D_ATTN].reshape(nb, N_HEADS, HEAD_DIM)
    o_s = _attn_sample(_sample_bias(), q3, kn3, vn3, [c[0] for c in caches])
    new_rows = [jnp.stack([kn3[:, g * HEADS_PER_GROUP:(g + 1) * HEADS_PER_GROUP],
                           vn3[:, g * HEADS_PER_GROUP:(g + 1) * HEADS_PER_GROUP]], axis=1)
                for g in range(N_GROUPS)]
    x1_s, h2_s, pool_s = _mix_sample(o_s.reshape(nb, D_GROUP), proj_s, state_pool[0], xs, mod_s,
                                     w_ua_b, w_pool_b, ps, w_up_b, w_out_b, g2)
    y_s, w_mu_b, w_md_b = _mlp_sample(h2_s, x1_s, mod_s, w_mlp_up[0], w_mlp_down[0], gf)

    x2d = x_prompt.reshape(batch * seq, D_MODEL)
    *qkv_sub, u_p, gates, kv2, kv1, kv0, kv_s1 = _inproj_prompt(
        x2d, mod_p, g1, w_in_b, batch, seq,
        (_Hosted(caches[1], new_rows[1], rows=512, ns=1, slots=3),))
    outs, lses = [], []
    for g in range(N_GROUPS):
        o, l = _attn_prompt_group(qkv_sub[g], _prompt_bias(g), g, batch, seq)
        outs.append(o)
        lses.append(l)
    x1_p, h2_p, kv_s0 = _mix_prompt(
        outs, lses, u_p, gates, x2d, mod_p, w_ua_b, w_pool_b, ps, w_up_b, w_out_b, g2, seq,
        (_Hosted(caches[0], new_rows[0], rows=128, ns=4, slots=3),))
    y_p, kv_s2 = _mlp_prompt(
        h2_p, x1_p, mod_p, w_mu_b, w_md_b, gf, seq,
        (_Hosted(caches[2], new_rows[2], rows=1024, ns=1, slots=3),))

    pool_p = u_p.reshape(batch, seq, D_POOL)[:, seq - POOL_HIST:][None]
    kv0, kv1, kv2 = (a.reshape(1, batch, -1, 2, HEADS_PER_GROUP, HEAD_DIM) for a in (kv0, kv1, kv2))
    return (y_p.reshape(batch, seq, D_MODEL), y_s.reshape(nb, 1, D_MODEL), kv0, kv1, kv2, pool_p,
            kv_s0, kv_s1, kv_s2, pool_s[None])
```

```python
import functools
from typing import NamedTuple

import jax
import jax.numpy as jnp
from jax import lax
from jax.experimental import pallas as pl
from jax.experimental.pallas import tpu as pltpu

F32 = jnp.float32
BF16 = jnp.bfloat16

D_MODEL = 2048
HEAD_DIM = 128
HEADS_PER_GROUP = 4
ATTN_GROUPS = ((128, 1), (512, 4), (2048, 16))
N_GROUPS = len(ATTN_GROUPS)
N_HEADS = HEADS_PER_GROUP * N_GROUPS
D_ATTN = N_HEADS * HEAD_DIM
D_GROUP = HEADS_PER_GROUP * HEAD_DIM
BAND = 128
POOL_WINDOWS = (2, 4, 8, 16)
POOL_GROUP = 128
D_POOL = POOL_GROUP * len(POOL_WINDOWS)
POOL_HIST = max(POOL_WINDOWS) - 1
POOL_HALO = 16
D_FF = 4 * D_MODEL
D_IN = 3 * D_ATTN + D_POOL + 2 * D_MODEL
ALIBI_MAX_BIAS = 8.0
EPS = 1e-6
SCALE = HEAD_DIM ** -0.5
NEG = -1e30

VMEM_LIMIT_BYTES = 60 * 1024 * 1024

IN_TILE = D_GROUP
N_IN_TILES = D_IN // IN_TILE
QKV_TILES = 3 * N_GROUPS
POOL_TILE = QKV_TILES
GATE_TILE0 = POOL_TILE + 1

TM_IN = 1024
IN_SPLIT = 2
RELAY = 4
TM_MIX = 256
MIX_SPLIT = 2
TM_MLP = 1024
TF_MLP = 512
TF_MLP_SAMPLE = 1024
BS_ATTN = 8
ATTN_UNROLL = 3


def _cparams(*sem):
    return pltpu.CompilerParams(dimension_semantics=sem, vmem_limit_bytes=VMEM_LIMIT_BYTES)


def _rms(x):
    return x * lax.rsqrt(jnp.mean(x * x, axis=-1, keepdims=True) + EPS)


def _ada_kernel(c_ref, w_ref, b_ref, o_ref):
    c = c_ref[...]
    a = (c * jax.nn.sigmoid(c)).astype(BF16)
    o_ref[...] = jnp.dot(a, w_ref[...].astype(BF16), preferred_element_type=F32) + b_ref[...]


def _ada(c_all, w_ada, b_ada):
    m = c_all.shape[0]
    tn = 1024
    return pl.pallas_call(
        _ada_kernel,
        grid=(6 * D_MODEL // tn,),
        in_specs=[pl.BlockSpec((m, D_MODEL), lambda j: (0, 0)),
                  pl.BlockSpec((D_MODEL, tn), lambda j: (0, j)),
                  pl.BlockSpec((1, tn), lambda j: (0, j))],
        out_specs=pl.BlockSpec((m, tn), lambda j: (0, j)),
        out_shape=jax.ShapeDtypeStruct((m, 6 * D_MODEL), F32),
        compiler_params=_cparams("arbitrary"),
        name="ada",
    )(c_all, w_ada, b_ada.reshape(1, 6 * D_MODEL))


KV_ROWS = 2 * HEADS_PER_GROUP


def _store_heads(ref, row0, kv, val):
    for h in range(HEADS_PER_GROUP):
        start = row0 * KV_ROWS + kv * HEADS_PER_GROUP + h
        ref[pl.ds(start, val.shape[0], stride=KV_ROWS), :] = val[:, h * HEAD_DIM:(h + 1) * HEAD_DIM]


def _inproj_p_kernel(sh_ref, sc_ref, g_ref, w_ref, x_hbm,
                     qkv0_ref, qkv1_ref, qkv2_ref, u_ref, gate_ref, kv2_ref, kv1_ref, kv0_ref,
                     h_scr, acc_scr, mid_scr, x_scr, x_sem, *, tiles_per_batch):
    mi, j = pl.program_id(0), pl.program_id(1)
    tm = x_scr.shape[0]

    def x_copy(i):
        return pltpu.make_async_copy(x_hbm.at[pl.ds(i * tm, tm)], x_scr, x_sem.at[0])

    @pl.when(j == 0)
    def _():
        @pl.when(mi == 0)
        def _():
            x_copy(mi).start()

        x_copy(mi).wait()
        y = _rms(x_scr[...]) * g_ref[...]
        h_scr[...] = (y * (1.0 + sc_ref[...]) + sh_ref[...]).astype(BF16)

    @pl.when((j == 1) & (mi + 1 < pl.num_programs(0)))
    def _():
        x_copy(mi + 1).start()

    th = tm // IN_SPLIT

    def halves():
        for s in range(IN_SPLIT):
            yield s, jnp.dot(h_scr[s * th:(s + 1) * th, :], w_ref[...], preferred_element_type=F32)

    def store_sub(qkv_ref, dil, s, acc):
        if dil == 1:
            qkv_ref[0, s * th:(s + 1) * th, :] = acc.astype(BF16)
            return
        n = th // dil
        for h in range(HEADS_PER_GROUP):
            sl = slice(h * HEAD_DIM, (h + 1) * HEAD_DIM)
            acc_scr[s, h] = acc[:, sl]
            if dil == RELAY * RELAY:
                m = th // RELAY
                for q in range(RELAY):
                    mid_scr[s, h, q * m:(q + 1) * m, :] = acc_scr[s, h, pl.ds(q, m, stride=RELAY), :]
                for q in range(RELAY):
                    for p in range(RELAY):
                        qkv_ref[RELAY * p + q, s * n:(s + 1) * n, sl] = (
                            mid_scr[s, h, pl.ds(q * m + p, n, stride=RELAY), :].astype(BF16))
            else:
                for r in range(dil):
                    qkv_ref[r, s * n:(s + 1) * n, sl] = acc_scr[s, h, pl.ds(r, n, stride=dil), :].astype(BF16)

    last = pl.program_id(0) % tiles_per_batch == tiles_per_batch - 1
    kv_refs = (kv0_ref, kv1_ref, kv2_ref)
    for g, (qkv_ref, (_, dil)) in enumerate(zip((qkv0_ref, qkv1_ref, qkv2_ref), ATTN_GROUPS)):
        @pl.when(j == g)
        def _(qkv_ref=qkv_ref, dil=dil):
            for s, acc in halves():
                store_sub(qkv_ref, dil, s, acc)

        @pl.when((j == N_GROUPS + g) | (j == 2 * N_GROUPS + g))
        def _(g=g, qkv_ref=qkv_ref, dil=dil):
            kv_ref = kv_refs[g]
            first_kept = tm - kv_ref.shape[0] // KV_ROWS
            is_v = (j >= 2 * N_GROUPS).astype(jnp.int32)
            for s, acc in halves():
                store_sub(qkv_ref, dil, s, acc)
                row0 = max(s * th, first_kept)
                if first_kept == 0:
                    _store_heads(kv_ref, s * th, is_v, acc)
                elif row0 < (s + 1) * th:
                    @pl.when(last)
                    def _(acc=acc, row0=row0, s=s):
                        _store_heads(kv_ref, row0 - first_kept, is_v, acc[row0 - s * th:, :])

    @pl.when(j == POOL_TILE)
    def _():
        for s, acc in halves():
            u_ref[s * th:(s + 1) * th, :] = acc

    @pl.when(j >= GATE_TILE0)
    def _():
        for s, acc in halves():
            gate_ref[s * th:(s + 1) * th, :] = (0.5 * jnp.tanh(0.5 * acc) + 0.5).astype(BF16)


def _inproj_prompt(x2d, mod_p, norm_g, w_in, batch, seq, hosted):
    m = x2d.shape[0]
    tm = TM_IN
    tpb = seq // tm
    keep = [min(w, seq) for w, _ in ATTN_GROUPS]
    assert keep[2] == seq and tm >= keep[1] and seq % tm == 0

    def sub_spec(g):
        dil = ATTN_GROUPS[g][1]
        return pl.BlockSpec((None, None, dil, tm // dil, D_GROUP),
                            lambda mi, j: (jnp.clip((j - g) // N_GROUPS, 0, 2), mi // tpb, 0, mi % tpb, 0))

    def sub_shape(g):
        dil = ATTN_GROUPS[g][1]
        return jax.ShapeDtypeStruct((3, batch, dil, seq // dil, D_GROUP), BF16)

    def kv_spec(rows, tail_only):
        return pl.BlockSpec((None, rows * KV_ROWS, HEAD_DIM),
                            lambda mi, j: (mi // tpb, 0 if tail_only else mi % tpb, 0))

    def kv_shape(rows):
        return jax.ShapeDtypeStruct((batch, rows * KV_ROWS, HEAD_DIM), F32)

    in_specs = [
        pl.BlockSpec((None, 1, D_MODEL), lambda mi, j: (mi // tpb, 0, 0)),
        pl.BlockSpec((None, 1, D_MODEL), lambda mi, j: (mi // tpb, 0, 1)),
        pl.BlockSpec((1, D_MODEL), lambda mi, j: (0, 0)),
        pl.BlockSpec((D_MODEL, IN_TILE), lambda mi, j: (0, j)),
        pl.BlockSpec(memory_space=pl.ANY),
    ]
    out_specs = [
        sub_spec(0), sub_spec(1), sub_spec(2),
        pl.BlockSpec((tm, IN_TILE), lambda mi, j: (mi, 0)),
        pl.BlockSpec((tm, IN_TILE), lambda mi, j: (mi, jnp.clip(j - GATE_TILE0, 0, N_IN_TILES - GATE_TILE0 - 1))),
        kv_spec(tm, False), kv_spec(keep[1], True), kv_spec(keep[0], True),
    ]
    out_shape = [
        sub_shape(0), sub_shape(1), sub_shape(2),
        jax.ShapeDtypeStruct((m, D_POOL), F32),
        jax.ShapeDtypeStruct((m, 2 * D_MODEL), BF16),
        kv_shape(keep[2]), kv_shape(keep[1]), kv_shape(keep[0]),
    ]
    s_args, s_in_specs, s_out_shape, s_out_specs, s_scratch = _stream_operands(hosted)
    kernel = _host_streams(functools.partial(_inproj_p_kernel, tiles_per_batch=tpb), len(in_specs), len(out_specs),
                           5, hosted, (m // tm) * N_IN_TILES,
                           lambda: pl.program_id(0) * N_IN_TILES + pl.program_id(1))
    return pl.pallas_call(
        kernel,
        grid=(m // tm, N_IN_TILES),
        in_specs=in_specs + s_in_specs, out_specs=out_specs + s_out_specs, out_shape=out_shape + s_out_shape,
        scratch_shapes=[pltpu.VMEM((tm, D_MODEL), BF16),
                        pltpu.VMEM((IN_SPLIT, HEADS_PER_GROUP, tm // IN_SPLIT, HEAD_DIM), F32),
                        pltpu.VMEM((IN_SPLIT, HEADS_PER_GROUP, tm // IN_SPLIT, HEAD_DIM), F32),
                        pltpu.VMEM((tm, D_MODEL), F32), pltpu.SemaphoreType.DMA((1,))] + s_scratch,
        compiler_params=_cparams("arbitrary", "arbitrary"),
        name="inproj_prompt",
    )(mod_p, mod_p, norm_g, w_in, x2d, *s_args)


def _attn_block(q, k, v, bias_fn):
    lane = lax.broadcasted_iota(jnp.int32, (BAND, HEAD_DIM), 1)
    heads = [slice(h * HEAD_DIM, (h + 1) * HEAD_DIM) for h in range(HEADS_PER_GROUP)]
    scores = [lax.dot_general(q[:, sl], k[:, sl], (((1,), (1,)), ((), ())), preferred_element_type=F32)
              for sl in heads]
    scores = [s * SCALE + bias_fn(h) for h, s in enumerate(scores)]
    maxes = [jnp.max(s, axis=-1, keepdims=True) for s in scores]
    probs = [jnp.exp(s - mx) for s, mx in zip(scores, maxes)]
    dens = [jnp.sum(p, axis=-1, keepdims=True) for p in probs]
    outs = [jnp.dot(p.astype(BF16), v[:, sl], preferred_element_type=F32) / den
            for p, sl, den in zip(probs, heads, dens)]
    lse = jnp.zeros((BAND, HEAD_DIM), F32)
    for h, (mx, den) in enumerate(zip(maxes, dens)):
        lse = jnp.where(lane == h, mx + jnp.log(den), lse)
    return outs, lse


def _attn_p_kernel(bias_ref, q_ref, k_ref, v_ref, o_ref, l_ref, o_scr):
    dil, n, _ = q_ref.shape
    nb = n // BAND

    def rows(start):
        return pl.ds(start, BAND) if dil == 1 else pl.ds(start, BAND, stride=dil)

    def put(start, outs, lse):
        for h in range(HEADS_PER_GROUP):
            o_scr[h, rows(start), :] = outs[h]
        l_ref[rows(start), :] = lse

    for r in range(dil):
        put(r, *_attn_block(q_ref[r, 0:BAND, :], k_ref[r, 0:BAND, :], v_ref[r, 0:BAND, :],
                            lambda h: bias_ref[h, :, BAND:2 * BAND]))

        if nb > 1:
            def body(i, carry, r=r):
                r0 = pl.multiple_of(i * BAND, BAND)
                rk = pl.multiple_of((i - 1) * BAND, BAND)
                put(r0 * dil + r, *_attn_block(q_ref[r, pl.ds(r0, BAND), :], k_ref[r, pl.ds(rk, 2 * BAND), :],
                                               v_ref[r, pl.ds(rk, 2 * BAND), :], lambda h: bias_ref[h]))
                return carry

            lax.fori_loop(1, nb, body, 0, unroll=ATTN_UNROLL)

    for h in range(HEADS_PER_GROUP):
        o_ref[:, h * HEAD_DIM:(h + 1) * HEAD_DIM] = o_scr[h].astype(BF16)


def _attn_prompt_group(qkv_sub, bias, g, batch, seq):
    dil = ATTN_GROUPS[g][1]
    n = seq // dil

    def sub(which):
        return pl.BlockSpec((None, None, dil, n, D_GROUP), lambda b: (which, b, 0, 0, 0))

    return pl.pallas_call(
        _attn_p_kernel,
        grid=(batch,),
        in_specs=[pl.BlockSpec((HEADS_PER_GROUP, BAND, 2 * BAND), lambda b: (0, 0, 0)),
                  sub(0), sub(1), sub(2)],
        out_specs=[pl.BlockSpec((seq, D_GROUP), lambda b: (b, 0)),
                   pl.BlockSpec((seq, HEAD_DIM), lambda b: (b, 0))],
        out_shape=[jax.ShapeDtypeStruct((batch * seq, D_GROUP), BF16),
                   jax.ShapeDtypeStruct((batch * seq, HEAD_DIM), F32)],
        scratch_shapes=[pltpu.VMEM((HEADS_PER_GROUP, seq, HEAD_DIM), F32)],
        compiler_params=_cparams("arbitrary"),
        name=f"attn_prompt_g{g}",
    )(bias, qkv_sub, qkv_sub, qkv_sub)


def _merge_heads(outs, lses):
    cols = []
    for h in range(HEADS_PER_GROUP):
        sl = slice(h * HEAD_DIM, (h + 1) * HEAD_DIM)
        l = [lg[:, h:h + 1] for lg in lses]
        mx = jnp.maximum(jnp.maximum(l[0], l[1]), l[2])
        e = [jnp.exp(x - mx) for x in l]
        tot = e[0] + e[1] + e[2]
        cols.append(sum((e[g] / tot) * outs[g][:, sl].astype(F32) for g in range(N_GROUPS)))
    return jnp.concatenate(cols, axis=-1)


def _mix_tail(parts, o_attn, p, load_gates, load_x, load_mod, wua_ref, wup_ref, wo_ref, g2_ref, x1_ref, h2_ref):
    a = [jnp.dot(o.astype(BF16), wua_ref[...], preferred_element_type=F32) for o in o_attn]
    b = [jnp.dot(q.astype(BF16), wup_ref[...], preferred_element_type=F32) for q in p]
    mix = []
    for rs, ai, bi in zip(parts, a, b):
        ga, gb = load_gates(rs)
        mix.append((ga * ai + gb * bi).astype(BF16))
    y = [jnp.dot(m, wo_ref[...], preferred_element_type=F32) for m in mix]
    for rs, yi in zip(parts, y):
        gt1, sh2, sc2 = load_mod(rs)
        x1 = load_x(rs) + gt1 * yi
        x1_ref[rs, :] = x1
        h2_ref[rs, :] = (_rms(x1) * g2_ref[...] * (1.0 + sc2) + sh2).astype(BF16)


def _pool_project(z_groups, wp_ref, ps_ref):
    cols = [jnp.dot(z.astype(BF16), wp_ref[g], preferred_element_type=F32) for g, z in enumerate(z_groups)]
    return jnp.concatenate(cols, axis=-1) * ps_ref[...]


def _mix_p_kernel(o0_ref, o1_ref, o2_ref, l0_ref, l1_ref, l2_ref, u_ref, uh_ref, gate_ref, x_ref,
                  gt1_ref, sh2_ref, sc2_ref, wua_ref, wp_ref, ps_ref, wup_ref, wo_ref, g2_ref,
                  x1_ref, h2_ref, ext_scr, *, tiles_per_batch):
    tm = u_ref.shape[0]
    tp = tm // MIX_SPLIT
    t = pl.program_id(0) % tiles_per_batch
    parts = [slice(i * tp, (i + 1) * tp) for i in range(MIX_SPLIT)]
    o_attn = [_merge_heads([o0_ref[rs, :], o1_ref[rs, :], o2_ref[rs, :]], [l0_ref[rs, :], l1_ref[rs, :], l2_ref[rs, :]])
              for rs in parts]

    ext_scr[0:POOL_HALO, :] = jnp.where(t == 0, 0.0, uh_ref[...])
    ext_scr[POOL_HALO:, :] = u_ref[...]
    p = []
    for i, rs in enumerate(parts):
        pos = t * tm + i * tp + lax.broadcasted_iota(jnp.int32, (tp, 1), 0)
        zs = []
        for g, w in enumerate(POOL_WINDOWS):
            sl = slice(g * POOL_GROUP, (g + 1) * POOL_GROUP)
            u = u_ref[rs, sl]
            win = u
            for jj in range(1, w):
                r0 = POOL_HALO + i * tp - jj
                win = win + ext_scr[r0:r0 + tp, sl]
            cnt = jnp.minimum(pos + 1, w).astype(F32)
            zs.append(win / cnt - u)
        p.append(_pool_project(zs, wp_ref, ps_ref))

    _mix_tail(parts, o_attn, p,
              lambda rs: (gate_ref[rs, :D_MODEL].astype(F32), gate_ref[rs, D_MODEL:].astype(F32)),
              lambda rs: x_ref[rs, :],
              lambda rs: (gt1_ref[...], sh2_ref[...], sc2_ref[...]),
              wua_ref, wup_ref, wo_ref, g2_ref, x1_ref, h2_ref)


def _const_spec(shape):
    nd = len(shape)
    return pl.BlockSpec(shape, lambda *_: (0,) * nd)


def _mix_prompt(outs, lses, u, gates, x2d, mod_p, w_up_attn, w_pool, pool_scale, w_up_pool, w_out, norm_g2, seq,
                hosted):
    m = x2d.shape[0]
    tm = TM_MIX
    tpb = seq // tm
    hb = tm // POOL_HALO

    def row(width):
        return pl.BlockSpec((tm, width), lambda mi: (mi, 0))

    def mod(c):
        return pl.BlockSpec((None, 1, D_MODEL), lambda mi: (mi // tpb, 0, c))

    in_specs = ([row(D_GROUP)] * 3 + [row(HEAD_DIM)] * 3 + [
        row(D_POOL),
        pl.BlockSpec((POOL_HALO, D_POOL), lambda mi: (jnp.maximum(mi * hb - 1, 0), 0)),
        row(2 * D_MODEL), row(D_MODEL), mod(2), mod(3), mod(4),
        _const_spec((D_GROUP, D_MODEL)), _const_spec((len(POOL_WINDOWS), POOL_GROUP, POOL_GROUP)),
        _const_spec((1, D_POOL)), _const_spec((D_POOL, D_MODEL)), _const_spec((D_MODEL, D_MODEL)),
        _const_spec((1, D_MODEL))])
    s_args, s_in_specs, s_out_shape, s_out_specs, s_scratch = _stream_operands(hosted)
    kernel = _host_streams(functools.partial(_mix_p_kernel, tiles_per_batch=tpb), len(in_specs), 2, 1, hosted,
                           m // tm, lambda: pl.program_id(0))
    return pl.pallas_call(
        kernel,
        grid=(m // tm,),
        in_specs=in_specs + s_in_specs,
        out_specs=[row(D_MODEL), row(D_MODEL)] + s_out_specs,
        out_shape=[jax.ShapeDtypeStruct((m, D_MODEL), F32), jax.ShapeDtypeStruct((m, D_MODEL), BF16)] + s_out_shape,
        scratch_shapes=[pltpu.VMEM((POOL_HALO + tm, D_POOL), F32)] + s_scratch,
        compiler_params=_cparams("arbitrary"),
        name="mix_prompt",
    )(*outs, *lses, u, u, gates, x2d, mod_p, mod_p, mod_p, w_up_attn, w_pool, pool_scale, w_up_pool, w_out, norm_g2,
      *s_args)


def _mlp_kernel(h_ref, wu_ref, wd_ref, x1_ref, gt_ref, gf_ref, y_ref, acc_ref):
    f = pl.program_id(1)
    a = jnp.dot(h_ref[...], wu_ref[...], preferred_element_type=F32)
    a = jnp.square(jnp.maximum(a, 0.0)).astype(BF16)
    part = jnp.dot(a, wd_ref[...], preferred_element_type=F32)

    @pl.when(f == 0)
    def _():
        acc_ref[...] = part

    @pl.when(f > 0)
    def _():
        acc_ref[...] += part

    @pl.when(f == pl.num_programs(1) - 1)
    def _():
        x2 = x1_ref[...] + gt_ref[...] * acc_ref[...]
        y_ref[...] = _rms(x2) * gf_ref[...]


def _mlp_sample(h2, x1, mod_s, w_up, w_down, norm_gf):
    m = h2.shape[0]
    tf = TF_MLP_SAMPLE
    return pl.pallas_call(
        _mlp_kernel,
        grid=(1, D_FF // tf),
        in_specs=[pl.BlockSpec((m, D_MODEL), lambda mi, f: (0, 0)),
                  pl.BlockSpec((D_MODEL, tf), lambda mi, f: (0, f)),
                  pl.BlockSpec((tf, D_MODEL), lambda mi, f: (f, 0)),
                  pl.BlockSpec((m, D_MODEL), lambda mi, f: (0, 0)),
                  pl.BlockSpec((m, D_MODEL), lambda mi, f: (0, 5)),
                  pl.BlockSpec((1, D_MODEL), lambda mi, f: (0, 0))],
        out_specs=pl.BlockSpec((m, D_MODEL), lambda mi, f: (0, 0)),
        out_shape=jax.ShapeDtypeStruct((m, D_MODEL), F32),
        scratch_shapes=[pltpu.VMEM((m, D_MODEL), F32)],
        compiler_params=_cparams("arbitrary", "arbitrary"),
        name="mlp_sample",
    )(h2, w_up, w_down, x1, mod_s, norm_gf)


def _mlp_p_kernel(h_ref, wu_ref, wd_ref, gt_ref, gf_ref, x1_hbm, y_hbm, acc_ref, xy_ref, sems):
    mi, f = pl.program_id(0), pl.program_id(1)
    nm, nf = pl.num_programs(0), pl.num_programs(1)
    tm = acc_ref.shape[0]

    def x1_copy(i):
        return pltpu.make_async_copy(x1_hbm.at[pl.ds(i * tm, tm)], xy_ref, sems.at[0])

    def y_copy(i):
        return pltpu.make_async_copy(xy_ref, y_hbm.at[pl.ds(i * tm, tm)], sems.at[1])

    @pl.when(f == 1)
    def _():
        @pl.when(mi > 0)
        def _():
            y_copy(mi - 1).wait()

        x1_copy(mi).start()

    @pl.when(f == 0)
    def _():
        acc_ref[...] = jnp.zeros_like(acc_ref)

    a = jnp.dot(h_ref[...], wu_ref[...], preferred_element_type=F32)
    a = jnp.square(jnp.maximum(a, 0.0)).astype(BF16)
    acc_ref[...] += jnp.dot(a, wd_ref[...], preferred_element_type=F32)

    @pl.when(f == nf - 1)
    def _():
        x1_copy(mi).wait()
        x2 = xy_ref[...] + gt_ref[...] * acc_ref[...]
        xy_ref[...] = _rms(x2) * gf_ref[...]
        y_copy(mi).start()

        @pl.when(mi == nm - 1)
        def _():
            y_copy(mi).wait()


def _mlp_prompt(h2, x1, mod_p, w_up, w_down, norm_gf, seq, hosted):
    m = h2.shape[0]
    tm, tf = TM_MLP, TF_MLP
    tpb = seq // tm
    nm, nf = m // tm, D_FF // tf
    assert nf >= 3
    any_spec = pl.BlockSpec(memory_space=pl.ANY)
    s_args, s_in_specs, s_out_shape, s_out_specs, s_scratch = _stream_operands(hosted)
    kernel = _host_streams(_mlp_p_kernel, 6, 1, 3, hosted, nm * nf,
                           lambda: pl.program_id(0) * nf + pl.program_id(1))
    return pl.pallas_call(
        kernel,
        grid=(nm, nf),
        in_specs=[pl.BlockSpec((tm, D_MODEL), lambda mi, f: (mi, 0)),
                  pl.BlockSpec((D_MODEL, tf), lambda mi, f: (0, f)),
                  pl.BlockSpec((tf, D_MODEL), lambda mi, f: (f, 0)),
                  pl.BlockSpec((None, 1, D_MODEL), lambda mi, f: (mi // tpb, 0, 5)),
                  pl.BlockSpec((1, D_MODEL), lambda mi, f: (0, 0)),
                  any_spec] + s_in_specs,
        out_specs=[any_spec] + s_out_specs,
        out_shape=[jax.ShapeDtypeStruct((m, D_MODEL), F32)] + s_out_shape,
        scratch_shapes=[pltpu.VMEM((tm, D_MODEL), F32), pltpu.VMEM((tm, D_MODEL), F32),
                        pltpu.SemaphoreType.DMA((2,))] + s_scratch,
        compiler_params=_cparams("arbitrary", "arbitrary"),
        name="mlp_prompt",
    )(h2, w_up, w_down, mod_p, norm_gf, x1, *s_args)


def _inproj_s_kernel(x_ref, sh_ref, sc_ref, g_ref, w_ref, o_ref, wb_ref, h_scr):
    @pl.when(pl.program_id(0) == 0)
    def _():
        y = _rms(x_ref[...]) * g_ref[...]
        h_scr[...] = (y * (1.0 + sc_ref[...]) + sh_ref[...]).astype(BF16)

    w = w_ref[...].astype(BF16)
    wb_ref[...] = w
    o_ref[...] = jnp.dot(h_scr[...], w, preferred_element_type=F32)


def _inproj_sample(x_s, mod_s, norm_g, w_in):
    m = x_s.shape[0]
    tn = D_ATTN
    return pl.pallas_call(
        _inproj_s_kernel,
        grid=(D_IN // tn,),
        in_specs=[pl.BlockSpec((m, D_MODEL), lambda j: (0, 0)),
                  pl.BlockSpec((m, D_MODEL), lambda j: (0, 0)),
                  pl.BlockSpec((m, D_MODEL), lambda j: (0, 1)),
                  pl.BlockSpec((1, D_MODEL), lambda j: (0, 0)),
                  pl.BlockSpec((D_MODEL, tn), lambda j: (0, j))],
        out_specs=[pl.BlockSpec((m, tn), lambda j: (0, j)),
                   pl.BlockSpec((D_MODEL, tn), lambda j: (0, j))],
        out_shape=[jax.ShapeDtypeStruct((m, D_IN), F32), jax.ShapeDtypeStruct((D_MODEL, D_IN), BF16)],
        scratch_shapes=[pltpu.VMEM((m, D_MODEL), BF16)],
        compiler_params=_cparams("arbitrary"),
        name="inproj_sample",
    )(x_s, mod_s, mod_s, norm_g, w_in)


def _attn_s_kernel(bias_ref, q_ref, kn_ref, vn_ref, k0_ref, v0_ref, k1_ref, v1_ref, k2_ref, v2_ref,
                   wu_ref, wd_ref, o_ref, wub_ref, wdb_ref):
    k_refs = (k0_ref, k1_ref, k2_ref)
    v_refs = (v0_ref, v1_ref, v2_ref)
    wub_ref[...] = wu_ref[...].astype(BF16)
    wdb_ref[...] = wd_ref[...].astype(BF16)

    def body(s, carry):
        outs, lses = [], []
        for g in range(N_GROUPS):
            hs = slice(g * HEADS_PER_GROUP, (g + 1) * HEADS_PER_GROUP)
            q = q_ref[s, hs, :]
            kn = kn_ref[s, hs, :]
            vn = vn_ref[s, hs, :]
            kt = k_refs[g][s]
            vt = v_refs[g][s]
            sh = jnp.sum(kt * q[None], axis=-1, keepdims=True) * SCALE + bias_ref[g]
            sn = jnp.sum(kn * q, axis=-1, keepdims=True) * SCALE
            mx = jnp.maximum(jnp.max(sh, axis=0), sn)
            p = jnp.exp(sh - mx[None])
            pn = jnp.exp(sn - mx)
            den = jnp.sum(p, axis=0) + pn
            num = jnp.sum(p * vt, axis=0) + pn * vn
            outs.append(num / den)
            lses.append(mx + jnp.log(den))
        mx = jnp.maximum(jnp.maximum(lses[0], lses[1]), lses[2])
        e = [jnp.exp(l - mx) for l in lses]
        tot = e[0] + e[1] + e[2]
        o_ref[s] = (e[0] * outs[0] + e[1] * outs[1] + e[2] * outs[2]) / tot
        return carry

    lax.fori_loop(0, q_ref.shape[0], body, 0)


def _attn_sample(bias_s, q3, kn3, vn3, caches, w_up, w_down):
    nb = q3.shape[0]
    bs = BS_ATTN
    steps = nb // bs
    up_rows, down_rows = w_up.shape[0] // steps, w_down.shape[0] // steps
    up_spec = pl.BlockSpec((up_rows, w_up.shape[1]), lambda i: (i, 0))
    down_spec = pl.BlockSpec((down_rows, w_down.shape[1]), lambda i: (i, 0))
    tok = pl.BlockSpec((bs, N_HEADS, HEAD_DIM), lambda i: (i, 0, 0))
    in_specs = [_const_spec((N_GROUPS, BAND, HEADS_PER_GROUP, HEAD_DIM)), tok, tok, tok]
    args = [bias_s, q3, kn3, vn3]
    for c, (_, dil) in zip(caches, ATTN_GROUPS):
        cv = c.reshape(nb, BAND, dil * 2, HEADS_PER_GROUP, HEAD_DIM)
        for kv in range(2):
            in_specs.append(pl.BlockSpec((bs, BAND, None, HEADS_PER_GROUP, HEAD_DIM),
                                         lambda i, kv=kv: (i, 0, kv, 0, 0)))
            args.append(cv)
    return pl.pallas_call(
        _attn_s_kernel,
        grid=(steps,),
        in_specs=in_specs + [up_spec, down_spec],
        out_specs=[pl.BlockSpec((bs, HEADS_PER_GROUP, HEAD_DIM), lambda i: (i, 0, 0)), up_spec, down_spec],
        out_shape=[jax.ShapeDtypeStruct((nb, HEADS_PER_GROUP, HEAD_DIM), F32),
                   jax.ShapeDtypeStruct(w_up.shape, BF16), jax.ShapeDtypeStruct(w_down.shape, BF16)],
        compiler_params=_cparams("arbitrary"),
        name="attn_sample",
    )(*args, w_up, w_down)


def _mix_s_kernel(o_ref, u_ref, hist_ref, ga_ref, gb_ref, x_ref, gt1_ref, sh2_ref, sc2_ref,
                  wua_ref, wp_ref, ps_ref, wup_ref, wo_ref, g2_ref, x1_ref, h2_ref, np_ref):
    u = u_ref[...]
    zs = []
    for g, w in enumerate(POOL_WINDOWS):
        sl = slice(g * POOL_GROUP, (g + 1) * POOL_GROUP)
        win = u[:, sl]
        for jj in range(1, w):
            win = win + hist_ref[:, POOL_HIST - jj, sl]
        zs.append(win / float(w) - u[:, sl])
    p = _pool_project(zs, wp_ref, ps_ref)
    np_ref[:, 0:POOL_HIST - 1, :] = hist_ref[:, 1:POOL_HIST, :]
    np_ref[:, POOL_HIST - 1, :] = u
    _mix_tail([slice(None)], [o_ref[...]], [p],
              lambda rs: (jax.nn.sigmoid(ga_ref[rs, :]), jax.nn.sigmoid(gb_ref[rs, :])),
              lambda rs: x_ref[rs, :],
              lambda rs: (gt1_ref[rs, :], sh2_ref[rs, :], sc2_ref[rs, :]),
              wua_ref, wup_ref, wo_ref, g2_ref, x1_ref, h2_ref)


def _mix_sample(o_attn, proj_s, hist, x_s, mod_s, w_up_attn, w_pool, pool_scale, w_up_pool, w_out, norm_g2):
    m = x_s.shape[0]

    def cols(width, c):
        return pl.BlockSpec((m, width), lambda i: (0, c))

    u0 = 3 * D_ATTN
    u = lax.slice_in_dim(proj_s, u0, u0 + D_POOL, axis=1)
    ga = lax.slice_in_dim(proj_s, u0 + D_POOL, u0 + D_POOL + D_MODEL, axis=1)
    gb = lax.slice_in_dim(proj_s, u0 + D_POOL + D_MODEL, D_IN, axis=1)
    in_specs = [cols(D_GROUP, 0), cols(D_POOL, 0), _const_spec((m, POOL_HIST, D_POOL)),
                cols(D_MODEL, 0), cols(D_MODEL, 0),
                cols(D_MODEL, 0), cols(D_MODEL, 2), cols(D_MODEL, 3), cols(D_MODEL, 4),
                _const_spec((D_GROUP, D_MODEL)), _const_spec((len(POOL_WINDOWS), POOL_GROUP, POOL_GROUP)),
                _const_spec((1, D_POOL)), _const_spec((D_POOL, D_MODEL)), _const_spec((D_MODEL, D_MODEL)),
                _const_spec((1, D_MODEL))]
    return pl.pallas_call(
        _mix_s_kernel,
        grid=(1,),
        in_specs=in_specs,
        out_specs=[_const_spec((m, D_MODEL)), _const_spec((m, D_MODEL)), _const_spec((m, POOL_HIST, D_POOL))],
        out_shape=[jax.ShapeDtypeStruct((m, D_MODEL), F32), jax.ShapeDtypeStruct((m, D_MODEL), BF16),
                   jax.ShapeDtypeStruct((m, POOL_HIST, D_POOL), F32)],
        compiler_params=_cparams("arbitrary"),
        name="mix_sample",
    )(o_attn, u, hist, ga, gb, x_s, mod_s, mod_s, mod_s, w_up_attn, w_pool, pool_scale, w_up_pool, w_out, norm_g2)


class _ShiftStream:
    def __init__(self, cache, new, out, buf, sems, rows, ns):
        self.cache, self.new, self.out, self.buf, self.sems = cache, new, out, buf, sems
        self.rows, self.ns = rows, ns
        self.slots = buf.shape[0]
        nb, w = cache.shape[1], cache.shape[2]
        assert w % rows == 0 and nb % ns == 0 and (ns == 1 or rows == w)
        self.cps = w // rows
        self.n_chunks = (nb // ns) * self.cps

    def _where(self, k):
        return (k // self.cps) * self.ns, (k % self.cps) * self.rows, k % self.slots

    def _body_in(self, k):
        b0, w0, slot = self._where(k)
        return pltpu.make_async_copy(self.cache.at[0, pl.ds(b0, self.ns), pl.ds(w0 + 1, self.rows - 1)],
                                     self.buf.at[slot, :, pl.ds(0, self.rows - 1)], self.sems.at[0, slot])

    def _next_in(self, k):
        b0, w0, slot = self._where(k)
        return pltpu.make_async_copy(self.cache.at[0, pl.ds(b0, self.ns), pl.ds(w0 + self.rows, 1)],
                                     self.buf.at[slot, :, pl.ds(self.rows - 1, 1)], self.sems.at[1, slot])

    def _new_in(self, k):
        b0, _, slot = self._where(k)
        return pltpu.make_async_copy(self.new.at[pl.ds(b0, self.ns)],
                                     self.buf.at[slot, :, self.rows - 1], self.sems.at[1, slot])

    def _out(self, k):
        b0, w0, slot = self._where(k)
        return pltpu.make_async_copy(self.buf.at[slot],
                                     self.out.at[0, pl.ds(b0, self.ns), pl.ds(w0, self.rows)], self.sems.at[2, slot])

    def _last_row(self, k, act):
        if isinstance(k, int) or self.cps == 1:
            at_end = self.cps == 1 or k % self.cps == self.cps - 1
            getattr(self._new_in(k) if at_end else self._next_in(k), act)()
            return
        at_end = k % self.cps == self.cps - 1

        @pl.when(at_end)
        def _():
            getattr(self._new_in(k), act)()

        @pl.when(jnp.logical_not(at_end))
        def _():
            getattr(self._next_in(k), act)()

    def start_in(self, k):
        self._body_in(k).start()
        self._last_row(k, "start")

    def wait_in(self, k):
        self._body_in(k).wait()
        self._last_row(k, "wait")

    def start_out(self, k):
        self._out(k).start()

    def wait_out(self, k):
        self._out(k).wait()

    def step(self, k, lag):
        self.wait_in(k)
        self.start_out(k)

        @pl.when(k - lag >= 0)
        def _():
            self.wait_out(k - lag)

        @pl.when(k - lag + self.slots < self.n_chunks)
        def _():
            self.start_in(k - lag + self.slots)

    def tick(self, s, n_steps):
        n = self.n_chunks
        per = -(-n // n_steps)
        stride = max(n_steps // n, 1)
        assert self.slots >= 2 * per and n >= self.slots

        @pl.when(s == 0)
        def _():
            for k in range(self.slots - per):
                self.start_in(k)

        for i in range(per):
            k = (s // stride) * per + i

            @pl.when((s % stride == 0) & (k < n))
            def _(k=k):
                self.step(k, per)

        @pl.when(s == n_steps - 1)
        def _():
            for k in range(n - per, n):
                self.wait_out(k)


class _Hosted(NamedTuple):
    cache: jax.Array
    new: jax.Array
    rows: int
    ns: int
    slots: int


def _host_streams(body, n_in, n_out, n_scr, hosted, n_steps, step_index):
    plans = [(h.rows, h.ns) for h in hosted]
    ns_ = len(plans)

    def kernel(*refs):
        refs = list(refs)
        ins, s_in = refs[:n_in], refs[n_in:n_in + 2 * ns_]
        o0 = n_in + 2 * ns_
        outs, s_out = refs[o0:o0 + n_out], refs[o0 + n_out:o0 + n_out + ns_]
        c0 = o0 + n_out + ns_
        scr, s_scr = refs[c0:c0 + n_scr], refs[c0 + n_scr:]
        s = step_index()
        for i, (rows, ns) in enumerate(plans):
            _ShiftStream(s_in[2 * i], s_in[2 * i + 1], s_out[i], s_scr[2 * i], s_scr[2 * i + 1],
                         rows, ns).tick(s, n_steps)
        body(*ins, *outs, *scr)

    return kernel


def _stream_operands(hosted):
    any_spec = pl.BlockSpec(memory_space=pl.ANY)
    args = [a for h in hosted for a in (h.cache, h.new)]
    out_shape = [jax.ShapeDtypeStruct(h.cache.shape, h.cache.dtype) for h in hosted]
    scratch = [s for h in hosted for s in (
        pltpu.VMEM((h.slots, h.ns, h.rows, 2, HEADS_PER_GROUP, HEAD_DIM), F32),
        pltpu.SemaphoreType.DMA((3, h.slots)))]
    return args, [any_spec] * len(args), out_shape, [any_spec] * len(hosted), scratch


def _alibi_slopes():
    h = jnp.arange(1, N_HEADS + 1, dtype=F32)
    return jnp.exp2(-ALIBI_MAX_BIAS * h / N_HEADS)


def _prompt_bias(g):
    dil = ATTN_GROUPS[g][1]
    slopes = _alibi_slopes()[g * HEADS_PER_GROUP:(g + 1) * HEADS_PER_GROUP]
    a = jnp.arange(BAND)[:, None]
    b = jnp.arange(2 * BAND)[None, :]
    dist = a - b + BAND
    valid = (dist >= 0) & (dist <= BAND)
    bias = -slopes[:, None, None] * (dist * dil).astype(F32)
    return jnp.where(valid[None], bias, NEG)


def _sample_bias():
    slopes = _alibi_slopes().reshape(N_GROUPS, 1, HEADS_PER_GROUP, 1)
    dil = jnp.array([d for _, d in ATTN_GROUPS], F32).reshape(N_GROUPS, 1, 1, 1)
    back = (BAND - jnp.arange(BAND, dtype=F32)).reshape(1, BAND, 1, 1)
    return jnp.broadcast_to(-slopes * (back * dil), (N_GROUPS, BAND, HEADS_PER_GROUP, HEAD_DIM))


def kernel(x_prompt, x_sample, c_prompt, c_sample, cache_kv_w128, cache_kv_w512, cache_kv_w2048, state_pool,
           norm_mix_g, w_ada, b_ada, w_in, w_up_attn, w_pool, pool_scale, w_up_pool, w_out, norm_mlp_g,
           w_mlp_up, w_mlp_down, norm_final_g):
    batch, seq, _ = x_prompt.shape
    nb = x_sample.shape[0]
    depth = w_in.shape[0]
    assert depth == 1 and x_sample.shape[1] == 1
    caches = (cache_kv_w128, cache_kv_w512, cache_kv_w2048)
    for c, (w, dil) in zip(caches, ATTN_GROUPS):
        assert c.shape[2] == w == BAND * dil

    w_ua_b = w_up_attn[0].astype(BF16)
    w_pool_b = w_pool[0].astype(BF16)
    w_up_b = w_up_pool[0].astype(BF16)
    w_out_b = w_out[0].astype(BF16)
    g1 = norm_mix_g[0].reshape(1, D_MODEL)
    g2 = norm_mlp_g[0].reshape(1, D_MODEL)
    gf = norm_final_g.reshape(1, D_MODEL)
    ps = pool_scale[0].reshape(1, D_POOL)

    mod = _ada(jnp.concatenate([c_sample, c_prompt], axis=0), w_ada[0], b_ada[0])
    mod_s = mod[:nb]
    mod_p = mod[nb:].reshape(batch, 1, 6 * D_MODEL)

    xs = x_sample.reshape(nb, D_MODEL)
    proj_s, w_in_b = _inproj_sample(xs, mod_s, g1, w_in[0])
    q3 = proj_s[:, 0:D_ATTN].reshape(nb, N_HEADS, HEAD_DIM)
    kn3 = proj_s[:, D_ATTN:2 * D_ATTN].reshape(nb, N_HEADS, HEAD_DIM)
    vn3 = proj_s[:, 2 * D_ATTN:3 * D_ATTN].reshape(nb, N_HEADS, HEAD_DIM)
    o_s, w_mu_b, w_md_b = _attn_sample(_sample_bias(), q3, kn3, vn3, [c[0] for c in caches],
                                       w_mlp_up[0], w_mlp_down[0])
    new_rows = [jnp.stack([kn3[:, g * HEADS_PER_GROUP:(g + 1) * HEADS_PER_GROUP],
                           vn3[:, g * HEADS_PER_GROUP:(g + 1) * HEADS_PER_GROUP]], axis=1)
                for g in range(N_GROUPS)]
    x1_s, h2_s, pool_s = _mix_sample(o_s.reshape(nb, D_GROUP), proj_s, state_pool[0], xs, mod_s,
                                     w_ua_b, w_pool_b, ps, w_up_b, w_out_b, g2)
    y_s = _mlp_sample(h2_s, x1_s, mod_s, w_mu_b, w_md_b, gf)

    x2d = x_prompt.reshape(batch * seq, D_MODEL)
    *qkv_sub, u_p, gates, kv2, kv1, kv0, kv_s1 = _inproj_prompt(
        x2d, mod_p, g1, w_in_b, batch, seq,
        (_Hosted(caches[1], new_rows[1], rows=512, ns=1, slots=3),))
    outs, lses = [], []
    for g in range(N_GROUPS):
        o, l = _attn_prompt_group(qkv_sub[g], _prompt_bias(g), g, batch, seq)
        outs.append(o)
        lses.append(l)
    x1_p, h2_p, kv_s0 = _mix_prompt(
        outs, lses, u_p, gates, x2d, mod_p, w_ua_b, w_pool_b, ps, w_up_b, w_out_b, g2, seq,
        (_Hosted(caches[0], new_rows[0], rows=128, ns=4, slots=3),))
    y_p, kv_s2 = _mlp_prompt(
        h2_p, x1_p, mod_p, w_mu_b, w_md_b, gf, seq,
        (_Hosted(caches[2], new_rows[2], rows=1024, ns=1, slots=3),))

    pool_p = u_p.reshape(batch, seq, D_POOL)[:, seq - POOL_HIST:][None]
    kv0, kv1, kv2 = (a.reshape(1, batch, -1, 2, HEADS_PER_GROUP, HEAD_DIM) for a in (kv0, kv1, kv2))
    return (y_p.reshape(batch, seq, D_MODEL), y_s.reshape(nb, 1, D_MODEL), kv0, kv1, kv2, pool_p,
            kv_s0, kv_s1, kv_s2, pool_s[None])
```

```python
import functools
from typing import NamedTuple

import jax
import jax.numpy as jnp
from jax import lax
from jax.experimental import pallas as pl
from jax.experimental.pallas import tpu as pltpu

F32 = jnp.float32
BF16 = jnp.bfloat16

D_MODEL = 2048
HEAD_DIM = 128
HEADS_PER_GROUP = 4
ATTN_GROUPS = ((128, 1), (512, 4), (2048, 16))
N_GROUPS = len(ATTN_GROUPS)
N_HEADS = HEADS_PER_GROUP * N_GROUPS
D_ATTN = N_HEADS * HEAD_DIM
D_GROUP = HEADS_PER_GROUP * HEAD_DIM
BAND = 128
POOL_WINDOWS = (2, 4, 8, 16)
POOL_GROUP = 128
D_POOL = POOL_GROUP * len(POOL_WINDOWS)
POOL_HIST = max(POOL_WINDOWS) - 1
POOL_HALO = 16
D_FF = 4 * D_MODEL
D_IN = 3 * D_ATTN + D_POOL + 2 * D_MODEL
ALIBI_MAX_BIAS = 8.0
EPS = 1e-6
SCALE = HEAD_DIM ** -0.5
NEG = -1e30

VMEM_LIMIT_BYTES = 60 * 1024 * 1024

IN_TILE = D_GROUP
N_IN_TILES = D_IN // IN_TILE
QKV_TILES = 3 * N_GROUPS
POOL_TILE = QKV_TILES
GATE_TILE0 = POOL_TILE + 1

TM_IN = 1024
IN_SPLIT = 2
RELAY = 4
TM_MIX = 256
MIX_SPLIT = 2
TM_MLP = 1024
TF_MLP = 512
TF_MLP_SAMPLE = 1024
BS_ATTN = 8
ATTN_UNROLL = 3


def _cparams(*sem):
    return pltpu.CompilerParams(dimension_semantics=sem, vmem_limit_bytes=VMEM_LIMIT_BYTES)


def _rms(x):
    return x * lax.rsqrt(jnp.mean(x * x, axis=-1, keepdims=True) + EPS)


def _ada_kernel(c_ref, w_ref, b_ref, o_ref):
    c = c_ref[...]
    a = (c * jax.nn.sigmoid(c)).astype(BF16)
    o_ref[...] = jnp.dot(a, w_ref[...].astype(BF16), preferred_element_type=F32) + b_ref[...]


def _ada(c_all, w_ada, b_ada):
    m = c_all.shape[0]
    tn = 1024
    return pl.pallas_call(
        _ada_kernel,
        grid=(6 * D_MODEL // tn,),
        in_specs=[pl.BlockSpec((m, D_MODEL), lambda j: (0, 0)),
                  pl.BlockSpec((D_MODEL, tn), lambda j: (0, j)),
                  pl.BlockSpec((1, tn), lambda j: (0, j))],
        out_specs=pl.BlockSpec((m, tn), lambda j: (0, j)),
        out_shape=jax.ShapeDtypeStruct((m, 6 * D_MODEL), F32),
        compiler_params=_cparams("arbitrary"),
        name="ada",
    )(c_all, w_ada, b_ada.reshape(1, 6 * D_MODEL))


KV_ROWS = 2 * HEADS_PER_GROUP


def _store_heads(ref, row0, kv, val):
    for h in range(HEADS_PER_GROUP):
        start = row0 * KV_ROWS + kv * HEADS_PER_GROUP + h
        ref[pl.ds(start, val.shape[0], stride=KV_ROWS), :] = val[:, h * HEAD_DIM:(h + 1) * HEAD_DIM]


def _inproj_p_kernel(sh_ref, sc_ref, g_ref, w_ref, x_hbm,
                     qkv0_ref, qkv1_ref, qkv2_ref, u_ref, gate_ref, kv2_ref, kv1_ref, kv0_ref,
                     h_scr, acc_scr, mid_scr, x_scr, x_sem, *, tiles_per_batch):
    mi, j = pl.program_id(0), pl.program_id(1)
    tm = x_scr.shape[0]

    def x_copy(i):
        return pltpu.make_async_copy(x_hbm.at[pl.ds(i * tm, tm)], x_scr, x_sem.at[0])

    @pl.when(j == 0)
    def _():
        @pl.when(mi == 0)
        def _():
            x_copy(mi).start()

        x_copy(mi).wait()
        y = _rms(x_scr[...]) * g_ref[...]
        h_scr[...] = (y * (1.0 + sc_ref[...]) + sh_ref[...]).astype(BF16)

    @pl.when((j == 1) & (mi + 1 < pl.num_programs(0)))
    def _():
        x_copy(mi + 1).start()

    th = tm // IN_SPLIT

    def halves():
        for s in range(IN_SPLIT):
            yield s, jnp.dot(h_scr[s * th:(s + 1) * th, :], w_ref[...], preferred_element_type=F32)

    def store_sub(qkv_ref, dil, s, acc):
        if dil == 1:
            qkv_ref[0, s * th:(s + 1) * th, :] = acc.astype(BF16)
            return
        n = th // dil
        for h in range(HEADS_PER_GROUP):
            sl = slice(h * HEAD_DIM, (h + 1) * HEAD_DIM)
            acc_scr[s, h] = acc[:, sl]
            if dil == RELAY * RELAY:
                m = th // RELAY
                for q in range(RELAY):
                    mid_scr[s, h, q * m:(q + 1) * m, :] = acc_scr[s, h, pl.ds(q, m, stride=RELAY), :]
                for q in range(RELAY):
                    for p in range(RELAY):
                        qkv_ref[RELAY * p + q, s * n:(s + 1) * n, sl] = (
                            mid_scr[s, h, pl.ds(q * m + p, n, stride=RELAY), :].astype(BF16))
            else:
                for r in range(dil):
                    qkv_ref[r, s * n:(s + 1) * n, sl] = acc_scr[s, h, pl.ds(r, n, stride=dil), :].astype(BF16)

    last = pl.program_id(0) % tiles_per_batch == tiles_per_batch - 1
    kv_refs = (kv0_ref, kv1_ref, kv2_ref)
    for g, (qkv_ref, (_, dil)) in enumerate(zip((qkv0_ref, qkv1_ref, qkv2_ref), ATTN_GROUPS)):
        @pl.when(j == g)
        def _(qkv_ref=qkv_ref, dil=dil):
            for s, acc in halves():
                store_sub(qkv_ref, dil, s, acc)

        @pl.when((j == N_GROUPS + g) | (j == 2 * N_GROUPS + g))
        def _(g=g, qkv_ref=qkv_ref, dil=dil):
            kv_ref = kv_refs[g]
            first_kept = tm - kv_ref.shape[0] // KV_ROWS
            is_v = (j >= 2 * N_GROUPS).astype(jnp.int32)
            for s, acc in halves():
                store_sub(qkv_ref, dil, s, acc)
                row0 = max(s * th, first_kept)
                if first_kept == 0:
                    _store_heads(kv_ref, s * th, is_v, acc)
                elif row0 < (s + 1) * th:
                    @pl.when(last)
                    def _(acc=acc, row0=row0, s=s):
                        _store_heads(kv_ref, row0 - first_kept, is_v, acc[row0 - s * th:, :])

    @pl.when(j == POOL_TILE)
    def _():
        for s, acc in halves():
            u_ref[s * th:(s + 1) * th, :] = acc

    @pl.when(j >= GATE_TILE0)
    def _():
        for s, acc in halves():
            gate_ref[s * th:(s + 1) * th, :] = (0.5 * jnp.tanh(0.5 * acc) + 0.5).astype(BF16)


def _inproj_prompt(x2d, mod_p, norm_g, w_in, batch, seq, hosted):
    m = x2d.shape[0]
    tm = TM_IN
    tpb = seq // tm
    keep = [min(w, seq) for w, _ in ATTN_GROUPS]
    assert keep[2] == seq and tm >= keep[1] and seq % tm == 0

    def sub_spec(g):
        dil = ATTN_GROUPS[g][1]
        return pl.BlockSpec((None, None, dil, tm // dil, D_GROUP),
                            lambda mi, j: (jnp.clip((j - g) // N_GROUPS, 0, 2), mi // tpb, 0, mi % tpb, 0))

    def sub_shape(g):
        dil = ATTN_GROUPS[g][1]
        return jax.ShapeDtypeStruct((3, batch, dil, seq // dil, D_GROUP), BF16)

    def kv_spec(rows, tail_only):
        return pl.BlockSpec((None, rows * KV_ROWS, HEAD_DIM),
                            lambda mi, j: (mi // tpb, 0 if tail_only else mi % tpb, 0))

    def kv_shape(rows):
        return jax.ShapeDtypeStruct((batch, rows * KV_ROWS, HEAD_DIM), F32)

    in_specs = [
        pl.BlockSpec((None, 1, D_MODEL), lambda mi, j: (mi // tpb, 0, 0)),
        pl.BlockSpec((None, 1, D_MODEL), lambda mi, j: (mi // tpb, 0, 1)),
        pl.BlockSpec((1, D_MODEL), lambda mi, j: (0, 0)),
        pl.BlockSpec((D_MODEL, IN_TILE), lambda mi, j: (0, j)),
        pl.BlockSpec(memory_space=pl.ANY),
    ]
    out_specs = [
        sub_spec(0), sub_spec(1), sub_spec(2),
        pl.BlockSpec((tm, IN_TILE), lambda mi, j: (mi, 0)),
        pl.BlockSpec((tm, IN_TILE), lambda mi, j: (mi, jnp.clip(j - GATE_TILE0, 0, N_IN_TILES - GATE_TILE0 - 1))),
        kv_spec(tm, False), kv_spec(keep[1], True), kv_spec(keep[0], True),
    ]
    out_shape = [
        sub_shape(0), sub_shape(1), sub_shape(2),
        jax.ShapeDtypeStruct((m, D_POOL), F32),
        jax.ShapeDtypeStruct((m, 2 * D_MODEL), BF16),
        kv_shape(keep[2]), kv_shape(keep[1]), kv_shape(keep[0]),
    ]
    s_args, s_in_specs, s_out_shape, s_out_specs, s_scratch = _stream_operands(hosted)
    kernel = _host_streams(functools.partial(_inproj_p_kernel, tiles_per_batch=tpb), len(in_specs), len(out_specs),
                           5, hosted, (m // tm) * N_IN_TILES,
                           lambda: pl.program_id(0) * N_IN_TILES + pl.program_id(1))
    return pl.pallas_call(
        kernel,
        grid=(m // tm, N_IN_TILES),
        in_specs=in_specs + s_in_specs, out_specs=out_specs + s_out_specs, out_shape=out_shape + s_out_shape,
        scratch_shapes=[pltpu.VMEM((tm, D_MODEL), BF16),
                        pltpu.VMEM((IN_SPLIT, HEADS_PER_GROUP, tm // IN_SPLIT, HEAD_DIM), F32),
                        pltpu.VMEM((IN_SPLIT, HEADS_PER_GROUP, tm // IN_SPLIT, HEAD_DIM), F32),
                        pltpu.VMEM((tm, D_MODEL), F32), pltpu.SemaphoreType.DMA((1,))] + s_scratch,
        compiler_params=_cparams("arbitrary", "arbitrary"),
        name="inproj_prompt",
    )(mod_p, mod_p, norm_g, w_in, x2d, *s_args)


def _attn_block(q, k, v, bias_fn):
    lane = lax.broadcasted_iota(jnp.int32, (BAND, HEAD_DIM), 1)
    heads = [slice(h * HEAD_DIM, (h + 1) * HEAD_DIM) for h in range(HEADS_PER_GROUP)]
    scores = [lax.dot_general(q[:, sl], k[:, sl], (((1,), (1,)), ((), ())), preferred_element_type=F32)
              for sl in heads]
    scores = [s * SCALE + bias_fn(h) for h, s in enumerate(scores)]
    maxes = [jnp.max(s, axis=-1, keepdims=True) for s in scores]
    probs = [jnp.exp(s - mx) for s, mx in zip(scores, maxes)]
    dens = [jnp.sum(p, axis=-1, keepdims=True) for p in probs]
    outs = [jnp.dot(p.astype(BF16), v[:, sl], preferred_element_type=F32) / den
            for p, sl, den in zip(probs, heads, dens)]
    lse = jnp.zeros((BAND, HEAD_DIM), F32)
    for h, (mx, den) in enumerate(zip(maxes, dens)):
        lse = jnp.where(lane == h, mx + jnp.log(den), lse)
    return outs, lse


def _attn_p_kernel(bias_ref, q_ref, k_ref, v_ref, o_ref, l_ref, o_scr):
    dil, n, _ = q_ref.shape
    nb = n // BAND

    def rows(start):
        return pl.ds(start, BAND) if dil == 1 else pl.ds(start, BAND, stride=dil)

    def put(start, outs, lse):
        for h in range(HEADS_PER_GROUP):
            o_scr[h, rows(start), :] = outs[h]
        l_ref[rows(start), :] = lse

    for r in range(dil):
        put(r, *_attn_block(q_ref[r, 0:BAND, :], k_ref[r, 0:BAND, :], v_ref[r, 0:BAND, :],
                            lambda h: bias_ref[h, :, BAND:2 * BAND]))

        if nb > 1:
            def body(i, carry, r=r):
                r0 = pl.multiple_of(i * BAND, BAND)
                rk = pl.multiple_of((i - 1) * BAND, BAND)
                put(r0 * dil + r, *_attn_block(q_ref[r, pl.ds(r0, BAND), :], k_ref[r, pl.ds(rk, 2 * BAND), :],
                                               v_ref[r, pl.ds(rk, 2 * BAND), :], lambda h: bias_ref[h]))
                return carry

            lax.fori_loop(1, nb, body, 0, unroll=ATTN_UNROLL)

    for h in range(HEADS_PER_GROUP):
        o_ref[:, h * HEAD_DIM:(h + 1) * HEAD_DIM] = o_scr[h].astype(BF16)


def _with_cast_passenger(body, n_in, n_out):
    def kernel(*refs):
        ins, w_ref = refs[:n_in], refs[n_in]
        outs, wb_ref = refs[n_in + 1:n_in + 1 + n_out], refs[n_in + 1 + n_out]
        wb_ref[...] = w_ref[...].astype(BF16)
        body(*ins, *outs, *refs[n_in + 2 + n_out:])

    return kernel


def _attn_prompt_group(qkv_sub, bias, g, batch, seq, cast_weight=None):
    dil = ATTN_GROUPS[g][1]
    n = seq // dil

    def sub(which):
        return pl.BlockSpec((None, None, dil, n, D_GROUP), lambda b: (which, b, 0, 0, 0))

    kernel = _attn_p_kernel
    in_specs = [pl.BlockSpec((HEADS_PER_GROUP, BAND, 2 * BAND), lambda b: (0, 0, 0)), sub(0), sub(1), sub(2)]
    out_specs = [pl.BlockSpec((seq, D_GROUP), lambda b: (b, 0)), pl.BlockSpec((seq, HEAD_DIM), lambda b: (b, 0))]
    out_shape = [jax.ShapeDtypeStruct((batch * seq, D_GROUP), BF16),
                 jax.ShapeDtypeStruct((batch * seq, HEAD_DIM), F32)]
    args = [bias, qkv_sub, qkv_sub, qkv_sub]
    if cast_weight is not None:
        slab = pl.BlockSpec((cast_weight.shape[0] // batch, cast_weight.shape[1]), lambda b: (b, 0))
        kernel = _with_cast_passenger(kernel, len(in_specs), len(out_specs))
        in_specs.append(slab)
        out_specs.append(slab)
        out_shape.append(jax.ShapeDtypeStruct(cast_weight.shape, BF16))
        args.append(cast_weight)
    return pl.pallas_call(
        kernel,
        grid=(batch,),
        in_specs=in_specs, out_specs=out_specs, out_shape=out_shape,
        scratch_shapes=[pltpu.VMEM((HEADS_PER_GROUP, seq, HEAD_DIM), F32)],
        compiler_params=_cparams("arbitrary"),
        name=f"attn_prompt_g{g}",
    )(*args)


def _merge_heads(outs, lses):
    cols = []
    for h in range(HEADS_PER_GROUP):
        sl = slice(h * HEAD_DIM, (h + 1) * HEAD_DIM)
        l = [lg[:, h:h + 1] for lg in lses]
        mx = jnp.maximum(jnp.maximum(l[0], l[1]), l[2])
        e = [jnp.exp(x - mx) for x in l]
        tot = e[0] + e[1] + e[2]
        cols.append(sum((e[g] / tot) * outs[g][:, sl].astype(F32) for g in range(N_GROUPS)))
    return jnp.concatenate(cols, axis=-1)


def _mix_tail(parts, o_attn, p, load_gates, load_x, load_mod, wua_ref, wup_ref, wo_ref, g2_ref, x1_ref, h2_ref):
    a = [jnp.dot(o.astype(BF16), wua_ref[...], preferred_element_type=F32) for o in o_attn]
    b = [jnp.dot(q.astype(BF16), wup_ref[...], preferred_element_type=F32) for q in p]
    mix = []
    for rs, ai, bi in zip(parts, a, b):
        ga, gb = load_gates(rs)
        mix.append((ga * ai + gb * bi).astype(BF16))
    y = [jnp.dot(m, wo_ref[...], preferred_element_type=F32) for m in mix]
    for rs, yi in zip(parts, y):
        gt1, sh2, sc2 = load_mod(rs)
        x1 = load_x(rs) + gt1 * yi
        x1_ref[rs, :] = x1
        h2_ref[rs, :] = (_rms(x1) * g2_ref[...] * (1.0 + sc2) + sh2).astype(BF16)


def _pool_project(z_groups, wp_ref, ps_ref):
    cols = [jnp.dot(z.astype(BF16), wp_ref[g], preferred_element_type=F32) for g, z in enumerate(z_groups)]
    return jnp.concatenate(cols, axis=-1) * ps_ref[...]


def _mix_p_kernel(o0_ref, o1_ref, o2_ref, l0_ref, l1_ref, l2_ref, u_ref, uh_ref, gate_ref, x_ref,
                  gt1_ref, sh2_ref, sc2_ref, wua_ref, wp_ref, ps_ref, wup_ref, wo_ref, g2_ref,
                  x1_ref, h2_ref, ext_scr, *, tiles_per_batch):
    tm = u_ref.shape[0]
    tp = tm // MIX_SPLIT
    t = pl.program_id(0) % tiles_per_batch
    parts = [slice(i * tp, (i + 1) * tp) for i in range(MIX_SPLIT)]
    o_attn = [_merge_heads([o0_ref[rs, :], o1_ref[rs, :], o2_ref[rs, :]], [l0_ref[rs, :], l1_ref[rs, :], l2_ref[rs, :]])
              for rs in parts]

    ext_scr[0:POOL_HALO, :] = jnp.where(t == 0, 0.0, uh_ref[...])
    ext_scr[POOL_HALO:, :] = u_ref[...]
    p = []
    for i, rs in enumerate(parts):
        pos = t * tm + i * tp + lax.broadcasted_iota(jnp.int32, (tp, 1), 0)
        zs = []
        for g, w in enumerate(POOL_WINDOWS):
            sl = slice(g * POOL_GROUP, (g + 1) * POOL_GROUP)
            u = u_ref[rs, sl]
            win = u
            for jj in range(1, w):
                r0 = POOL_HALO + i * tp - jj
                win = win + ext_scr[r0:r0 + tp, sl]
            cnt = jnp.minimum(pos + 1, w).astype(F32)
            zs.append(win / cnt - u)
        p.append(_pool_project(zs, wp_ref, ps_ref))

    _mix_tail(parts, o_attn, p,
              lambda rs: (gate_ref[rs, :D_MODEL].astype(F32), gate_ref[rs, D_MODEL:].astype(F32)),
              lambda rs: x_ref[rs, :],
              lambda rs: (gt1_ref[...], sh2_ref[...], sc2_ref[...]),
              wua_ref, wup_ref, wo_ref, g2_ref, x1_ref, h2_ref)


def _const_spec(shape):
    nd = len(shape)
    return pl.BlockSpec(shape, lambda *_: (0,) * nd)


def _mix_prompt(outs, lses, u, gates, x2d, mod_p, w_up_attn, w_pool, pool_scale, w_up_pool, w_out, norm_g2, seq,
                hosted):
    m = x2d.shape[0]
    tm = TM_MIX
    tpb = seq // tm
    hb = tm // POOL_HALO

    def row(width):
        return pl.BlockSpec((tm, width), lambda mi: (mi, 0))

    def mod(c):
        return pl.BlockSpec((None, 1, D_MODEL), lambda mi: (mi // tpb, 0, c))

    in_specs = ([row(D_GROUP)] * 3 + [row(HEAD_DIM)] * 3 + [
        row(D_POOL),
        pl.BlockSpec((POOL_HALO, D_POOL), lambda mi: (jnp.maximum(mi * hb - 1, 0), 0)),
        row(2 * D_MODEL), row(D_MODEL), mod(2), mod(3), mod(4),
        _const_spec((D_GROUP, D_MODEL)), _const_spec((len(POOL_WINDOWS), POOL_GROUP, POOL_GROUP)),
        _const_spec((1, D_POOL)), _const_spec((D_POOL, D_MODEL)), _const_spec((D_MODEL, D_MODEL)),
        _const_spec((1, D_MODEL))])
    s_args, s_in_specs, s_out_shape, s_out_specs, s_scratch = _stream_operands(hosted)
    kernel = _host_streams(functools.partial(_mix_p_kernel, tiles_per_batch=tpb), len(in_specs), 2, 1, hosted,
                           m // tm, lambda: pl.program_id(0))
    return pl.pallas_call(
        kernel,
        grid=(m // tm,),
        in_specs=in_specs + s_in_specs,
        out_specs=[row(D_MODEL), row(D_MODEL)] + s_out_specs,
        out_shape=[jax.ShapeDtypeStruct((m, D_MODEL), F32), jax.ShapeDtypeStruct((m, D_MODEL), BF16)] + s_out_shape,
        scratch_shapes=[pltpu.VMEM((POOL_HALO + tm, D_POOL), F32)] + s_scratch,
        compiler_params=_cparams("arbitrary"),
        name="mix_prompt",
    )(*outs, *lses, u, u, gates, x2d, mod_p, mod_p, mod_p, w_up_attn, w_pool, pool_scale, w_up_pool, w_out, norm_g2,
      *s_args)


def _mlp_kernel(h_ref, wu_ref, wd_ref, x1_ref, gt_ref, gf_ref, y_ref, acc_ref):
    f = pl.program_id(1)
    a = jnp.dot(h_ref[...], wu_ref[...], preferred_element_type=F32)
    a = jnp.square(jnp.maximum(a, 0.0)).astype(BF16)
    part = jnp.dot(a, wd_ref[...], preferred_element_type=F32)

    @pl.when(f == 0)
    def _():
        acc_ref[...] = part

    @pl.when(f > 0)
    def _():
        acc_ref[...] += part

    @pl.when(f == pl.num_programs(1) - 1)
    def _():
        x2 = x1_ref[...] + gt_ref[...] * acc_ref[...]
        y_ref[...] = _rms(x2) * gf_ref[...]


def _mlp_sample(h2, x1, mod_s, w_up, w_down, norm_gf):
    m = h2.shape[0]
    tf = TF_MLP_SAMPLE
    return pl.pallas_call(
        _mlp_kernel,
        grid=(1, D_FF // tf),
        in_specs=[pl.BlockSpec((m, D_MODEL), lambda mi, f: (0, 0)),
                  pl.BlockSpec((D_MODEL, tf), lambda mi, f: (0, f)),
                  pl.BlockSpec((tf, D_MODEL), lambda mi, f: (f, 0)),
                  pl.BlockSpec((m, D_MODEL), lambda mi, f: (0, 0)),
                  pl.BlockSpec((m, D_MODEL), lambda mi, f: (0, 5)),
                  pl.BlockSpec((1, D_MODEL), lambda mi, f: (0, 0))],
        out_specs=pl.BlockSpec((m, D_MODEL), lambda mi, f: (0, 0)),
        out_shape=jax.ShapeDtypeStruct((m, D_MODEL), F32),
        scratch_shapes=[pltpu.VMEM((m, D_MODEL), F32)],
        compiler_params=_cparams("arbitrary", "arbitrary"),
        name="mlp_sample",
    )(h2, w_up, w_down, x1, mod_s, norm_gf)


def _mlp_p_kernel(h_ref, wu_ref, wd_ref, gt_ref, gf_ref, x1_hbm, y_hbm, acc_ref, xy_ref, sems):
    mi, f = pl.program_id(0), pl.program_id(1)
    nm, nf = pl.num_programs(0), pl.num_programs(1)
    tm = acc_ref.shape[0]

    def x1_copy(i):
        return pltpu.make_async_copy(x1_hbm.at[pl.ds(i * tm, tm)], xy_ref, sems.at[0])

    def y_copy(i):
        return pltpu.make_async_copy(xy_ref, y_hbm.at[pl.ds(i * tm, tm)], sems.at[1])

    @pl.when(f == 1)
    def _():
        @pl.when(mi > 0)
        def _():
            y_copy(mi - 1).wait()

        x1_copy(mi).start()

    @pl.when(f == 0)
    def _():
        acc_ref[...] = jnp.zeros_like(acc_ref)

    a = jnp.dot(h_ref[...], wu_ref[...], preferred_element_type=F32)
    a = jnp.square(jnp.maximum(a, 0.0)).astype(BF16)
    acc_ref[...] += jnp.dot(a, wd_ref[...], preferred_element_type=F32)

    @pl.when(f == nf - 1)
    def _():
        x1_copy(mi).wait()
        x2 = xy_ref[...] + gt_ref[...] * acc_ref[...]
        xy_ref[...] = _rms(x2) * gf_ref[...]
        y_copy(mi).start()

        @pl.when(mi == nm - 1)
        def _():
            y_copy(mi).wait()


def _mlp_prompt(h2, x1, mod_p, w_up, w_down, norm_gf, seq, hosted):
    m = h2.shape[0]
    tm, tf = TM_MLP, TF_MLP
    tpb = seq // tm
    nm, nf = m // tm, D_FF // tf
    assert nf >= 3
    any_spec = pl.BlockSpec(memory_space=pl.ANY)
    s_args, s_in_specs, s_out_shape, s_out_specs, s_scratch = _stream_operands(hosted)
    kernel = _host_streams(_mlp_p_kernel, 6, 1, 3, hosted, nm * nf,
                           lambda: pl.program_id(0) * nf + pl.program_id(1))
    return pl.pallas_call(
        kernel,
        grid=(nm, nf),
        in_specs=[pl.BlockSpec((tm, D_MODEL), lambda mi, f: (mi, 0)),
                  pl.BlockSpec((D_MODEL, tf), lambda mi, f: (0, f)),
                  pl.BlockSpec((tf, D_MODEL), lambda mi, f: (f, 0)),
                  pl.BlockSpec((None, 1, D_MODEL), lambda mi, f: (mi // tpb, 0, 5)),
                  pl.BlockSpec((1, D_MODEL), lambda mi, f: (0, 0)),
                  any_spec] + s_in_specs,
        out_specs=[any_spec] + s_out_specs,
        out_shape=[jax.ShapeDtypeStruct((m, D_MODEL), F32)] + s_out_shape,
        scratch_shapes=[pltpu.VMEM((tm, D_MODEL), F32), pltpu.VMEM((tm, D_MODEL), F32),
                        pltpu.SemaphoreType.DMA((2,))] + s_scratch,
        compiler_params=_cparams("arbitrary", "arbitrary"),
        name="mlp_prompt",
    )(h2, w_up, w_down, mod_p, norm_gf, x1, *s_args)


def _inproj_s_kernel(x_ref, sh_ref, sc_ref, g_ref, w_ref, o_ref, wb_ref, h_scr):
    @pl.when(pl.program_id(0) == 0)
    def _():
        y = _rms(x_ref[...]) * g_ref[...]
        h_scr[...] = (y * (1.0 + sc_ref[...]) + sh_ref[...]).astype(BF16)

    w = w_ref[...].astype(BF16)
    wb_ref[...] = w
    o_ref[...] = jnp.dot(h_scr[...], w, preferred_element_type=F32)


def _inproj_sample(x_s, mod_s, norm_g, w_in):
    m = x_s.shape[0]
    tn = D_ATTN
    return pl.pallas_call(
        _inproj_s_kernel,
        grid=(D_IN // tn,),
        in_specs=[pl.BlockSpec((m, D_MODEL), lambda j: (0, 0)),
                  pl.BlockSpec((m, D_MODEL), lambda j: (0, 0)),
                  pl.BlockSpec((m, D_MODEL), lambda j: (0, 1)),
                  pl.BlockSpec((1, D_MODEL), lambda j: (0, 0)),
                  pl.BlockSpec((D_MODEL, tn), lambda j: (0, j))],
        out_specs=[pl.BlockSpec((m, tn), lambda j: (0, j)),
                   pl.BlockSpec((D_MODEL, tn), lambda j: (0, j))],
        out_shape=[jax.ShapeDtypeStruct((m, D_IN), F32), jax.ShapeDtypeStruct((D_MODEL, D_IN), BF16)],
        scratch_shapes=[pltpu.VMEM((m, D_MODEL), BF16)],
        compiler_params=_cparams("arbitrary"),
        name="inproj_sample",
    )(x_s, mod_s, mod_s, norm_g, w_in)


def _attn_s_kernel(bias_ref, q_ref, kn_ref, vn_ref, k0_ref, v0_ref, k1_ref, v1_ref, k2_ref, v2_ref,
                   o_ref):
    k_refs = (k0_ref, k1_ref, k2_ref)
    v_refs = (v0_ref, v1_ref, v2_ref)

    bs, pairs = k0_ref.shape[0], k0_ref.shape[1]
    k_refs = [r.reshape(bs, pairs, KV_ROWS, HEAD_DIM) for r in k_refs]
    v_refs = [r.reshape(bs, pairs, KV_ROWS, HEAD_DIM) for r in v_refs]

    def both(x):
        return x, pltpu.roll(x, HEADS_PER_GROUP, axis=0)

    def body(s, carry):
        outs, lses = [], []
        for g in range(N_GROUPS):
            q = q_ref[s, g]
            kn = kn_ref[s, g]
            vn = vn_ref[s, g]
            kt = k_refs[g][s]
            vt = v_refs[g][s]
            sh = jnp.sum(kt * q[None], axis=-1, keepdims=True) * SCALE + bias_ref[g]
            sn = jnp.sum(kn * q, axis=-1, keepdims=True) * SCALE
            m_a, m_b = both(jnp.max(sh, axis=0))
            mx = jnp.maximum(jnp.maximum(m_a, m_b), sn)
            p = jnp.exp(sh - mx[None])
            pn = jnp.exp(sn - mx)
            d_a, d_b = both(jnp.sum(p, axis=0))
            n_a, n_b = both(jnp.sum(p * vt, axis=0))
            den = d_a + d_b + pn
            outs.append((n_a + n_b + pn * vn) / den)
            lses.append(mx + jnp.log(den))
        mx = jnp.maximum(jnp.maximum(lses[0], lses[1]), lses[2])
        e = [jnp.exp(l - mx) for l in lses]
        tot = e[0] + e[1] + e[2]
        o_ref[s] = ((e[0] * outs[0] + e[1] * outs[1] + e[2] * outs[2]) / tot)[0:HEADS_PER_GROUP, :]
        return carry

    lax.fori_loop(0, q_ref.shape[0], body, 0)


def _attn_sample(bias_s, q3, kn3, vn3, caches, cast_weight):
    nb = q3.shape[0]
    bs = BS_ATTN
    steps = nb // bs
    slab = pl.BlockSpec((cast_weight.shape[0] // steps, cast_weight.shape[1]), lambda i: (i, 0))
    def twice(t3):
        t = t3.reshape(nb, N_GROUPS, HEADS_PER_GROUP, HEAD_DIM)
        return jnp.concatenate([t, t], axis=2)

    tok = pl.BlockSpec((bs, N_GROUPS, KV_ROWS, HEAD_DIM), lambda i: (i, 0, 0, 0))
    in_specs = [_const_spec((N_GROUPS, BAND // 2, KV_ROWS, HEAD_DIM)), tok, tok, tok]
    args = [bias_s, twice(q3), twice(kn3), twice(vn3)]
    for c, (_, dil) in zip(caches, ATTN_GROUPS):
        cv = c.reshape(nb, BAND // 2, 2, dil * 2, HEADS_PER_GROUP, HEAD_DIM)
        for kv in range(2):
            in_specs.append(pl.BlockSpec((bs, BAND // 2, 2, None, HEADS_PER_GROUP, HEAD_DIM),
                                         lambda i, kv=kv: (i, 0, 0, kv, 0, 0)))
            args.append(cv)
    return pl.pallas_call(
        _with_cast_passenger(_attn_s_kernel, len(in_specs), 1),
        grid=(steps,),
        in_specs=in_specs + [slab],
        out_specs=[pl.BlockSpec((bs, HEADS_PER_GROUP, HEAD_DIM), lambda i: (i, 0, 0)), slab],
        out_shape=[jax.ShapeDtypeStruct((nb, HEADS_PER_GROUP, HEAD_DIM), F32),
                   jax.ShapeDtypeStruct(cast_weight.shape, BF16)],
        compiler_params=_cparams("arbitrary"),
        name="attn_sample",
    )(*args, cast_weight)


def _mix_s_kernel(o_ref, u_ref, hist_ref, ga_ref, gb_ref, x_ref, gt1_ref, sh2_ref, sc2_ref,
                  wua_ref, wp_ref, ps_ref, wup_ref, wo_ref, g2_ref, x1_ref, h2_ref, np_ref):
    u = u_ref[...]
    zs = []
    for g, w in enumerate(POOL_WINDOWS):
        sl = slice(g * POOL_GROUP, (g + 1) * POOL_GROUP)
        win = u[:, sl]
        for jj in range(1, w):
            win = win + hist_ref[:, POOL_HIST - jj, sl]
        zs.append(win / float(w) - u[:, sl])
    p = _pool_project(zs, wp_ref, ps_ref)
    np_ref[:, 0:POOL_HIST - 1, :] = hist_ref[:, 1:POOL_HIST, :]
    np_ref[:, POOL_HIST - 1, :] = u
    _mix_tail([slice(None)], [o_ref[...]], [p],
              lambda rs: (jax.nn.sigmoid(ga_ref[rs, :]), jax.nn.sigmoid(gb_ref[rs, :])),
              lambda rs: x_ref[rs, :],
              lambda rs: (gt1_ref[rs, :], sh2_ref[rs, :], sc2_ref[rs, :]),
              wua_ref, wup_ref, wo_ref, g2_ref, x1_ref, h2_ref)


def _mix_sample(o_attn, proj_s, hist, x_s, mod_s, w_up_attn, w_pool, pool_scale, w_up_pool, w_out, norm_g2):
    m = x_s.shape[0]

    def cols(width, c):
        return pl.BlockSpec((m, width), lambda i: (0, c))

    u0 = 3 * D_ATTN
    u = lax.slice_in_dim(proj_s, u0, u0 + D_POOL, axis=1)
    ga = lax.slice_in_dim(proj_s, u0 + D_POOL, u0 + D_POOL + D_MODEL, axis=1)
    gb = lax.slice_in_dim(proj_s, u0 + D_POOL + D_MODEL, D_IN, axis=1)
    in_specs = [cols(D_GROUP, 0), cols(D_POOL, 0), _const_spec((m, POOL_HIST, D_POOL)),
                cols(D_MODEL, 0), cols(D_MODEL, 0),
                cols(D_MODEL, 0), cols(D_MODEL, 2), cols(D_MODEL, 3), cols(D_MODEL, 4),
                _const_spec((D_GROUP, D_MODEL)), _const_spec((len(POOL_WINDOWS), POOL_GROUP, POOL_GROUP)),
                _const_spec((1, D_POOL)), _const_spec((D_POOL, D_MODEL)), _const_spec((D_MODEL, D_MODEL)),
                _const_spec((1, D_MODEL))]
    return pl.pallas_call(
        _mix_s_kernel,
        grid=(1,),
        in_specs=in_specs,
        out_specs=[_const_spec((m, D_MODEL)), _const_spec((m, D_MODEL)), _const_spec((m, POOL_HIST, D_POOL))],
        out_shape=[jax.ShapeDtypeStruct((m, D_MODEL), F32), jax.ShapeDtypeStruct((m, D_MODEL), BF16),
                   jax.ShapeDtypeStruct((m, POOL_HIST, D_POOL), F32)],
        compiler_params=_cparams("arbitrary"),
        name="mix_sample",
    )(o_attn, u, hist, ga, gb, x_s, mod_s, mod_s, mod_s, w_up_attn, w_pool, pool_scale, w_up_pool, w_out, norm_g2)


class _ShiftStream:
    def __init__(self, cache, new, out, buf, sems, rows, ns):
        self.cache, self.new, self.out, self.buf, self.sems = cache, new, out, buf, sems
        self.rows, self.ns = rows, ns
        self.slots = buf.shape[0]
        nb, w = cache.shape[1], cache.shape[2]
        assert w % rows == 0 and nb % ns == 0 and (ns == 1 or rows == w)
        self.cps = w // rows
        self.n_chunks = (nb // ns) * self.cps

    def _where(self, k):
        return (k // self.cps) * self.ns, (k % self.cps) * self.rows, k % self.slots

    def _body_in(self, k):
        b0, w0, slot = self._where(k)
        return pltpu.make_async_copy(self.cache.at[0, pl.ds(b0, self.ns), pl.ds(w0 + 1, self.rows - 1)],
                                     self.buf.at[slot, :, pl.ds(0, self.rows - 1)], self.sems.at[0, slot])

    def _next_in(self, k):
        b0, w0, slot = self._where(k)
        return pltpu.make_async_copy(self.cache.at[0, pl.ds(b0, self.ns), pl.ds(w0 + self.rows, 1)],
                                     self.buf.at[slot, :, pl.ds(self.rows - 1, 1)], self.sems.at[1, slot])

    def _new_in(self, k):
        b0, _, slot = self._where(k)
        return pltpu.make_async_copy(self.new.at[pl.ds(b0, self.ns)],
                                     self.buf.at[slot, :, self.rows - 1], self.sems.at[1, slot])

    def _out(self, k):
        b0, w0, slot = self._where(k)
        return pltpu.make_async_copy(self.buf.at[slot],
                                     self.out.at[0, pl.ds(b0, self.ns), pl.ds(w0, self.rows)], self.sems.at[2, slot])

    def _last_row(self, k, act):
        if isinstance(k, int) or self.cps == 1:
            at_end = self.cps == 1 or k % self.cps == self.cps - 1
            getattr(self._new_in(k) if at_end else self._next_in(k), act)()
            return
        at_end = k % self.cps == self.cps - 1

        @pl.when(at_end)
        def _():
            getattr(self._new_in(k), act)()

        @pl.when(jnp.logical_not(at_end))
        def _():
            getattr(self._next_in(k), act)()

    def start_in(self, k):
        self._body_in(k).start()
        self._last_row(k, "start")

    def wait_in(self, k):
        self._body_in(k).wait()
        self._last_row(k, "wait")

    def start_out(self, k):
        self._out(k).start()

    def wait_out(self, k):
        self._out(k).wait()

    def step(self, k, lag):
        self.wait_in(k)
        self.start_out(k)

        @pl.when(k - lag >= 0)
        def _():
            self.wait_out(k - lag)

        @pl.when(k - lag + self.slots < self.n_chunks)
        def _():
            self.start_in(k - lag + self.slots)

    def tick(self, s, n_steps):
        n = self.n_chunks
        per = -(-n // n_steps)
        stride = max(n_steps // n, 1)
        assert self.slots >= 2 * per and n >= self.slots

        @pl.when(s == 0)
        def _():
            for k in range(self.slots - per):
                self.start_in(k)

        for i in range(per):
            k = (s // stride) * per + i

            @pl.when((s % stride == 0) & (k < n))
            def _(k=k):
                self.step(k, per)

        @pl.when(s == n_steps - 1)
        def _():
            for k in range(n - per, n):
                self.wait_out(k)


class _Hosted(NamedTuple):
    cache: jax.Array
    new: jax.Array
    rows: int
    ns: int
    slots: int


def _host_streams(body, n_in, n_out, n_scr, hosted, n_steps, step_index):
    plans = [(h.rows, h.ns) for h in hosted]
    ns_ = len(plans)

    def kernel(*refs):
        refs = list(refs)
        ins, s_in = refs[:n_in], refs[n_in:n_in + 2 * ns_]
        o0 = n_in + 2 * ns_
        outs, s_out = refs[o0:o0 + n_out], refs[o0 + n_out:o0 + n_out + ns_]
        c0 = o0 + n_out + ns_
        scr, s_scr = refs[c0:c0 + n_scr], refs[c0 + n_scr:]
        s = step_index()
        for i, (rows, ns) in enumerate(plans):
            _ShiftStream(s_in[2 * i], s_in[2 * i + 1], s_out[i], s_scr[2 * i], s_scr[2 * i + 1],
                         rows, ns).tick(s, n_steps)
        body(*ins, *outs, *scr)

    return kernel


def _stream_operands(hosted):
    any_spec = pl.BlockSpec(memory_space=pl.ANY)
    args = [a for h in hosted for a in (h.cache, h.new)]
    out_shape = [jax.ShapeDtypeStruct(h.cache.shape, h.cache.dtype) for h in hosted]
    scratch = [s for h in hosted for s in (
        pltpu.VMEM((h.slots, h.ns, h.rows, 2, HEADS_PER_GROUP, HEAD_DIM), F32),
        pltpu.SemaphoreType.DMA((3, h.slots)))]
    return args, [any_spec] * len(args), out_shape, [any_spec] * len(hosted), scratch


def _alibi_slopes():
    h = jnp.arange(1, N_HEADS + 1, dtype=F32)
    return jnp.exp2(-ALIBI_MAX_BIAS * h / N_HEADS)


def _prompt_bias(g):
    dil = ATTN_GROUPS[g][1]
    slopes = _alibi_slopes()[g * HEADS_PER_GROUP:(g + 1) * HEADS_PER_GROUP]
    a = jnp.arange(BAND)[:, None]
    b = jnp.arange(2 * BAND)[None, :]
    dist = a - b + BAND
    valid = (dist >= 0) & (dist <= BAND)
    bias = -slopes[:, None, None] * (dist * dil).astype(F32)
    return jnp.where(valid[None], bias, NEG)


def _sample_bias():
    slopes = _alibi_slopes().reshape(N_GROUPS, 1, HEADS_PER_GROUP, 1)
    dil = jnp.array([d for _, d in ATTN_GROUPS], F32).reshape(N_GROUPS, 1, 1, 1)
    back = (BAND - jnp.arange(BAND, dtype=F32)).reshape(1, BAND, 1, 1)
    bias = jnp.broadcast_to(-slopes * (back * dil), (N_GROUPS, BAND, HEADS_PER_GROUP, HEAD_DIM))
    return bias.reshape(N_GROUPS, BAND // 2, KV_ROWS, HEAD_DIM)


def kernel(x_prompt, x_sample, c_prompt, c_sample, cache_kv_w128, cache_kv_w512, cache_kv_w2048, state_pool,
           norm_mix_g, w_ada, b_ada, w_in, w_up_attn, w_pool, pool_scale, w_up_pool, w_out, norm_mlp_g,
           w_mlp_up, w_mlp_down, norm_final_g):
    batch, seq, _ = x_prompt.shape
    nb = x_sample.shape[0]
    depth = w_in.shape[0]
    assert depth == 1 and x_sample.shape[1] == 1
    caches = (cache_kv_w128, cache_kv_w512, cache_kv_w2048)
    for c, (w, dil) in zip(caches, ATTN_GROUPS):
        assert c.shape[2] == w == BAND * dil

    w_ua_b = w_up_attn[0].astype(BF16)
    w_pool_b = w_pool[0].astype(BF16)
    w_up_b = w_up_pool[0].astype(BF16)
    w_out_b = w_out[0].astype(BF16)
    g1 = norm_mix_g[0].reshape(1, D_MODEL)
    g2 = norm_mlp_g[0].reshape(1, D_MODEL)
    gf = norm_final_g.reshape(1, D_MODEL)
    ps = pool_scale[0].reshape(1, D_POOL)

    mod = _ada(jnp.concatenate([c_sample, c_prompt], axis=0), w_ada[0], b_ada[0])
    mod_s = mod[:nb]
    mod_p = mod[nb:].reshape(batch, 1, 6 * D_MODEL)

    xs = x_sample.reshape(nb, D_MODEL)
    proj_s, w_in_b = _inproj_sample(xs, mod_s, g1, w_in[0])
    q3 = proj_s[:, 0:D_ATTN].reshape(nb, N_HEADS, HEAD_DIM)
    kn3 = proj_s[:, D_ATTN:2 * D_ATTN].reshape(nb, N_HEADS, HEAD_DIM)
    vn3 = proj_s[:, 2 * D_ATTN:3 * D_ATTN].reshape(nb, N_HEADS, HEAD_DIM)
    o_s, w_mu_b = _attn_sample(_sample_bias(), q3, kn3, vn3, [c[0] for c in caches], w_mlp_up[0])
    new_rows = [jnp.stack([kn3[:, g * HEADS_PER_GROUP:(g + 1) * HEADS_PER_GROUP],
                           vn3[:, g * HEADS_PER_GROUP:(g + 1) * HEADS_PER_GROUP]], axis=1)
                for g in range(N_GROUPS)]
    x1_s, h2_s, pool_s = _mix_sample(o_s.reshape(nb, D_GROUP), proj_s, state_pool[0], xs, mod_s,
                                     w_ua_b, w_pool_b, ps, w_up_b, w_out_b, g2)

    x2d = x_prompt.reshape(batch * seq, D_MODEL)
    *qkv_sub, u_p, gates, kv2, kv1, kv0, kv_s1 = _inproj_prompt(
        x2d, mod_p, g1, w_in_b, batch, seq,
        (_Hosted(caches[1], new_rows[1], rows=512, ns=1, slots=3),))
    o0, l0, w_md_b = _attn_prompt_group(qkv_sub[0], _prompt_bias(0), 0, batch, seq, cast_weight=w_mlp_down[0])
    outs, lses = [o0], [l0]
    for g in range(1, N_GROUPS):
        o, l = _attn_prompt_group(qkv_sub[g], _prompt_bias(g), g, batch, seq)
        outs.append(o)
        lses.append(l)
    y_s = _mlp_sample(h2_s, x1_s, mod_s, w_mu_b, w_md_b, gf)
    x1_p, h2_p, kv_s0 = _mix_prompt(
        outs, lses, u_p, gates, x2d, mod_p, w_ua_b, w_pool_b, ps, w_up_b, w_out_b, g2, seq,
        (_Hosted(caches[0], new_rows[0], rows=128, ns=4, slots=3),))
    y_p, kv_s2 = _mlp_prompt(
        h2_p, x1_p, mod_p, w_mu_b, w_md_b, gf, seq,
        (_Hosted(caches[2], new_rows[2], rows=1024, ns=1, slots=3),))

    pool_p = u_p.reshape(batch, seq, D_POOL)[:, seq - POOL_HIST:][None]
    kv0, kv1, kv2 = (a.reshape(1, batch, -1, 2, HEADS_PER_GROUP, HEAD_DIM) for a in (kv0, kv1, kv2))
    return (y_p.reshape(batch, seq, D_MODEL), y_s.reshape(nb, 1, D_MODEL), kv0, kv1, kv2, pool_p,
            kv_s0, kv_s1, kv_s2, pool_s[None])
```

```python
import functools
from typing import NamedTuple

import jax
import jax.numpy as jnp
from jax import lax
from jax.experimental import pallas as pl
from jax.experimental.pallas import tpu as pltpu

F32 = jnp.float32
BF16 = jnp.bfloat16

D_MODEL = 2048
HEAD_DIM = 128
HEADS_PER_GROUP = 4
ATTN_GROUPS = ((128, 1), (512, 4), (2048, 16))
N_GROUPS = len(ATTN_GROUPS)
N_HEADS = HEADS_PER_GROUP * N_GROUPS
D_ATTN = N_HEADS * HEAD_DIM
D_GROUP = HEADS_PER_GROUP * HEAD_DIM
BAND = 128
POOL_WINDOWS = (2, 4, 8, 16)
POOL_GROUP = 128
D_POOL = POOL_GROUP * len(POOL_WINDOWS)
POOL_HIST = max(POOL_WINDOWS) - 1
POOL_HALO = 16
D_FF = 4 * D_MODEL
D_IN = 3 * D_ATTN + D_POOL + 2 * D_MODEL
ALIBI_MAX_BIAS = 8.0
EPS = 1e-6
SCALE = HEAD_DIM ** -0.5
NEG = -1e30

VMEM_LIMIT_BYTES = 60 * 1024 * 1024

IN_TILE = D_GROUP
N_IN_TILES = D_IN // IN_TILE
QKV_TILES = 3 * N_GROUPS
POOL_TILE = QKV_TILES
GATE_TILE0 = POOL_TILE + 1

TM_IN = 1024
IN_SPLIT = 2
RELAY = 4
TM_MIX = 512
MIX_SPLIT = 2
TM_MLP = 1024
TF_MLP = 512
TF_MLP_SAMPLE = 1024
BS_ATTN = 8
ATTN_UNROLL = 3


def _cparams(*sem):
    return pltpu.CompilerParams(dimension_semantics=sem, vmem_limit_bytes=VMEM_LIMIT_BYTES)


def _rms(x):
    return x * lax.rsqrt(jnp.mean(x * x, axis=-1, keepdims=True) + EPS)


def _ada_kernel(c_ref, w_ref, b_ref, o_ref):
    c = c_ref[...]
    a = (c * jax.nn.sigmoid(c)).astype(BF16)
    o_ref[...] = jnp.dot(a, w_ref[...].astype(BF16), preferred_element_type=F32) + b_ref[...]


def _ada(c_all, w_ada, b_ada):
    m = c_all.shape[0]
    tn = 1024
    return pl.pallas_call(
        _ada_kernel,
        grid=(6 * D_MODEL // tn,),
        in_specs=[pl.BlockSpec((m, D_MODEL), lambda j: (0, 0)),
                  pl.BlockSpec((D_MODEL, tn), lambda j: (0, j)),
                  pl.BlockSpec((1, tn), lambda j: (0, j))],
        out_specs=pl.BlockSpec((m, tn), lambda j: (0, j)),
        out_shape=jax.ShapeDtypeStruct((m, 6 * D_MODEL), F32),
        compiler_params=_cparams("arbitrary"),
        name="ada",
    )(c_all, w_ada, b_ada.reshape(1, 6 * D_MODEL))


KV_ROWS = 2 * HEADS_PER_GROUP


def _store_heads(ref, row0, kv, val):
    for h in range(HEADS_PER_GROUP):
        start = row0 * KV_ROWS + kv * HEADS_PER_GROUP + h
        ref[pl.ds(start, val.shape[0], stride=KV_ROWS), :] = val[:, h * HEAD_DIM:(h + 1) * HEAD_DIM]


def _inproj_p_kernel(sh_ref, sc_ref, g_ref, w_ref, x_hbm,
                     qkv0_ref, qkv1_ref, qkv2_ref, u_ref, gate_ref, kv2_ref, kv1_ref, kv0_ref,
                     h_scr, acc_scr, mid_scr, x_scr, x_sem, *, tiles_per_batch):
    mi, j = pl.program_id(0), pl.program_id(1)
    tm = x_scr.shape[0]

    def x_copy(i):
        return pltpu.make_async_copy(x_hbm.at[pl.ds(i * tm, tm)], x_scr, x_sem.at[0])

    @pl.when(j == 0)
    def _():
        @pl.when(mi == 0)
        def _():
            x_copy(mi).start()

        x_copy(mi).wait()
        y = _rms(x_scr[...]) * g_ref[...]
        h_scr[...] = (y * (1.0 + sc_ref[...]) + sh_ref[...]).astype(BF16)

    @pl.when((j == 1) & (mi + 1 < pl.num_programs(0)))
    def _():
        x_copy(mi + 1).start()

    th = tm // IN_SPLIT

    def halves():
        for s in range(IN_SPLIT):
            yield s, jnp.dot(h_scr[s * th:(s + 1) * th, :], w_ref[...], preferred_element_type=F32)

    def store_sub(qkv_ref, dil, s, acc):
        if dil == 1:
            qkv_ref[0, s * th:(s + 1) * th, :] = acc.astype(BF16)
            return
        n = th // dil
        for h in range(HEADS_PER_GROUP):
            sl = slice(h * HEAD_DIM, (h + 1) * HEAD_DIM)
            acc_scr[s, h] = acc[:, sl]
            if dil == RELAY * RELAY:
                m = th // RELAY
                for q in range(RELAY):
                    mid_scr[s, h, q * m:(q + 1) * m, :] = acc_scr[s, h, pl.ds(q, m, stride=RELAY), :]
                for q in range(RELAY):
                    for p in range(RELAY):
                        qkv_ref[RELAY * p + q, s * n:(s + 1) * n, sl] = (
                            mid_scr[s, h, pl.ds(q * m + p, n, stride=RELAY), :].astype(BF16))
            else:
                for r in range(dil):
                    qkv_ref[r, s * n:(s + 1) * n, sl] = acc_scr[s, h, pl.ds(r, n, stride=dil), :].astype(BF16)

    last = pl.program_id(0) % tiles_per_batch == tiles_per_batch - 1
    kv_refs = (kv0_ref, kv1_ref, kv2_ref)
    for g, (qkv_ref, (_, dil)) in enumerate(zip((qkv0_ref, qkv1_ref, qkv2_ref), ATTN_GROUPS)):
        @pl.when(j == g)
        def _(qkv_ref=qkv_ref, dil=dil):
            for s, acc in halves():
                store_sub(qkv_ref, dil, s, acc)

        @pl.when((j == N_GROUPS + g) | (j == 2 * N_GROUPS + g))
        def _(g=g, qkv_ref=qkv_ref, dil=dil):
            kv_ref = kv_refs[g]
            first_kept = tm - kv_ref.shape[0] // KV_ROWS
            is_v = (j >= 2 * N_GROUPS).astype(jnp.int32)
            for s, acc in halves():
                store_sub(qkv_ref, dil, s, acc)
                row0 = max(s * th, first_kept)
                if first_kept == 0:
                    _store_heads(kv_ref, s * th, is_v, acc)
                elif row0 < (s + 1) * th:
                    @pl.when(last)
                    def _(acc=acc, row0=row0, s=s):
                        _store_heads(kv_ref, row0 - first_kept, is_v, acc[row0 - s * th:, :])

    @pl.when(j == POOL_TILE)
    def _():
        for s, acc in halves():
            u_ref[s * th:(s + 1) * th, :] = acc

    @pl.when(j >= GATE_TILE0)
    def _():
        for s, acc in halves():
            gate_ref[s * th:(s + 1) * th, :] = (0.5 * jnp.tanh(0.5 * acc) + 0.5).astype(BF16)


def _inproj_prompt(x2d, mod_p, norm_g, w_in, batch, seq, hosted):
    m = x2d.shape[0]
    tm = TM_IN
    tpb = seq // tm
    keep = [min(w, seq) for w, _ in ATTN_GROUPS]
    assert keep[2] == seq and tm >= keep[1] and seq % tm == 0

    def sub_spec(g):
        dil = ATTN_GROUPS[g][1]
        return pl.BlockSpec((None, None, dil, tm // dil, D_GROUP),
                            lambda mi, j: (jnp.clip((j - g) // N_GROUPS, 0, 2), mi // tpb, 0, mi % tpb, 0))

    def sub_shape(g):
        dil = ATTN_GROUPS[g][1]
        return jax.ShapeDtypeStruct((3, batch, dil, seq // dil, D_GROUP), BF16)

    def kv_spec(rows, tail_only):
        return pl.BlockSpec((None, rows * KV_ROWS, HEAD_DIM),
                            lambda mi, j: (mi // tpb, 0 if tail_only else mi % tpb, 0))

    def kv_shape(rows):
        return jax.ShapeDtypeStruct((batch, rows * KV_ROWS, HEAD_DIM), F32)

    in_specs = [
        pl.BlockSpec((None, 1, D_MODEL), lambda mi, j: (mi // tpb, 0, 0)),
        pl.BlockSpec((None, 1, D_MODEL), lambda mi, j: (mi // tpb, 0, 1)),
        pl.BlockSpec((1, D_MODEL), lambda mi, j: (0, 0)),
        pl.BlockSpec((D_MODEL, IN_TILE), lambda mi, j: (0, j)),
        pl.BlockSpec(memory_space=pl.ANY),
    ]
    out_specs = [
        sub_spec(0), sub_spec(1), sub_spec(2),
        pl.BlockSpec((tm, IN_TILE), lambda mi, j: (mi, 0)),
        pl.BlockSpec((tm, IN_TILE), lambda mi, j: (mi, jnp.clip(j - GATE_TILE0, 0, N_IN_TILES - GATE_TILE0 - 1))),
        kv_spec(tm, False), kv_spec(keep[1], True), kv_spec(keep[0], True),
    ]
    out_shape = [
        sub_shape(0), sub_shape(1), sub_shape(2),
        jax.ShapeDtypeStruct((m, D_POOL), F32),
        jax.ShapeDtypeStruct((m, 2 * D_MODEL), BF16),
        kv_shape(keep[2]), kv_shape(keep[1]), kv_shape(keep[0]),
    ]
    s_args, s_in_specs, s_out_shape, s_out_specs, s_scratch = _stream_operands(hosted)
    kernel = _host_streams(functools.partial(_inproj_p_kernel, tiles_per_batch=tpb), len(in_specs), len(out_specs),
                           5, hosted, (m // tm) * N_IN_TILES,
                           lambda: pl.program_id(0) * N_IN_TILES + pl.program_id(1))
    return pl.pallas_call(
        kernel,
        grid=(m // tm, N_IN_TILES),
        in_specs=in_specs + s_in_specs, out_specs=out_specs + s_out_specs, out_shape=out_shape + s_out_shape,
        scratch_shapes=[pltpu.VMEM((tm, D_MODEL), BF16),
                        pltpu.VMEM((IN_SPLIT, HEADS_PER_GROUP, tm // IN_SPLIT, HEAD_DIM), F32),
                        pltpu.VMEM((IN_SPLIT, HEADS_PER_GROUP, tm // IN_SPLIT, HEAD_DIM), F32),
                        pltpu.VMEM((tm, D_MODEL), F32), pltpu.SemaphoreType.DMA((1,))] + s_scratch,
        compiler_params=_cparams("arbitrary", "arbitrary"),
        name="inproj_prompt",
    )(mod_p, mod_p, norm_g, w_in, x2d, *s_args)


def _attn_block(q, k, v, bias_fn):
    lane = lax.broadcasted_iota(jnp.int32, (BAND, HEAD_DIM), 1)
    heads = [slice(h * HEAD_DIM, (h + 1) * HEAD_DIM) for h in range(HEADS_PER_GROUP)]
    scores = [lax.dot_general(q[:, sl], k[:, sl], (((1,), (1,)), ((), ())), preferred_element_type=F32)
              for sl in heads]
    scores = [s * SCALE + bias_fn(h) for h, s in enumerate(scores)]
    maxes = [jnp.max(s, axis=-1, keepdims=True) for s in scores]
    probs = [jnp.exp(s - mx) for s, mx in zip(scores, maxes)]
    dens = [jnp.sum(p, axis=-1, keepdims=True) for p in probs]
    outs = [jnp.dot(p.astype(BF16), v[:, sl], preferred_element_type=F32) / den
            for p, sl, den in zip(probs, heads, dens)]
    lse = jnp.zeros((BAND, HEAD_DIM), F32)
    for h, (mx, den) in enumerate(zip(maxes, dens)):
        lse = jnp.where(lane == h, mx + jnp.log(den), lse)
    return outs, lse


def _attn_p_kernel(bias_ref, q_ref, k_ref, v_ref, o_ref, l_ref, o_scr):
    dil, n, _ = q_ref.shape
    nb = n // BAND

    def rows(start):
        return pl.ds(start, BAND) if dil == 1 else pl.ds(start, BAND, stride=dil)

    def put(start, outs, lse):
        for h in range(HEADS_PER_GROUP):
            o_scr[h, rows(start), :] = outs[h]
        l_ref[rows(start), :] = lse

    for r in range(dil):
        put(r, *_attn_block(q_ref[r, 0:BAND, :], k_ref[r, 0:BAND, :], v_ref[r, 0:BAND, :],
                            lambda h: bias_ref[h, :, BAND:2 * BAND]))

        if nb > 1:
            def body(i, carry, r=r):
                r0 = pl.multiple_of(i * BAND, BAND)
                rk = pl.multiple_of((i - 1) * BAND, BAND)
                put(r0 * dil + r, *_attn_block(q_ref[r, pl.ds(r0, BAND), :], k_ref[r, pl.ds(rk, 2 * BAND), :],
                                               v_ref[r, pl.ds(rk, 2 * BAND), :], lambda h: bias_ref[h]))
                return carry

            lax.fori_loop(1, nb, body, 0, unroll=ATTN_UNROLL)

    for h in range(HEADS_PER_GROUP):
        o_ref[:, h * HEAD_DIM:(h + 1) * HEAD_DIM] = o_scr[h].astype(BF16)


def _with_cast_passenger(body, n_in, n_out):
    def kernel(*refs):
        ins, w_ref = refs[:n_in], refs[n_in]
        outs, wb_ref = refs[n_in + 1:n_in + 1 + n_out], refs[n_in + 1 + n_out]
        wb_ref[...] = w_ref[...].astype(BF16)
        body(*ins, *outs, *refs[n_in + 2 + n_out:])

    return kernel


def _attn_prompt_group(qkv_sub, bias, g, batch, seq, cast_weight=None, hosted=()):
    assert cast_weight is None or not hosted
    dil = ATTN_GROUPS[g][1]
    n = seq // dil

    def sub(which):
        return pl.BlockSpec((None, None, dil, n, D_GROUP), lambda b: (which, b, 0, 0, 0))

    kernel = _attn_p_kernel
    in_specs = [pl.BlockSpec((HEADS_PER_GROUP, BAND, 2 * BAND), lambda b: (0, 0, 0)), sub(0), sub(1), sub(2)]
    out_specs = [pl.BlockSpec((seq, D_GROUP), lambda b: (b, 0)), pl.BlockSpec((seq, HEAD_DIM), lambda b: (b, 0))]
    out_shape = [jax.ShapeDtypeStruct((batch * seq, D_GROUP), BF16),
                 jax.ShapeDtypeStruct((batch * seq, HEAD_DIM), F32)]
    args = [bias, qkv_sub, qkv_sub, qkv_sub]
    if cast_weight is not None:
        slab = pl.BlockSpec((cast_weight.shape[0] // batch, cast_weight.shape[1]), lambda b: (b, 0))
        kernel = _with_cast_passenger(kernel, len(in_specs), len(out_specs))
        in_specs.append(slab)
        out_specs.append(slab)
        out_shape.append(jax.ShapeDtypeStruct(cast_weight.shape, BF16))
        args.append(cast_weight)
    s_args, s_in_specs, s_out_shape, s_out_specs, s_scratch = _stream_operands(hosted)
    if hosted:
        kernel = _host_streams(kernel, len(in_specs), len(out_specs), 1, hosted, batch, lambda: pl.program_id(0))
    return pl.pallas_call(
        kernel,
        grid=(batch,),
        in_specs=in_specs + s_in_specs, out_specs=out_specs + s_out_specs, out_shape=out_shape + s_out_shape,
        scratch_shapes=[pltpu.VMEM((HEADS_PER_GROUP, seq, HEAD_DIM), F32)] + s_scratch,
        compiler_params=_cparams("arbitrary"),
        name=f"attn_prompt_g{g}",
    )(*args, *s_args)


def _merge_heads(outs, lses):
    cols = []
    for h in range(HEADS_PER_GROUP):
        sl = slice(h * HEAD_DIM, (h + 1) * HEAD_DIM)
        l = [lg[:, h:h + 1] for lg in lses]
        mx = jnp.maximum(jnp.maximum(l[0], l[1]), l[2])
        e = [jnp.exp(x - mx) for x in l]
        tot = e[0] + e[1] + e[2]
        cols.append(sum((e[g] / tot) * outs[g][:, sl].astype(F32) for g in range(N_GROUPS)))
    return jnp.concatenate(cols, axis=-1)


def _mix_tail(parts, o_attn, p, load_gates, load_x, load_mod, wua_ref, wup_ref, wo_ref, g2_ref, x1_ref, h2_ref):
    a = [jnp.dot(o.astype(BF16), wua_ref[...], preferred_element_type=F32) for o in o_attn]
    b = [jnp.dot(q.astype(BF16), wup_ref[...], preferred_element_type=F32) for q in p]
    mix = []
    for rs, ai, bi in zip(parts, a, b):
        ga, gb = load_gates(rs)
        mix.append((ga * ai + gb * bi).astype(BF16))
    y = [jnp.dot(m, wo_ref[...], preferred_element_type=F32) for m in mix]
    for rs, yi in zip(parts, y):
        gt1, sh2, sc2 = load_mod(rs)
        x1 = load_x(rs) + gt1 * yi
        x1_ref[rs, :] = x1
        h2_ref[rs, :] = (_rms(x1) * g2_ref[...] * (1.0 + sc2) + sh2).astype(BF16)


def _pool_project(z_groups, wp_ref, ps_ref):
    cols = [jnp.dot(z.astype(BF16), wp_ref[g], preferred_element_type=F32) for g, z in enumerate(z_groups)]
    return jnp.concatenate(cols, axis=-1) * ps_ref[...]


def _mix_p_kernel(o0_ref, o1_ref, o2_ref, l0_ref, l1_ref, l2_ref, u_ref, uh_ref, gate_ref, x_ref,
                  gt1_ref, sh2_ref, sc2_ref, wua_ref, wp_ref, ps_ref, wup_ref, wo_ref, g2_ref,
                  x1_ref, h2_ref, ext_scr, *, tiles_per_batch):
    tm = u_ref.shape[0]
    tp = tm // MIX_SPLIT
    t = pl.program_id(0) % tiles_per_batch
    parts = [slice(i * tp, (i + 1) * tp) for i in range(MIX_SPLIT)]
    o_attn = [_merge_heads([o0_ref[rs, :], o1_ref[rs, :], o2_ref[rs, :]], [l0_ref[rs, :], l1_ref[rs, :], l2_ref[rs, :]])
              for rs in parts]

    ext_scr[0:POOL_HALO, :] = jnp.where(t == 0, 0.0, uh_ref[...])
    ext_scr[POOL_HALO:, :] = u_ref[...]
    p = []
    for i, rs in enumerate(parts):
        pos = t * tm + i * tp + lax.broadcasted_iota(jnp.int32, (tp, 1), 0)
        zs = []
        for g, w in enumerate(POOL_WINDOWS):
            sl = slice(g * POOL_GROUP, (g + 1) * POOL_GROUP)
            u = u_ref[rs, sl]
            win = u
            for jj in range(1, w):
                r0 = POOL_HALO + i * tp - jj
                win = win + ext_scr[r0:r0 + tp, sl]
            cnt = jnp.minimum(pos + 1, w).astype(F32)
            zs.append(win / cnt - u)
        p.append(_pool_project(zs, wp_ref, ps_ref))

    _mix_tail(parts, o_attn, p,
              lambda rs: (gate_ref[rs, :D_MODEL].astype(F32), gate_ref[rs, D_MODEL:].astype(F32)),
              lambda rs: x_ref[rs, :],
              lambda rs: (gt1_ref[...], sh2_ref[...], sc2_ref[...]),
              wua_ref, wup_ref, wo_ref, g2_ref, x1_ref, h2_ref)


def _const_spec(shape):
    nd = len(shape)
    return pl.BlockSpec(shape, lambda *_: (0,) * nd)


def _mix_prompt(outs, lses, u, gates, x2d, mod_p, w_up_attn, w_pool, pool_scale, w_up_pool, w_out, norm_g2, seq,
                hosted):
    m = x2d.shape[0]
    tm = TM_MIX
    tpb = seq // tm
    hb = tm // POOL_HALO

    def row(width):
        return pl.BlockSpec((tm, width), lambda mi: (mi, 0))

    def mod(c):
        return pl.BlockSpec((None, 1, D_MODEL), lambda mi: (mi // tpb, 0, c))

    in_specs = ([row(D_GROUP)] * 3 + [row(HEAD_DIM)] * 3 + [
        row(D_POOL),
        pl.BlockSpec((POOL_HALO, D_POOL), lambda mi: (jnp.maximum(mi * hb - 1, 0), 0)),
        row(2 * D_MODEL), row(D_MODEL), mod(2), mod(3), mod(4),
        _const_spec((D_GROUP, D_MODEL)), _const_spec((len(POOL_WINDOWS), POOL_GROUP, POOL_GROUP)),
        _const_spec((1, D_POOL)), _const_spec((D_POOL, D_MODEL)), _const_spec((D_MODEL, D_MODEL)),
        _const_spec((1, D_MODEL))])
    s_args, s_in_specs, s_out_shape, s_out_specs, s_scratch = _stream_operands(hosted)
    kernel = _host_streams(functools.partial(_mix_p_kernel, tiles_per_batch=tpb), len(in_specs), 2, 1, hosted,
                           m // tm, lambda: pl.program_id(0))
    return pl.pallas_call(
        kernel,
        grid=(m // tm,),
        in_specs=in_specs + s_in_specs,
        out_specs=[row(D_MODEL), row(D_MODEL)] + s_out_specs,
        out_shape=[jax.ShapeDtypeStruct((m, D_MODEL), F32), jax.ShapeDtypeStruct((m, D_MODEL), BF16)] + s_out_shape,
        scratch_shapes=[pltpu.VMEM((POOL_HALO + tm, D_POOL), F32)] + s_scratch,
        compiler_params=_cparams("arbitrary"),
        name="mix_prompt",
    )(*outs, *lses, u, u, gates, x2d, mod_p, mod_p, mod_p, w_up_attn, w_pool, pool_scale, w_up_pool, w_out, norm_g2,
      *s_args)


def _mlp_kernel(h_ref, wu_ref, wd_ref, x1_ref, gt_ref, gf_ref, y_ref, acc_ref):
    f = pl.program_id(1)
    a = jnp.dot(h_ref[...], wu_ref[...], preferred_element_type=F32)
    a = jnp.square(jnp.maximum(a, 0.0)).astype(BF16)
    part = jnp.dot(a, wd_ref[...], preferred_element_type=F32)

    @pl.when(f == 0)
    def _():
        acc_ref[...] = part

    @pl.when(f > 0)
    def _():
        acc_ref[...] += part

    @pl.when(f == pl.num_programs(1) - 1)
    def _():
        x2 = x1_ref[...] + gt_ref[...] * acc_ref[...]
        y_ref[...] = _rms(x2) * gf_ref[...]


def _mlp_sample(h2, x1, mod_s, w_up, w_down, norm_gf):
    m = h2.shape[0]
    tf = TF_MLP_SAMPLE
    return pl.pallas_call(
        _mlp_kernel,
        grid=(1, D_FF // tf),
        in_specs=[pl.BlockSpec((m, D_MODEL), lambda mi, f: (0, 0)),
                  pl.BlockSpec((D_MODEL, tf), lambda mi, f: (0, f)),
                  pl.BlockSpec((tf, D_MODEL), lambda mi, f: (f, 0)),
                  pl.BlockSpec((m, D_MODEL), lambda mi, f: (0, 0)),
                  pl.BlockSpec((m, D_MODEL), lambda mi, f: (0, 5)),
                  pl.BlockSpec((1, D_MODEL), lambda mi, f: (0, 0))],
        out_specs=pl.BlockSpec((m, D_MODEL), lambda mi, f: (0, 0)),
        out_shape=jax.ShapeDtypeStruct((m, D_MODEL), F32),
        scratch_shapes=[pltpu.VMEM((m, D_MODEL), F32)],
        compiler_params=_cparams("arbitrary", "arbitrary"),
        name="mlp_sample",
    )(h2, w_up, w_down, x1, mod_s, norm_gf)


def _mlp_p_kernel(h_ref, wu_ref, wd_ref, gt_ref, gf_ref, x1_hbm, y_hbm, acc_ref, xy_ref, sems):
    mi, f = pl.program_id(0), pl.program_id(1)
    nm, nf = pl.num_programs(0), pl.num_programs(1)
    tm = acc_ref.shape[0]

    def x1_copy(i):
        return pltpu.make_async_copy(x1_hbm.at[pl.ds(i * tm, tm)], xy_ref, sems.at[0])

    def y_copy(i):
        return pltpu.make_async_copy(xy_ref, y_hbm.at[pl.ds(i * tm, tm)], sems.at[1])

    @pl.when(f == 1)
    def _():
        @pl.when(mi > 0)
        def _():
            y_copy(mi - 1).wait()

        x1_copy(mi).start()

    @pl.when(f == 0)
    def _():
        acc_ref[...] = jnp.zeros_like(acc_ref)

    a = jnp.dot(h_ref[...], wu_ref[...], preferred_element_type=F32)
    a = jnp.square(jnp.maximum(a, 0.0)).astype(BF16)
    acc_ref[...] += jnp.dot(a, wd_ref[...], preferred_element_type=F32)

    @pl.when(f == nf - 1)
    def _():
        x1_copy(mi).wait()
        x2 = xy_ref[...] + gt_ref[...] * acc_ref[...]
        xy_ref[...] = _rms(x2) * gf_ref[...]
        y_copy(mi).start()

        @pl.when(mi == nm - 1)
        def _():
            y_copy(mi).wait()


def _mlp_prompt(h2, x1, mod_p, w_up, w_down, norm_gf, seq, hosted):
    m = h2.shape[0]
    tm, tf = TM_MLP, TF_MLP
    tpb = seq // tm
    nm, nf = m // tm, D_FF // tf
    assert nf >= 3
    any_spec = pl.BlockSpec(memory_space=pl.ANY)
    s_args, s_in_specs, s_out_shape, s_out_specs, s_scratch = _stream_operands(hosted)
    kernel = _host_streams(_mlp_p_kernel, 6, 1, 3, hosted, nm * nf,
                           lambda: pl.program_id(0) * nf + pl.program_id(1))
    return pl.pallas_call(
        kernel,
        grid=(nm, nf),
        in_specs=[pl.BlockSpec((tm, D_MODEL), lambda mi, f: (mi, 0)),
                  pl.BlockSpec((D_MODEL, tf), lambda mi, f: (0, f)),
                  pl.BlockSpec((tf, D_MODEL), lambda mi, f: (f, 0)),
                  pl.BlockSpec((None, 1, D_MODEL), lambda mi, f: (mi // tpb, 0, 5)),
                  pl.BlockSpec((1, D_MODEL), lambda mi, f: (0, 0)),
                  any_spec] + s_in_specs,
        out_specs=[any_spec] + s_out_specs,
        out_shape=[jax.ShapeDtypeStruct((m, D_MODEL), F32)] + s_out_shape,
        scratch_shapes=[pltpu.VMEM((tm, D_MODEL), F32), pltpu.VMEM((tm, D_MODEL), F32),
                        pltpu.SemaphoreType.DMA((2,))] + s_scratch,
        compiler_params=_cparams("arbitrary", "arbitrary"),
        name="mlp_prompt",
    )(h2, w_up, w_down, mod_p, norm_gf, x1, *s_args)


def _inproj_s_kernel(x_ref, sh_ref, sc_ref, g_ref, w_ref, o_ref, wb_ref, h_scr):
    @pl.when(pl.program_id(0) == 0)
    def _():
        y = _rms(x_ref[...]) * g_ref[...]
        h_scr[...] = (y * (1.0 + sc_ref[...]) + sh_ref[...]).astype(BF16)

    w = w_ref[...].astype(BF16)
    wb_ref[...] = w
    o_ref[...] = jnp.dot(h_scr[...], w, preferred_element_type=F32)


def _inproj_sample(x_s, mod_s, norm_g, w_in):
    m = x_s.shape[0]
    tn = D_ATTN
    return pl.pallas_call(
        _inproj_s_kernel,
        grid=(D_IN // tn,),
        in_specs=[pl.BlockSpec((m, D_MODEL), lambda j: (0, 0)),
                  pl.BlockSpec((m, D_MODEL), lambda j: (0, 0)),
                  pl.BlockSpec((m, D_MODEL), lambda j: (0, 1)),
                  pl.BlockSpec((1, D_MODEL), lambda j: (0, 0)),
                  pl.BlockSpec((D_MODEL, tn), lambda j: (0, j))],
        out_specs=[pl.BlockSpec((m, tn), lambda j: (0, j)),
                   pl.BlockSpec((D_MODEL, tn), lambda j: (0, j))],
        out_shape=[jax.ShapeDtypeStruct((m, D_IN), F32), jax.ShapeDtypeStruct((D_MODEL, D_IN), BF16)],
        scratch_shapes=[pltpu.VMEM((m, D_MODEL), BF16)],
        compiler_params=_cparams("arbitrary"),
        name="inproj_sample",
    )(x_s, mod_s, mod_s, norm_g, w_in)


def _attn_s_kernel(bias_ref, q_ref, kn_ref, vn_ref, k0_ref, v0_ref, k1_ref, v1_ref, k2_ref, v2_ref,
                   o_ref):
    k_refs = (k0_ref, k1_ref, k2_ref)
    v_refs = (v0_ref, v1_ref, v2_ref)

    bs, pairs = k0_ref.shape[0], k0_ref.shape[1]
    k_refs = [r.reshape(bs, pairs, KV_ROWS, HEAD_DIM) for r in k_refs]
    v_refs = [r.reshape(bs, pairs, KV_ROWS, HEAD_DIM) for r in v_refs]

    def both(x):
        return x, pltpu.roll(x, HEADS_PER_GROUP, axis=0)

    def body(s, carry):
        outs, lses = [], []
        for g in range(N_GROUPS):
            q = q_ref[s, g]
            kn = kn_ref[s, g]
            vn = vn_ref[s, g]
            kt = k_refs[g][s]
            vt = v_refs[g][s]
            sh = jnp.sum(kt * q[None], axis=-1, keepdims=True) * SCALE + bias_ref[g]
            sn = jnp.sum(kn * q, axis=-1, keepdims=True) * SCALE
            m_a, m_b = both(jnp.max(sh, axis=0))
            mx = jnp.maximum(jnp.maximum(m_a, m_b), sn)
            p = jnp.exp(sh - mx[None])
            pn = jnp.exp(sn - mx)
            d_a, d_b = both(jnp.sum(p, axis=0))
            n_a, n_b = both(jnp.sum(p * vt, axis=0))
            den = d_a + d_b + pn
            outs.append((n_a + n_b + pn * vn) / den)
            lses.append(mx + jnp.log(den))
        mx = jnp.maximum(jnp.maximum(lses[0], lses[1]), lses[2])
        e = [jnp.exp(l - mx) for l in lses]
        tot = e[0] + e[1] + e[2]
        o_ref[s] = ((e[0] * outs[0] + e[1] * outs[1] + e[2] * outs[2]) / tot)[0:HEADS_PER_GROUP, :]
        return carry

    lax.fori_loop(0, q_ref.shape[0], body, 0)


def _attn_sample(bias_s, q3, kn3, vn3, caches):
    nb = q3.shape[0]
    bs = BS_ATTN

    def twice(t3):
        t = t3.reshape(nb, N_GROUPS, HEADS_PER_GROUP, HEAD_DIM)
        return jnp.concatenate([t, t], axis=2)

    tok = pl.BlockSpec((bs, N_GROUPS, KV_ROWS, HEAD_DIM), lambda i: (i, 0, 0, 0))
    in_specs = [_const_spec((N_GROUPS, BAND // 2, KV_ROWS, HEAD_DIM)), tok, tok, tok]
    args = [bias_s, twice(q3), twice(kn3), twice(vn3)]
    for c, (_, dil) in zip(caches, ATTN_GROUPS):
        cv = c.reshape(nb, BAND // 2, 2, dil * 2, HEADS_PER_GROUP, HEAD_DIM)
        for kv in range(2):
            in_specs.append(pl.BlockSpec((bs, BAND // 2, 2, None, HEADS_PER_GROUP, HEAD_DIM),
                                         lambda i, kv=kv: (i, 0, 0, kv, 0, 0)))
            args.append(cv)
    return pl.pallas_call(
        _attn_s_kernel,
        grid=(nb // bs,),
        in_specs=in_specs,
        out_specs=pl.BlockSpec((bs, HEADS_PER_GROUP, HEAD_DIM), lambda i: (i, 0, 0)),
        out_shape=jax.ShapeDtypeStruct((nb, HEADS_PER_GROUP, HEAD_DIM), F32),
        compiler_params=_cparams("arbitrary"),
        name="attn_sample",
    )(*args)


def _mix_s_kernel(o_ref, u_ref, hist_ref, ga_ref, gb_ref, x_ref, gt1_ref, sh2_ref, sc2_ref,
                  wua_ref, wp_ref, ps_ref, wup_ref, wo_ref, g2_ref, x1_ref, h2_ref, np_ref):
    u = u_ref[...]
    zs = []
    for g, w in enumerate(POOL_WINDOWS):
        sl = slice(g * POOL_GROUP, (g + 1) * POOL_GROUP)
        win = u[:, sl]
        for jj in range(1, w):
            win = win + hist_ref[:, POOL_HIST - jj, sl]
        zs.append(win / float(w) - u[:, sl])
    p = _pool_project(zs, wp_ref, ps_ref)
    np_ref[:, 0:POOL_HIST - 1, :] = hist_ref[:, 1:POOL_HIST, :]
    np_ref[:, POOL_HIST - 1, :] = u
    _mix_tail([slice(None)], [o_ref[...]], [p],
              lambda rs: (jax.nn.sigmoid(ga_ref[rs, :]), jax.nn.sigmoid(gb_ref[rs, :])),
              lambda rs: x_ref[rs, :],
              lambda rs: (gt1_ref[rs, :], sh2_ref[rs, :], sc2_ref[rs, :]),
              wua_ref, wup_ref, wo_ref, g2_ref, x1_ref, h2_ref)


def _mix_sample(o_attn, proj_s, hist, x_s, mod_s, w_up_attn, w_pool, pool_scale, w_up_pool, w_out, norm_g2):
    m = x_s.shape[0]

    def cols(width, c):
        return pl.BlockSpec((m, width), lambda i: (0, c))

    u0 = 3 * D_ATTN
    u = lax.slice_in_dim(proj_s, u0, u0 + D_POOL, axis=1)
    ga = lax.slice_in_dim(proj_s, u0 + D_POOL, u0 + D_POOL + D_MODEL, axis=1)
    gb = lax.slice_in_dim(proj_s, u0 + D_POOL + D_MODEL, D_IN, axis=1)
    in_specs = [cols(D_GROUP, 0), cols(D_POOL, 0), _const_spec((m, POOL_HIST, D_POOL)),
                cols(D_MODEL, 0), cols(D_MODEL, 0),
                cols(D_MODEL, 0), cols(D_MODEL, 2), cols(D_MODEL, 3), cols(D_MODEL, 4),
                _const_spec((D_GROUP, D_MODEL)), _const_spec((len(POOL_WINDOWS), POOL_GROUP, POOL_GROUP)),
                _const_spec((1, D_POOL)), _const_spec((D_POOL, D_MODEL)), _const_spec((D_MODEL, D_MODEL)),
                _const_spec((1, D_MODEL))]
    return pl.pallas_call(
        _mix_s_kernel,
        grid=(1,),
        in_specs=in_specs,
        out_specs=[_const_spec((m, D_MODEL)), _const_spec((m, D_MODEL)), _const_spec((m, POOL_HIST, D_POOL))],
        out_shape=[jax.ShapeDtypeStruct((m, D_MODEL), F32), jax.ShapeDtypeStruct((m, D_MODEL), BF16),
                   jax.ShapeDtypeStruct((m, POOL_HIST, D_POOL), F32)],
        compiler_params=_cparams("arbitrary"),
        name="mix_sample",
    )(o_attn, u, hist, ga, gb, x_s, mod_s, mod_s, mod_s, w_up_attn, w_pool, pool_scale, w_up_pool, w_out, norm_g2)


class _ShiftStream:
    def __init__(self, cache, new, out, buf, sems, rows, ns):
        self.cache, self.new, self.out, self.buf, self.sems = cache, new, out, buf, sems
        self.rows, self.ns = rows, ns
        self.slots = buf.shape[0]
        nb, w = cache.shape[1], cache.shape[2]
        assert w % rows == 0 and nb % ns == 0 and (ns == 1 or rows == w)
        self.cps = w // rows
        self.n_chunks = (nb // ns) * self.cps

    def _where(self, k):
        return (k // self.cps) * self.ns, (k % self.cps) * self.rows, k % self.slots

    def _body_in(self, k):
        b0, w0, slot = self._where(k)
        return pltpu.make_async_copy(self.cache.at[0, pl.ds(b0, self.ns), pl.ds(w0 + 1, self.rows - 1)],
                                     self.buf.at[slot, :, pl.ds(0, self.rows - 1)], self.sems.at[0, slot])

    def _next_in(self, k):
        b0, w0, slot = self._where(k)
        return pltpu.make_async_copy(self.cache.at[0, pl.ds(b0, self.ns), pl.ds(w0 + self.rows, 1)],
                                     self.buf.at[slot, :, pl.ds(self.rows - 1, 1)], self.sems.at[1, slot])

    def _new_in(self, k):
        b0, _, slot = self._where(k)
        return pltpu.make_async_copy(self.new.at[pl.ds(b0, self.ns)],
                                     self.buf.at[slot, :, self.rows - 1], self.sems.at[1, slot])

    def _out(self, k):
        b0, w0, slot = self._where(k)
        return pltpu.make_async_copy(self.buf.at[slot],
                                     self.out.at[0, pl.ds(b0, self.ns), pl.ds(w0, self.rows)], self.sems.at[2, slot])

    def _last_row(self, k, act):
        if isinstance(k, int) or self.cps == 1:
            at_end = self.cps == 1 or k % self.cps == self.cps - 1
            getattr(self._new_in(k) if at_end else self._next_in(k), act)()
            return
        at_end = k % self.cps == self.cps - 1

        @pl.when(at_end)
        def _():
            getattr(self._new_in(k), act)()

        @pl.when(jnp.logical_not(at_end))
        def _():
            getattr(self._next_in(k), act)()

    def start_in(self, k):
        self._body_in(k).start()
        self._last_row(k, "start")

    def wait_in(self, k):
        self._body_in(k).wait()
        self._last_row(k, "wait")

    def start_out(self, k):
        self._out(k).start()

    def wait_out(self, k):
        self._out(k).wait()

    def step(self, k, lag):
        self.wait_in(k)
        self.start_out(k)

        @pl.when(k - lag >= 0)
        def _():
            self.wait_out(k - lag)

        @pl.when(k - lag + self.slots < self.n_chunks)
        def _():
            self.start_in(k - lag + self.slots)

    def tick(self, s, n_steps):
        n = self.n_chunks
        per = -(-n // n_steps)
        stride = max(n_steps // n, 1)
        assert self.slots >= 2 * per and n >= self.slots

        @pl.when(s == 0)
        def _():
            for k in range(self.slots - per):
                self.start_in(k)

        for i in range(per):
            k = (s // stride) * per + i

            @pl.when((s % stride == 0) & (k < n))
            def _(k=k):
                self.step(k, per)

        @pl.when(s == n_steps - 1)
        def _():
            for k in range(n - per, n):
                self.wait_out(k)


class _Hosted(NamedTuple):
    cache: jax.Array
    new: jax.Array
    rows: int
    ns: int
    slots: int


def _host_streams(body, n_in, n_out, n_scr, hosted, n_steps, step_index):
    plans = [(h.rows, h.ns) for h in hosted]
    ns_ = len(plans)

    def kernel(*refs):
        refs = list(refs)
        ins, s_in = refs[:n_in], refs[n_in:n_in + 2 * ns_]
        o0 = n_in + 2 * ns_
        outs, s_out = refs[o0:o0 + n_out], refs[o0 + n_out:o0 + n_out + ns_]
        c0 = o0 + n_out + ns_
        scr, s_scr = refs[c0:c0 + n_scr], refs[c0 + n_scr:]
        s = step_index()
        for i, (rows, ns) in enumerate(plans):
            _ShiftStream(s_in[2 * i], s_in[2 * i + 1], s_out[i], s_scr[2 * i], s_scr[2 * i + 1],
                         rows, ns).tick(s, n_steps)
        body(*ins, *outs, *scr)

    return kernel


def _stream_operands(hosted):
    any_spec = pl.BlockSpec(memory_space=pl.ANY)
    args = [a for h in hosted for a in (h.cache, h.new)]
    out_shape = [jax.ShapeDtypeStruct(h.cache.shape, h.cache.dtype) for h in hosted]
    scratch = [s for h in hosted for s in (
        pltpu.VMEM((h.slots, h.ns, h.rows, 2, HEADS_PER_GROUP, HEAD_DIM), F32),
        pltpu.SemaphoreType.DMA((3, h.slots)))]
    return args, [any_spec] * len(args), out_shape, [any_spec] * len(hosted), scratch


def _alibi_slopes():
    h = jnp.arange(1, N_HEADS + 1, dtype=F32)
    return jnp.exp2(-ALIBI_MAX_BIAS * h / N_HEADS)


def _prompt_bias(g):
    dil = ATTN_GROUPS[g][1]
    slopes = _alibi_slopes()[g * HEADS_PER_GROUP:(g + 1) * HEADS_PER_GROUP]
    a = jnp.arange(BAND)[:, None]
    b = jnp.arange(2 * BAND)[None, :]
    dist = a - b + BAND
    valid = (dist >= 0) & (dist <= BAND)
    bias = -slopes[:, None, None] * (dist * dil).astype(F32)
    return jnp.where(valid[None], bias, NEG)


def _sample_bias():
    slopes = _alibi_slopes().reshape(N_GROUPS, 1, HEADS_PER_GROUP, 1)
    dil = jnp.array([d for _, d in ATTN_GROUPS], F32).reshape(N_GROUPS, 1, 1, 1)
    back = (BAND - jnp.arange(BAND, dtype=F32)).reshape(1, BAND, 1, 1)
    bias = jnp.broadcast_to(-slopes * (back * dil), (N_GROUPS, BAND, HEADS_PER_GROUP, HEAD_DIM))
    return bias.reshape(N_GROUPS, BAND // 2, KV_ROWS, HEAD_DIM)


def kernel(x_prompt, x_sample, c_prompt, c_sample, cache_kv_w128, cache_kv_w512, cache_kv_w2048, state_pool,
           norm_mix_g, w_ada, b_ada, w_in, w_up_attn, w_pool, pool_scale, w_up_pool, w_out, norm_mlp_g,
           w_mlp_up, w_mlp_down, norm_final_g):
    batch, seq, _ = x_prompt.shape
    nb = x_sample.shape[0]
    depth = w_in.shape[0]
    assert depth == 1 and x_sample.shape[1] == 1
    caches = (cache_kv_w128, cache_kv_w512, cache_kv_w2048)
    for c, (w, dil) in zip(caches, ATTN_GROUPS):
        assert c.shape[2] == w == BAND * dil

    w_ua_b = w_up_attn[0].astype(BF16)
    w_pool_b = w_pool[0].astype(BF16)
    w_up_b = w_up_pool[0].astype(BF16)
    w_out_b = w_out[0].astype(BF16)
    g1 = norm_mix_g[0].reshape(1, D_MODEL)
    g2 = norm_mlp_g[0].reshape(1, D_MODEL)
    gf = norm_final_g.reshape(1, D_MODEL)
    ps = pool_scale[0].reshape(1, D_POOL)

    mod = _ada(jnp.concatenate([c_sample, c_prompt], axis=0), w_ada[0], b_ada[0])
    mod_s = mod[:nb]
    mod_p = mod[nb:].reshape(batch, 1, 6 * D_MODEL)

    xs = x_sample.reshape(nb, D_MODEL)
    proj_s, w_in_b = _inproj_sample(xs, mod_s, g1, w_in[0])
    q3 = proj_s[:, 0:D_ATTN].reshape(nb, N_HEADS, HEAD_DIM)
    kn3 = proj_s[:, D_ATTN:2 * D_ATTN].reshape(nb, N_HEADS, HEAD_DIM)
    vn3 = proj_s[:, 2 * D_ATTN:3 * D_ATTN].reshape(nb, N_HEADS, HEAD_DIM)
    o_s = _attn_sample(_sample_bias(), q3, kn3, vn3, [c[0] for c in caches])
    new_rows = [jnp.stack([kn3[:, g * HEADS_PER_GROUP:(g + 1) * HEADS_PER_GROUP],
                           vn3[:, g * HEADS_PER_GROUP:(g + 1) * HEADS_PER_GROUP]], axis=1)
                for g in range(N_GROUPS)]
    x1_s, h2_s, pool_s = _mix_sample(o_s.reshape(nb, D_GROUP), proj_s, state_pool[0], xs, mod_s,
                                     w_ua_b, w_pool_b, ps, w_up_b, w_out_b, g2)

    x2d = x_prompt.reshape(batch * seq, D_MODEL)
    *qkv_sub, u_p, gates, kv2, kv1, kv0, kv_s1 = _inproj_prompt(
        x2d, mod_p, g1, w_in_b, batch, seq,
        (_Hosted(caches[1], new_rows[1], rows=512, ns=1, slots=3),))
    o0, l0, w_md_b = _attn_prompt_group(qkv_sub[0], _prompt_bias(0), 0, batch, seq, cast_weight=w_mlp_down[0])
    o1, l1, w_mu_b = _attn_prompt_group(qkv_sub[1], _prompt_bias(1), 1, batch, seq, cast_weight=w_mlp_up[0])
    o2, l2, kv_s0 = _attn_prompt_group(qkv_sub[2], _prompt_bias(2), 2, batch, seq,
                                       hosted=(_Hosted(caches[0], new_rows[0], rows=128, ns=8, slots=4),))
    outs, lses = [o0, o1, o2], [l0, l1, l2]
    y_s = _mlp_sample(h2_s, x1_s, mod_s, w_mu_b, w_md_b, gf)
    x1_p, h2_p = _mix_prompt(
        outs, lses, u_p, gates, x2d, mod_p, w_ua_b, w_pool_b, ps, w_up_b, w_out_b, g2, seq, ())
    y_p, kv_s2 = _mlp_prompt(
        h2_p, x1_p, mod_p, w_mu_b, w_md_b, gf, seq,
        (_Hosted(caches[2], new_rows[2], rows=1024, ns=1, slots=3),))

    pool_p = u_p.reshape(batch, seq, D_POOL)[:, seq - POOL_HIST:][None]
    kv0, kv1, kv2 = (a.reshape(1, batch, -1, 2, HEADS_PER_GROUP, HEAD_DIM) for a in (kv0, kv1, kv2))
    return (y_p.reshape(batch, seq, D_MODEL), y_s.reshape(nb, 1, D_MODEL), kv0, kv1, kv2, pool_p,
            kv_s0, kv_s1, kv_s2, pool_s[None])
```

```python
import functools
from typing import NamedTuple

import jax
import jax.numpy as jnp
from jax import lax
from jax.experimental import pallas as pl
from jax.experimental.pallas import tpu as pltpu

F32 = jnp.float32
BF16 = jnp.bfloat16

D_MODEL = 2048
HEAD_DIM = 128
HEADS_PER_GROUP = 4
ATTN_GROUPS = ((128, 1), (512, 4), (2048, 16))
N_GROUPS = len(ATTN_GROUPS)
N_HEADS = HEADS_PER_GROUP * N_GROUPS
D_ATTN = N_HEADS * HEAD_DIM
D_GROUP = HEADS_PER_GROUP * HEAD_DIM
BAND = 128
POOL_WINDOWS = (2, 4, 8, 16)
POOL_GROUP = 128
D_POOL = POOL_GROUP * len(POOL_WINDOWS)
POOL_HIST = max(POOL_WINDOWS) - 1
POOL_HALO = 16
D_FF = 4 * D_MODEL
D_IN = 3 * D_ATTN + D_POOL + 2 * D_MODEL
ALIBI_MAX_BIAS = 8.0
EPS = 1e-6
SCALE = HEAD_DIM ** -0.5
NEG = -1e30

VMEM_LIMIT_BYTES = 60 * 1024 * 1024

IN_TILE = D_GROUP
N_IN_TILES = D_IN // IN_TILE
QKV_TILES = 3 * N_GROUPS
POOL_TILE = QKV_TILES
GATE_TILE0 = POOL_TILE + 1

TM_IN = 1024
IN_SPLIT = 2
RELAY = 4
TM_MIX = 512
MIX_SPLIT = 2
TM_MLP = 1024
TF_MLP = 512
TF_MLP_SAMPLE = 1024
BS_ATTN = 8
ATTN_UNROLL = 3


def _cparams(*sem):
    return pltpu.CompilerParams(dimension_semantics=sem, vmem_limit_bytes=VMEM_LIMIT_BYTES)


def _rms(x):
    return x * lax.rsqrt(jnp.mean(x * x, axis=-1, keepdims=True) + EPS)


def _ada_kernel(c_ref, w_ref, b_ref, o_ref):
    c = c_ref[...]
    a = (c * jax.nn.sigmoid(c)).astype(BF16)
    o_ref[...] = jnp.dot(a, w_ref[...].astype(BF16), preferred_element_type=F32) + b_ref[...]


def _ada(c_all, w_ada, b_ada):
    m = c_all.shape[0]
    tn = 1024
    return pl.pallas_call(
        _ada_kernel,
        grid=(6 * D_MODEL // tn,),
        in_specs=[pl.BlockSpec((m, D_MODEL), lambda j: (0, 0)),
                  pl.BlockSpec((D_MODEL, tn), lambda j: (0, j)),
                  pl.BlockSpec((1, tn), lambda j: (0, j))],
        out_specs=pl.BlockSpec((m, tn), lambda j: (0, j)),
        out_shape=jax.ShapeDtypeStruct((m, 6 * D_MODEL), F32),
        compiler_params=_cparams("arbitrary"),
        name="ada",
    )(c_all, w_ada, b_ada.reshape(1, 6 * D_MODEL))


KV_ROWS = 2 * HEADS_PER_GROUP


def _store_heads(ref, row0, kv, val):
    for h in range(HEADS_PER_GROUP):
        start = row0 * KV_ROWS + kv * HEADS_PER_GROUP + h
        ref[pl.ds(start, val.shape[0], stride=KV_ROWS), :] = val[:, h * HEAD_DIM:(h + 1) * HEAD_DIM]


def _inproj_p_kernel(sh_ref, sc_ref, g_ref, w_ref, x_hbm,
                     qkv0_ref, qkv1_ref, qkv2_ref, u_ref, gate_ref, kv2_ref, kv1_ref, kv0_ref,
                     h_scr, acc_scr, mid_scr, x_scr, x_sem, *, tiles_per_batch):
    mi, j = pl.program_id(0), pl.program_id(1)
    tm = x_scr.shape[0]

    def x_copy(i):
        return pltpu.make_async_copy(x_hbm.at[pl.ds(i * tm, tm)], x_scr, x_sem.at[0])

    @pl.when(j == 0)
    def _():
        @pl.when(mi == 0)
        def _():
            x_copy(mi).start()

        x_copy(mi).wait()
        y = _rms(x_scr[...]) * g_ref[...]
        h_scr[...] = (y * (1.0 + sc_ref[...]) + sh_ref[...]).astype(BF16)

    @pl.when((j == 1) & (mi + 1 < pl.num_programs(0)))
    def _():
        x_copy(mi + 1).start()

    th = tm // IN_SPLIT

    def halves():
        for s in range(IN_SPLIT):
            yield s, jnp.dot(h_scr[s * th:(s + 1) * th, :], w_ref[...], preferred_element_type=F32)

    def store_sub(qkv_ref, dil, s, acc):
        if dil == 1:
            qkv_ref[0, s * th:(s + 1) * th, :] = acc.astype(BF16)
            return
        n = th // dil
        for h in range(HEADS_PER_GROUP):
            sl = slice(h * HEAD_DIM, (h + 1) * HEAD_DIM)
            acc_scr[s, h] = acc[:, sl]
            if dil == RELAY * RELAY:
                m = th // RELAY
                for q in range(RELAY):
                    mid_scr[s, h, q * m:(q + 1) * m, :] = acc_scr[s, h, pl.ds(q, m, stride=RELAY), :]
                for q in range(RELAY):
                    for p in range(RELAY):
                        qkv_ref[RELAY * p + q, s * n:(s + 1) * n, sl] = (
                            mid_scr[s, h, pl.ds(q * m + p, n, stride=RELAY), :].astype(BF16))
            else:
                for r in range(dil):
                    qkv_ref[r, s * n:(s + 1) * n, sl] = acc_scr[s, h, pl.ds(r, n, stride=dil), :].astype(BF16)

    last = pl.program_id(0) % tiles_per_batch == tiles_per_batch - 1
    kv_refs = (kv0_ref, kv1_ref, kv2_ref)
    for g, (qkv_ref, (_, dil)) in enumerate(zip((qkv0_ref, qkv1_ref, qkv2_ref), ATTN_GROUPS)):
        @pl.when(j == g)
        def _(qkv_ref=qkv_ref, dil=dil):
            for s, acc in halves():
                store_sub(qkv_ref, dil, s, acc)

        @pl.when((j == N_GROUPS + g) | (j == 2 * N_GROUPS + g))
        def _(g=g, qkv_ref=qkv_ref, dil=dil):
            kv_ref = kv_refs[g]
            first_kept = tm - kv_ref.shape[0] // KV_ROWS
            is_v = (j >= 2 * N_GROUPS).astype(jnp.int32)
            for s, acc in halves():
                store_sub(qkv_ref, dil, s, acc)
                row0 = max(s * th, first_kept)
                if first_kept == 0:
                    _store_heads(kv_ref, s * th, is_v, acc)
                elif row0 < (s + 1) * th:
                    @pl.when(last)
                    def _(acc=acc, row0=row0, s=s):
                        _store_heads(kv_ref, row0 - first_kept, is_v, acc[row0 - s * th:, :])

    @pl.when(j == POOL_TILE)
    def _():
        for s, acc in halves():
            u_ref[s * th:(s + 1) * th, :] = acc

    @pl.when(j >= GATE_TILE0)
    def _():
        for s, acc in halves():
            gate_ref[s * th:(s + 1) * th, :] = (0.5 * jnp.tanh(0.5 * acc) + 0.5).astype(BF16)


def _inproj_prompt(x2d, mod_p, norm_g, w_in, batch, seq, hosted):
    m = x2d.shape[0]
    tm = TM_IN
    tpb = seq // tm
    keep = [min(w, seq) for w, _ in ATTN_GROUPS]
    assert keep[2] == seq and tm >= keep[1] and seq % tm == 0

    def sub_spec(g):
        dil = ATTN_GROUPS[g][1]
        return pl.BlockSpec((None, None, dil, tm // dil, D_GROUP),
                            lambda mi, j: (jnp.clip((j - g) // N_GROUPS, 0, 2), mi // tpb, 0, mi % tpb, 0))

    def sub_shape(g):
        dil = ATTN_GROUPS[g][1]
        return jax.ShapeDtypeStruct((3, batch, dil, seq // dil, D_GROUP), BF16)

    def kv_spec(rows, tail_only):
        return pl.BlockSpec((None, rows * KV_ROWS, HEAD_DIM),
                            lambda mi, j: (mi // tpb, 0 if tail_only else mi % tpb, 0))

    def kv_shape(rows):
        return jax.ShapeDtypeStruct((batch, rows * KV_ROWS, HEAD_DIM), F32)

    in_specs = [
        pl.BlockSpec((None, 1, D_MODEL), lambda mi, j: (mi // tpb, 0, 0)),
        pl.BlockSpec((None, 1, D_MODEL), lambda mi, j: (mi // tpb, 0, 1)),
        pl.BlockSpec((1, D_MODEL), lambda mi, j: (0, 0)),
        pl.BlockSpec((D_MODEL, IN_TILE), lambda mi, j: (0, j)),
        pl.BlockSpec(memory_space=pl.ANY),
    ]
    out_specs = [
        sub_spec(0), sub_spec(1), sub_spec(2),
        pl.BlockSpec((tm, IN_TILE), lambda mi, j: (mi, 0)),
        pl.BlockSpec((tm, IN_TILE), lambda mi, j: (mi, jnp.clip(j - GATE_TILE0, 0, N_IN_TILES - GATE_TILE0 - 1))),
        kv_spec(tm, False), kv_spec(keep[1], True), kv_spec(keep[0], True),
    ]
    out_shape = [
        sub_shape(0), sub_shape(1), sub_shape(2),
        jax.ShapeDtypeStruct((m, D_POOL), F32),
        jax.ShapeDtypeStruct((m, 2 * D_MODEL), BF16),
        kv_shape(keep[2]), kv_shape(keep[1]), kv_shape(keep[0]),
    ]
    s_args, s_in_specs, s_out_shape, s_out_specs, s_scratch = _stream_operands(hosted)
    kernel = _host_streams(functools.partial(_inproj_p_kernel, tiles_per_batch=tpb), len(in_specs), len(out_specs),
                           5, hosted, (m // tm) * N_IN_TILES,
                           lambda: pl.program_id(0) * N_IN_TILES + pl.program_id(1))
    return pl.pallas_call(
        kernel,
        grid=(m // tm, N_IN_TILES),
        in_specs=in_specs + s_in_specs, out_specs=out_specs + s_out_specs, out_shape=out_shape + s_out_shape,
        scratch_shapes=[pltpu.VMEM((tm, D_MODEL), BF16),
                        pltpu.VMEM((IN_SPLIT, HEADS_PER_GROUP, tm // IN_SPLIT, HEAD_DIM), F32),
                        pltpu.VMEM((IN_SPLIT, HEADS_PER_GROUP, tm // IN_SPLIT, HEAD_DIM), F32),
                        pltpu.VMEM((tm, D_MODEL), F32), pltpu.SemaphoreType.DMA((1,))] + s_scratch,
        compiler_params=_cparams("arbitrary", "arbitrary"),
        name="inproj_prompt",
    )(mod_p, mod_p, norm_g, w_in, x2d, *s_args)


def _attn_block(q, k, v, bias_fn):
    lane = lax.broadcasted_iota(jnp.int32, (BAND, HEAD_DIM), 1)
    heads = [slice(h * HEAD_DIM, (h + 1) * HEAD_DIM) for h in range(HEADS_PER_GROUP)]
    scores = [lax.dot_general(q[:, sl], k[:, sl], (((1,), (1,)), ((), ())), preferred_element_type=F32)
              for sl in heads]
    scores = [s * SCALE + bias_fn(h) for h, s in enumerate(scores)]
    maxes = [jnp.max(s, axis=-1, keepdims=True) for s in scores]
    probs = [jnp.exp(s - mx) for s, mx in zip(scores, maxes)]
    dens = [jnp.sum(p, axis=-1, keepdims=True) for p in probs]
    outs = [jnp.dot(p.astype(BF16), v[:, sl], preferred_element_type=F32) / den
            for p, sl, den in zip(probs, heads, dens)]
    lse = jnp.zeros((BAND, HEAD_DIM), F32)
    for h, (mx, den) in enumerate(zip(maxes, dens)):
        lse = jnp.where(lane == h, mx + jnp.log(den), lse)
    return outs, lse


def _attn_p_kernel(bias_ref, q_ref, k_ref, v_ref, o_ref, l_ref, o_scr):
    dil, n, _ = q_ref.shape
    nb = n // BAND

    def rows(start):
        return pl.ds(start, BAND) if dil == 1 else pl.ds(start, BAND, stride=dil)

    def put(start, outs, lse):
        for h in range(HEADS_PER_GROUP):
            o_scr[h, rows(start), :] = outs[h]
        l_ref[rows(start), :] = lse

    for r in range(dil):
        put(r, *_attn_block(q_ref[r, 0:BAND, :], k_ref[r, 0:BAND, :], v_ref[r, 0:BAND, :],
                            lambda h: bias_ref[h, :, BAND:2 * BAND]))

        if nb > 1:
            def body(i, carry, r=r):
                r0 = pl.multiple_of(i * BAND, BAND)
                rk = pl.multiple_of((i - 1) * BAND, BAND)
                put(r0 * dil + r, *_attn_block(q_ref[r, pl.ds(r0, BAND), :], k_ref[r, pl.ds(rk, 2 * BAND), :],
                                               v_ref[r, pl.ds(rk, 2 * BAND), :], lambda h: bias_ref[h]))
                return carry

            lax.fori_loop(1, nb, body, 0, unroll=ATTN_UNROLL)

    for h in range(HEADS_PER_GROUP):
        o_ref[:, h * HEAD_DIM:(h + 1) * HEAD_DIM] = o_scr[h].astype(BF16)


def _with_cast_passenger(body, n_in, n_out):
    def kernel(*refs):
        ins, w_ref = refs[:n_in], refs[n_in]
        outs, wb_ref = refs[n_in + 1:n_in + 1 + n_out], refs[n_in + 1 + n_out]
        wb_ref[...] = w_ref[...].astype(BF16)
        body(*ins, *outs, *refs[n_in + 2 + n_out:])

    return kernel


def _attn_prompt_group(qkv_sub, bias, g, batch, seq, cast_weight=None, hosted=()):
    assert cast_weight is None or not hosted
    dil = ATTN_GROUPS[g][1]
    n = seq // dil

    def sub(which):
        return pl.BlockSpec((None, None, dil, n, D_GROUP), lambda b: (which, b, 0, 0, 0))

    kernel = _attn_p_kernel
    in_specs = [pl.BlockSpec((HEADS_PER_GROUP, BAND, 2 * BAND), lambda b: (0, 0, 0)), sub(0), sub(1), sub(2)]
    out_specs = [pl.BlockSpec((seq, D_GROUP), lambda b: (b, 0)), pl.BlockSpec((seq, HEAD_DIM), lambda b: (b, 0))]
    out_shape = [jax.ShapeDtypeStruct((batch * seq, D_GROUP), BF16),
                 jax.ShapeDtypeStruct((batch * seq, HEAD_DIM), F32)]
    args = [bias, qkv_sub, qkv_sub, qkv_sub]
    if cast_weight is not None:
        slab = pl.BlockSpec((cast_weight.shape[0] // batch, cast_weight.shape[1]), lambda b: (b, 0))
        kernel = _with_cast_passenger(kernel, len(in_specs), len(out_specs))
        in_specs.append(slab)
        out_specs.append(slab)
        out_shape.append(jax.ShapeDtypeStruct(cast_weight.shape, BF16))
        args.append(cast_weight)
    s_args, s_in_specs, s_out_shape, s_out_specs, s_scratch = _stream_operands(hosted)
    if hosted:
        kernel = _host_streams(kernel, len(in_specs), len(out_specs), 1, hosted, batch, lambda: pl.program_id(0))
    return pl.pallas_call(
        kernel,
        grid=(batch,),
        in_specs=in_specs + s_in_specs, out_specs=out_specs + s_out_specs, out_shape=out_shape + s_out_shape,
        scratch_shapes=[pltpu.VMEM((HEADS_PER_GROUP, seq, HEAD_DIM), F32)] + s_scratch,
        compiler_params=_cparams("arbitrary"),
        name=f"attn_prompt_g{g}",
    )(*args, *s_args)


def _merge_heads(outs, lses):
    cols = []
    for h in range(HEADS_PER_GROUP):
        sl = slice(h * HEAD_DIM, (h + 1) * HEAD_DIM)
        l = [lg[:, h:h + 1] for lg in lses]
        mx = jnp.maximum(jnp.maximum(l[0], l[1]), l[2])
        e = [jnp.exp(x - mx) for x in l]
        tot = e[0] + e[1] + e[2]
        cols.append(sum((e[g] / tot) * outs[g][:, sl].astype(F32) for g in range(N_GROUPS)))
    return jnp.concatenate(cols, axis=-1)


def _mix_tail(parts, o_attn, p, load_gates, load_x, load_mod, wua_ref, wup_ref, wo_ref, g2_ref, x1_ref, h2_ref):
    a = [jnp.dot(o.astype(BF16), wua_ref[...], preferred_element_type=F32) for o in o_attn]
    b = [jnp.dot(q.astype(BF16), wup_ref[...], preferred_element_type=F32) for q in p]
    mix = []
    for rs, ai, bi in zip(parts, a, b):
        ga, gb = load_gates(rs)
        mix.append((ga * ai + gb * bi).astype(BF16))
    y = [jnp.dot(m, wo_ref[...], preferred_element_type=F32) for m in mix]
    for rs, yi in zip(parts, y):
        gt1, sh2, sc2 = load_mod(rs)
        x1 = load_x(rs) + gt1 * yi
        x1_ref[rs, :] = x1
        h2_ref[rs, :] = (_rms(x1) * g2_ref[...] * (1.0 + sc2) + sh2).astype(BF16)


def _pool_project(z_groups, wp_ref, ps_ref):
    cols = [jnp.dot(z.astype(BF16), wp_ref[g], preferred_element_type=F32) for g, z in enumerate(z_groups)]
    return jnp.concatenate(cols, axis=-1) * ps_ref[...]


def _mix_p_kernel(o0_ref, o1_ref, o2_ref, l0_ref, l1_ref, l2_ref, u_ref, uh_ref, gate_ref, x_ref,
                  gt1_ref, sh2_ref, sc2_ref, wua_ref, wp_ref, ps_ref, wup_ref, wo_ref, g2_ref,
                  x1_ref, h2_ref, ext_scr, *, tiles_per_batch):
    tm = u_ref.shape[0]
    tp = tm // MIX_SPLIT
    t = pl.program_id(0) % tiles_per_batch
    parts = [slice(i * tp, (i + 1) * tp) for i in range(MIX_SPLIT)]
    o_attn = [_merge_heads([o0_ref[rs, :], o1_ref[rs, :], o2_ref[rs, :]], [l0_ref[rs, :], l1_ref[rs, :], l2_ref[rs, :]])
              for rs in parts]

    ext_scr[0:POOL_HALO, :] = jnp.where(t == 0, 0.0, uh_ref[...])
    ext_scr[POOL_HALO:, :] = u_ref[...]
    p = []
    for i, rs in enumerate(parts):
        pos = t * tm + i * tp + lax.broadcasted_iota(jnp.int32, (tp, 1), 0)
        zs = []
        for g, w in enumerate(POOL_WINDOWS):
            sl = slice(g * POOL_GROUP, (g + 1) * POOL_GROUP)
            u = u_ref[rs, sl]
            win = u
            for jj in range(1, w):
                r0 = POOL_HALO + i * tp - jj
                win = win + ext_scr[r0:r0 + tp, sl]
            cnt = jnp.minimum(pos + 1, w).astype(F32)
            zs.append(win / cnt - u)
        p.append(_pool_project(zs, wp_ref, ps_ref))

    _mix_tail(parts, o_attn, p,
              lambda rs: (gate_ref[rs, :D_MODEL].astype(F32), gate_ref[rs, D_MODEL:].astype(F32)),
              lambda rs: x_ref[rs, :],
              lambda rs: (gt1_ref[...], sh2_ref[...], sc2_ref[...]),
              wua_ref, wup_ref, wo_ref, g2_ref, x1_ref, h2_ref)


def _const_spec(shape):
    nd = len(shape)
    return pl.BlockSpec(shape, lambda *_: (0,) * nd)


def _mix_prompt(outs, lses, u, gates, x2d, mod_p, w_up_attn, w_pool, pool_scale, w_up_pool, w_out, norm_g2, seq,
                hosted):
    m = x2d.shape[0]
    tm = TM_MIX
    tpb = seq // tm
    hb = tm // POOL_HALO

    def row(width):
        return pl.BlockSpec((tm, width), lambda mi: (mi, 0))

    def mod(c):
        return pl.BlockSpec((None, 1, D_MODEL), lambda mi: (mi // tpb, 0, c))

    in_specs = ([row(D_GROUP)] * 3 + [row(HEAD_DIM)] * 3 + [
        row(D_POOL),
        pl.BlockSpec((POOL_HALO, D_POOL), lambda mi: (jnp.maximum(mi * hb - 1, 0), 0)),
        row(2 * D_MODEL), row(D_MODEL), mod(2), mod(3), mod(4),
        _const_spec((D_GROUP, D_MODEL)), _const_spec((len(POOL_WINDOWS), POOL_GROUP, POOL_GROUP)),
        _const_spec((1, D_POOL)), _const_spec((D_POOL, D_MODEL)), _const_spec((D_MODEL, D_MODEL)),
        _const_spec((1, D_MODEL))])
    s_args, s_in_specs, s_out_shape, s_out_specs, s_scratch = _stream_operands(hosted)
    kernel = _host_streams(functools.partial(_mix_p_kernel, tiles_per_batch=tpb), len(in_specs), 2, 1, hosted,
                           m // tm, lambda: pl.program_id(0))
    return pl.pallas_call(
        kernel,
        grid=(m // tm,),
        in_specs=in_specs + s_in_specs,
        out_specs=[row(D_MODEL), row(D_MODEL)] + s_out_specs,
        out_shape=[jax.ShapeDtypeStruct((m, D_MODEL), F32), jax.ShapeDtypeStruct((m, D_MODEL), BF16)] + s_out_shape,
        scratch_shapes=[pltpu.VMEM((POOL_HALO + tm, D_POOL), F32)] + s_scratch,
        compiler_params=_cparams("arbitrary"),
        name="mix_prompt",
    )(*outs, *lses, u, u, gates, x2d, mod_p, mod_p, mod_p, w_up_attn, w_pool, pool_scale, w_up_pool, w_out, norm_g2,
      *s_args)


def _mlp_kernel(h_ref, wu_ref, wd_ref, x1_ref, gt_ref, gf_ref, y_ref, acc_ref):
    f = pl.program_id(1)
    a = jnp.dot(h_ref[...], wu_ref[...], preferred_element_type=F32)
    a = jnp.square(jnp.maximum(a, 0.0)).astype(BF16)
    part = jnp.dot(a, wd_ref[...], preferred_element_type=F32)

    @pl.when(f == 0)
    def _():
        acc_ref[...] = part

    @pl.when(f > 0)
    def _():
        acc_ref[...] += part

    @pl.when(f == pl.num_programs(1) - 1)
    def _():
        x2 = x1_ref[...] + gt_ref[...] * acc_ref[...]
        y_ref[...] = _rms(x2) * gf_ref[...]


def _mlp_sample(h2, x1, mod_s, w_up, w_down, norm_gf):
    m = h2.shape[0]
    tf = TF_MLP_SAMPLE
    return pl.pallas_call(
        _mlp_kernel,
        grid=(1, D_FF // tf),
        in_specs=[pl.BlockSpec((m, D_MODEL), lambda mi, f: (0, 0)),
                  pl.BlockSpec((D_MODEL, tf), lambda mi, f: (0, f)),
                  pl.BlockSpec((tf, D_MODEL), lambda mi, f: (f, 0)),
                  pl.BlockSpec((m, D_MODEL), lambda mi, f: (0, 0)),
                  pl.BlockSpec((m, D_MODEL), lambda mi, f: (0, 5)),
                  pl.BlockSpec((1, D_MODEL), lambda mi, f: (0, 0))],
        out_specs=pl.BlockSpec((m, D_MODEL), lambda mi, f: (0, 0)),
        out_shape=jax.ShapeDtypeStruct((m, D_MODEL), F32),
        scratch_shapes=[pltpu.VMEM((m, D_MODEL), F32)],
        compiler_params=_cparams("arbitrary", "arbitrary"),
        name="mlp_sample",
    )(h2, w_up, w_down, x1, mod_s, norm_gf)


def _mlp_p_kernel(h_ref, wu_ref, wd_ref, gt_ref, gf_ref, x1_hbm, y_hbm, acc_ref, xy_ref, sems):
    mi, f = pl.program_id(0), pl.program_id(1)
    nm, nf = pl.num_programs(0), pl.num_programs(1)
    tm = acc_ref.shape[0]

    def x1_copy(i):
        return pltpu.make_async_copy(x1_hbm.at[pl.ds(i * tm, tm)], xy_ref, sems.at[0])

    def y_copy(i):
        return pltpu.make_async_copy(xy_ref, y_hbm.at[pl.ds(i * tm, tm)], sems.at[1])

    @pl.when(f == 1)
    def _():
        @pl.when(mi > 0)
        def _():
            y_copy(mi - 1).wait()

        x1_copy(mi).start()

    @pl.when(f == 0)
    def _():
        acc_ref[...] = jnp.zeros_like(acc_ref)

    a = jnp.dot(h_ref[...], wu_ref[...], preferred_element_type=F32)
    a = jnp.square(jnp.maximum(a, 0.0)).astype(BF16)
    acc_ref[...] += jnp.dot(a, wd_ref[...], preferred_element_type=F32)

    @pl.when(f == nf - 1)
    def _():
        x1_copy(mi).wait()
        x2 = xy_ref[...] + gt_ref[...] * acc_ref[...]
        xy_ref[...] = _rms(x2) * gf_ref[...]
        y_copy(mi).start()

        @pl.when(mi == nm - 1)
        def _():
            y_copy(mi).wait()


def _mlp_prompt(h2, x1, mod_p, w_up, w_down, norm_gf, seq, hosted):
    m = h2.shape[0]
    tm, tf = TM_MLP, TF_MLP
    tpb = seq // tm
    nm, nf = m // tm, D_FF // tf
    assert nf >= 3
    any_spec = pl.BlockSpec(memory_space=pl.ANY)
    s_args, s_in_specs, s_out_shape, s_out_specs, s_scratch = _stream_operands(hosted)
    kernel = _host_streams(_mlp_p_kernel, 6, 1, 3, hosted, nm * nf,
                           lambda: pl.program_id(0) * nf + pl.program_id(1))
    return pl.pallas_call(
        kernel,
        grid=(nm, nf),
        in_specs=[pl.BlockSpec((tm, D_MODEL), lambda mi, f: (mi, 0)),
                  pl.BlockSpec((D_MODEL, tf), lambda mi, f: (0, f)),
                  pl.BlockSpec((tf, D_MODEL), lambda mi, f: (f, 0)),
                  pl.BlockSpec((None, 1, D_MODEL), lambda mi, f: (mi // tpb, 0, 5)),
                  pl.BlockSpec((1, D_MODEL), lambda mi, f: (0, 0)),
                  any_spec] + s_in_specs,
        out_specs=[any_spec] + s_out_specs,
        out_shape=[jax.ShapeDtypeStruct((m, D_MODEL), F32)] + s_out_shape,
        scratch_shapes=[pltpu.VMEM((tm, D_MODEL), F32), pltpu.VMEM((tm, D_MODEL), F32),
                        pltpu.SemaphoreType.DMA((2,))] + s_scratch,
        compiler_params=_cparams("arbitrary", "arbitrary"),
        name="mlp_prompt",
    )(h2, w_up, w_down, mod_p, norm_gf, x1, *s_args)


def _inproj_s_kernel(x_ref, sh_ref, sc_ref, g_ref, w_ref, o_ref, wb_ref, h_scr):
    @pl.when(pl.program_id(0) == 0)
    def _():
        y = _rms(x_ref[...]) * g_ref[...]
        h_scr[...] = (y * (1.0 + sc_ref[...]) + sh_ref[...]).astype(BF16)

    w = w_ref[...].astype(BF16)
    wb_ref[...] = w
    o_ref[...] = jnp.dot(h_scr[...], w, preferred_element_type=F32)


def _inproj_sample(x_s, mod_s, norm_g, w_in):
    m = x_s.shape[0]
    tn = D_ATTN
    return pl.pallas_call(
        _inproj_s_kernel,
        grid=(D_IN // tn,),
        in_specs=[pl.BlockSpec((m, D_MODEL), lambda j: (0, 0)),
                  pl.BlockSpec((m, D_MODEL), lambda j: (0, 0)),
                  pl.BlockSpec((m, D_MODEL), lambda j: (0, 1)),
                  pl.BlockSpec((1, D_MODEL), lambda j: (0, 0)),
                  pl.BlockSpec((D_MODEL, tn), lambda j: (0, j))],
        out_specs=[pl.BlockSpec((m, tn), lambda j: (0, j)),
                   pl.BlockSpec((D_MODEL, tn), lambda j: (0, j))],
        out_shape=[jax.ShapeDtypeStruct((m, D_IN), F32), jax.ShapeDtypeStruct((D_MODEL, D_IN), BF16)],
        scratch_shapes=[pltpu.VMEM((m, D_MODEL), BF16)],
        compiler_params=_cparams("arbitrary"),
        name="inproj_sample",
    )(x_s, mod_s, mod_s, norm_g, w_in)


def _attn_s_kernel(bias_ref, q_ref, kn_ref, vn_ref, k0_ref, v0_ref, k1_ref, v1_ref, k2_ref, v2_ref,
                   o_ref):
    k_refs = (k0_ref, k1_ref, k2_ref)
    v_refs = (v0_ref, v1_ref, v2_ref)

    bs, pairs = k0_ref.shape[0], k0_ref.shape[1]
    k_refs = [r.reshape(bs, pairs, KV_ROWS, HEAD_DIM) for r in k_refs]
    v_refs = [r.reshape(bs, pairs, KV_ROWS, HEAD_DIM) for r in v_refs]

    def both(x):
        return x, pltpu.roll(x, HEADS_PER_GROUP, axis=0)

    def body(s, carry):
        outs, lses = [], []
        for g in range(N_GROUPS):
            q = q_ref[s, g]
            kn = kn_ref[s, g]
            vn = vn_ref[s, g]
            kt = k_refs[g][s]
            vt = v_refs[g][s]
            sh = jnp.sum(kt * q[None], axis=-1, keepdims=True) * SCALE + bias_ref[g]
            sn = jnp.sum(kn * q, axis=-1, keepdims=True) * SCALE
            m_a, m_b = both(jnp.max(sh, axis=0))
            mx = jnp.maximum(jnp.maximum(m_a, m_b), sn)
            p = jnp.exp(sh - mx[None])
            pn = jnp.exp(sn - mx)
            d_a, d_b = both(jnp.sum(p, axis=0))
            n_a, n_b = both(jnp.sum(p * vt, axis=0))
            den = d_a + d_b + pn
            outs.append((n_a + n_b + pn * vn) / den)
            lses.append(mx + jnp.log(den))
        mx = jnp.maximum(jnp.maximum(lses[0], lses[1]), lses[2])
        e = [jnp.exp(l - mx) for l in lses]
        tot = e[0] + e[1] + e[2]
        o_ref[s] = ((e[0] * outs[0] + e[1] * outs[1] + e[2] * outs[2]) / tot)[0:HEADS_PER_GROUP, :]
        return carry

    lax.fori_loop(0, q_ref.shape[0], body, 0)


def _attn_sample(bias_s, q3, kn3, vn3, caches, cast_weight):
    nb = q3.shape[0]
    bs = BS_ATTN
    slab = pl.BlockSpec((cast_weight.shape[0] // (nb // bs), cast_weight.shape[1]), lambda i: (i, 0))

    def twice(t3):
        t = t3.reshape(nb, N_GROUPS, HEADS_PER_GROUP, HEAD_DIM)
        return jnp.concatenate([t, t], axis=2)

    tok = pl.BlockSpec((bs, N_GROUPS, KV_ROWS, HEAD_DIM), lambda i: (i, 0, 0, 0))
    in_specs = [_const_spec((N_GROUPS, BAND // 2, KV_ROWS, HEAD_DIM)), tok, tok, tok]
    args = [bias_s, twice(q3), twice(kn3), twice(vn3)]
    for c, (_, dil) in zip(caches, ATTN_GROUPS):
        cv = c.reshape(nb, BAND // 2, 2, dil * 2, HEADS_PER_GROUP, HEAD_DIM)
        for kv in range(2):
            in_specs.append(pl.BlockSpec((bs, BAND // 2, 2, None, HEADS_PER_GROUP, HEAD_DIM),
                                         lambda i, kv=kv: (i, 0, 0, kv, 0, 0)))
            args.append(cv)
    return pl.pallas_call(
        _with_cast_passenger(_attn_s_kernel, len(in_specs), 1),
        grid=(nb // bs,),
        in_specs=in_specs + [slab],
        out_specs=[pl.BlockSpec((bs, HEADS_PER_GROUP, HEAD_DIM), lambda i: (i, 0, 0)), slab],
        out_shape=[jax.ShapeDtypeStruct((nb, HEADS_PER_GROUP, HEAD_DIM), F32),
                   jax.ShapeDtypeStruct(cast_weight.shape, BF16)],
        compiler_params=_cparams("arbitrary"),
        name="attn_sample",
    )(*args, cast_weight)


def _mix_s_kernel(o_ref, u_ref, hist_ref, ga_ref, gb_ref, x_ref, gt1_ref, sh2_ref, sc2_ref,
                  wua_ref, wp_ref, ps_ref, wup_ref, wo_ref, g2_ref, x1_ref, h2_ref, np_ref):
    u = u_ref[...]
    zs = []
    for g, w in enumerate(POOL_WINDOWS):
        sl = slice(g * POOL_GROUP, (g + 1) * POOL_GROUP)
        win = u[:, sl]
        for jj in range(1, w):
            win = win + hist_ref[:, POOL_HIST - jj, sl]
        zs.append(win / float(w) - u[:, sl])
    p = _pool_project(zs, wp_ref, ps_ref)
    np_ref[:, 0:POOL_HIST - 1, :] = hist_ref[:, 1:POOL_HIST, :]
    np_ref[:, POOL_HIST - 1, :] = u
    _mix_tail([slice(None)], [o_ref[...]], [p],
              lambda rs: (jax.nn.sigmoid(ga_ref[rs, :]), jax.nn.sigmoid(gb_ref[rs, :])),
              lambda rs: x_ref[rs, :],
              lambda rs: (gt1_ref[rs, :], sh2_ref[rs, :], sc2_ref[rs, :]),
              wua_ref, wup_ref, wo_ref, g2_ref, x1_ref, h2_ref)


def _mix_sample(o_attn, proj_s, hist, x_s, mod_s, w_up_attn, w_pool, pool_scale, w_up_pool, w_out, norm_g2):
    m = x_s.shape[0]

    def cols(width, c):
        return pl.BlockSpec((m, width), lambda i: (0, c))

    u0 = 3 * D_ATTN
    u = lax.slice_in_dim(proj_s, u0, u0 + D_POOL, axis=1)
    ga = lax.slice_in_dim(proj_s, u0 + D_POOL, u0 + D_POOL + D_MODEL, axis=1)
    gb = lax.slice_in_dim(proj_s, u0 + D_POOL + D_MODEL, D_IN, axis=1)
    in_specs = [cols(D_GROUP, 0), cols(D_POOL, 0), _const_spec((m, POOL_HIST, D_POOL)),
                cols(D_MODEL, 0), cols(D_MODEL, 0),
                cols(D_MODEL, 0), cols(D_MODEL, 2), cols(D_MODEL, 3), cols(D_MODEL, 4),
                _const_spec((D_GROUP, D_MODEL)), _const_spec((len(POOL_WINDOWS), POOL_GROUP, POOL_GROUP)),
                _const_spec((1, D_POOL)), _const_spec((D_POOL, D_MODEL)), _const_spec((D_MODEL, D_MODEL)),
                _const_spec((1, D_MODEL))]
    return pl.pallas_call(
        _mix_s_kernel,
        grid=(1,),
        in_specs=in_specs,
        out_specs=[_const_spec((m, D_MODEL)), _const_spec((m, D_MODEL)), _const_spec((m, POOL_HIST, D_POOL))],
        out_shape=[jax.ShapeDtypeStruct((m, D_MODEL), F32), jax.ShapeDtypeStruct((m, D_MODEL), BF16),
                   jax.ShapeDtypeStruct((m, POOL_HIST, D_POOL), F32)],
        compiler_params=_cparams("arbitrary"),
        name="mix_sample",
    )(o_attn, u, hist, ga, gb, x_s, mod_s, mod_s, mod_s, w_up_attn, w_pool, pool_scale, w_up_pool, w_out, norm_g2)


class _ShiftStream:
    def __init__(self, cache, new, out, buf, sems, rows, ns):
        self.cache, self.new, self.out, self.buf, self.sems = cache, new, out, buf, sems
        self.rows, self.ns = rows, ns
        self.slots = buf.shape[0]
        nb, w = cache.shape[1], cache.shape[2]
        assert w % rows == 0 and nb % ns == 0 and (ns == 1 or rows == w)
        self.cps = w // rows
        self.n_chunks = (nb // ns) * self.cps

    def _where(self, k):
        return (k // self.cps) * self.ns, (k % self.cps) * self.rows, k % self.slots

    def _body_in(self, k):
        b0, w0, slot = self._where(k)
        return pltpu.make_async_copy(self.cache.at[0, pl.ds(b0, self.ns), pl.ds(w0 + 1, self.rows - 1)],
                                     self.buf.at[slot, :, pl.ds(0, self.rows - 1)], self.sems.at[0, slot])

    def _next_in(self, k):
        b0, w0, slot = self._where(k)
        return pltpu.make_async_copy(self.cache.at[0, pl.ds(b0, self.ns), pl.ds(w0 + self.rows, 1)],
                                     self.buf.at[slot, :, pl.ds(self.rows - 1, 1)], self.sems.at[1, slot])

    def _new_in(self, k):
        b0, _, slot = self._where(k)
        return pltpu.make_async_copy(self.new.at[pl.ds(b0, self.ns)],
                                     self.buf.at[slot, :, self.rows - 1], self.sems.at[1, slot])

    def _out(self, k):
        b0, w0, slot = self._where(k)
        return pltpu.make_async_copy(self.buf.at[slot],
                                     self.out.at[0, pl.ds(b0, self.ns), pl.ds(w0, self.rows)], self.sems.at[2, slot])

    def _last_row(self, k, act):
        if isinstance(k, int) or self.cps == 1:
            at_end = self.cps == 1 or k % self.cps == self.cps - 1
            getattr(self._new_in(k) if at_end else self._next_in(k), act)()
            return
        at_end = k % self.cps == self.cps - 1

        @pl.when(at_end)
        def _():
            getattr(self._new_in(k), act)()

        @pl.when(jnp.logical_not(at_end))
        def _():
            getattr(self._next_in(k), act)()

    def start_in(self, k):
        self._body_in(k).start()
        self._last_row(k, "start")

    def wait_in(self, k):
        self._body_in(k).wait()
        self._last_row(k, "wait")

    def start_out(self, k):
        self._out(k).start()

    def wait_out(self, k):
        self._out(k).wait()

    def step(self, k, lag):
        self.wait_in(k)
        self.start_out(k)

        @pl.when(k - lag >= 0)
        def _():
            self.wait_out(k - lag)

        @pl.when(k - lag + self.slots < self.n_chunks)
        def _():
            self.start_in(k - lag + self.slots)

    def tick(self, s, n_steps):
        n = self.n_chunks
        per = -(-n // n_steps)
        stride = max(n_steps // n, 1)
        assert self.slots >= 2 * per and n >= self.slots

        @pl.when(s == 0)
        def _():
            for k in range(self.slots - per):
                self.start_in(k)

        for i in range(per):
            k = (s // stride) * per + i

            @pl.when((s % stride == 0) & (k < n))
            def _(k=k):
                self.step(k, per)

        @pl.when(s == n_steps - 1)
        def _():
            for k in range(n - per, n):
                self.wait_out(k)


class _Hosted(NamedTuple):
    cache: jax.Array
    new: jax.Array
    rows: int
    ns: int
    slots: int


def _host_streams(body, n_in, n_out, n_scr, hosted, n_steps, step_index):
    plans = [(h.rows, h.ns) for h in hosted]
    ns_ = len(plans)

    def kernel(*refs):
        refs = list(refs)
        ins, s_in = refs[:n_in], refs[n_in:n_in + 2 * ns_]
        o0 = n_in + 2 * ns_
        outs, s_out = refs[o0:o0 + n_out], refs[o0 + n_out:o0 + n_out + ns_]
        c0 = o0 + n_out + ns_
        scr, s_scr = refs[c0:c0 + n_scr], refs[c0 + n_scr:]
        s = step_index()
        for i, (rows, ns) in enumerate(plans):
            _ShiftStream(s_in[2 * i], s_in[2 * i + 1], s_out[i], s_scr[2 * i], s_scr[2 * i + 1],
                         rows, ns).tick(s, n_steps)
        body(*ins, *outs, *scr)

    return kernel


def _stream_operands(hosted):
    any_spec = pl.BlockSpec(memory_space=pl.ANY)
    args = [a for h in hosted for a in (h.cache, h.new)]
    out_shape = [jax.ShapeDtypeStruct(h.cache.shape, h.cache.dtype) for h in hosted]
    scratch = [s for h in hosted for s in (
        pltpu.VMEM((h.slots, h.ns, h.rows, 2, HEADS_PER_GROUP, HEAD_DIM), F32),
        pltpu.SemaphoreType.DMA((3, h.slots)))]
    return args, [any_spec] * len(args), out_shape, [any_spec] * len(hosted), scratch


def _alibi_slopes():
    h = jnp.arange(1, N_HEADS + 1, dtype=F32)
    return jnp.exp2(-ALIBI_MAX_BIAS * h / N_HEADS)


def _prompt_bias(g):
    dil = ATTN_GROUPS[g][1]
    slopes = _alibi_slopes()[g * HEADS_PER_GROUP:(g + 1) * HEADS_PER_GROUP]
    a = jnp.arange(BAND)[:, None]
    b = jnp.arange(2 * BAND)[None, :]
    dist = a - b + BAND
    valid = (dist >= 0) & (dist <= BAND)
    bias = -slopes[:, None, None] * (dist * dil).astype(F32)
    return jnp.where(valid[None], bias, NEG)


def _sample_bias():
    slopes = _alibi_slopes().reshape(N_GROUPS, 1, HEADS_PER_GROUP, 1)
    dil = jnp.array([d for _, d in ATTN_GROUPS], F32).reshape(N_GROUPS, 1, 1, 1)
    back = (BAND - jnp.arange(BAND, dtype=F32)).reshape(1, BAND, 1, 1)
    bias = jnp.broadcast_to(-slopes * (back * dil), (N_GROUPS, BAND, HEADS_PER_GROUP, HEAD_DIM))
    return bias.reshape(N_GROUPS, BAND // 2, KV_ROWS, HEAD_DIM)


def kernel(x_prompt, x_sample, c_prompt, c_sample, cache_kv_w128, cache_kv_w512, cache_kv_w2048, state_pool,
           norm_mix_g, w_ada, b_ada, w_in, w_up_attn, w_pool, pool_scale, w_up_pool, w_out, norm_mlp_g,
           w_mlp_up, w_mlp_down, norm_final_g):
    batch, seq, _ = x_prompt.shape
    nb = x_sample.shape[0]
    depth = w_in.shape[0]
    assert depth == 1 and x_sample.shape[1] == 1
    caches = (cache_kv_w128, cache_kv_w512, cache_kv_w2048)
    for c, (w, dil) in zip(caches, ATTN_GROUPS):
        assert c.shape[2] == w == BAND * dil

    w_ua_b = w_up_attn[0].astype(BF16)
    w_pool_b = w_pool[0].astype(BF16)
    w_up_b = w_up_pool[0].astype(BF16)
    g1 = norm_mix_g[0].reshape(1, D_MODEL)
    g2 = norm_mlp_g[0].reshape(1, D_MODEL)
    gf = norm_final_g.reshape(1, D_MODEL)
    ps = pool_scale[0].reshape(1, D_POOL)

    mod = _ada(jnp.concatenate([c_sample, c_prompt], axis=0), w_ada[0], b_ada[0])
    mod_s = mod
    mod_p = mod[nb:].reshape(batch, 1, 6 * D_MODEL)

    xs = x_sample.reshape(nb, D_MODEL)
    proj_s, w_in_b = _inproj_sample(xs, mod_s, g1, w_in[0])
    q3 = proj_s[:, 0:D_ATTN].reshape(nb, N_HEADS, HEAD_DIM)
    kn3 = proj_s[:, D_ATTN:2 * D_ATTN].reshape(nb, N_HEADS, HEAD_DIM)
    vn3 = proj_s[:, 2 * D_ATTN:3 * D_ATTN].reshape(nb, N_HEADS, HEAD_DIM)
    o_s, w_out_b = _attn_sample(_sample_bias(), q3, kn3, vn3, [c[0] for c in caches], w_out[0])
    new_rows = [jnp.stack([kn3[:, g * HEADS_PER_GROUP:(g + 1) * HEADS_PER_GROUP],
                           vn3[:, g * HEADS_PER_GROUP:(g + 1) * HEADS_PER_GROUP]], axis=1)
                for g in range(N_GROUPS)]
    x1_s, h2_s, pool_s = _mix_sample(o_s.reshape(nb, D_GROUP), proj_s, state_pool[0], xs, mod_s,
                                     w_ua_b, w_pool_b, ps, w_up_b, w_out_b, g2)

    x2d = x_prompt.reshape(batch * seq, D_MODEL)
    *qkv_sub, u_p, gates, kv2, kv1, kv0, kv_s1 = _inproj_prompt(
        x2d, mod_p, g1, w_in_b, batch, seq,
        (_Hosted(caches[1], new_rows[1], rows=512, ns=1, slots=3),))
    o0, l0, w_md_b = _attn_prompt_group(qkv_sub[0], _prompt_bias(0), 0, batch, seq, cast_weight=w_mlp_down[0])
    o1, l1, w_mu_b = _attn_prompt_group(qkv_sub[1], _prompt_bias(1), 1, batch, seq, cast_weight=w_mlp_up[0])
    o2, l2, kv_s0 = _attn_prompt_group(qkv_sub[2], _prompt_bias(2), 2, batch, seq,
                                       hosted=(_Hosted(caches[0], new_rows[0], rows=128, ns=8, slots=4),))
    outs, lses = [o0, o1, o2], [l0, l1, l2]
    y_s = _mlp_sample(h2_s, x1_s, mod_s, w_mu_b, w_md_b, gf)
    x1_p, h2_p = _mix_prompt(
        outs, lses, u_p, gates, x2d, mod_p, w_ua_b, w_pool_b, ps, w_up_b, w_out_b, g2, seq, ())
    y_p, kv_s2 = _mlp_prompt(
        h2_p, x1_p, mod_p, w_mu_b, w_md_b, gf, seq,
        (_Hosted(caches[2], new_rows[2], rows=1024, ns=1, slots=3),))

    pool_p = u_p.reshape(batch, seq, D_POOL)[:, seq - POOL_HIST:][None]
    kv0, kv1, kv2 = (a.reshape(1, batch, -1, 2, HEADS_PER_GROUP, HEAD_DIM) for a in (kv0, kv1, kv2))
    return (y_p.reshape(batch, seq, D_MODEL), y_s.reshape(nb, 1, D_MODEL), kv0, kv1, kv2, pool_p,
            kv_s0, kv_s1, kv_s2, pool_s[None])
```
